```python
import functools
import numpy as np
import jax
import jax.numpy as jnp
from jax import lax

D_MODEL = 1024
BATCH = 2
SEQ = 8192
DEPTH = 1
DEC_BATCH = 128
DEC_SEQ = 1
PAST_LEN = 2048
PAGE_SIZE = 128

D_CONV = 1024
CONV_WIDTH = 3
N_HEADS = 8
HEAD_DIM = 128
N_KV_HEADS = 2
GROUP = N_HEADS // N_KV_HEADS
IDX_HEADS = 4
IDX_DIM = 64
TOP_K_MAX = 256
Q_BLOCK = 128
N_MEM = 256
MEM_HEADS = 4
MEM_HEAD_DIM = D_MODEL // MEM_HEADS
D_FF = -(-8 * D_MODEL // (3 * 256)) * 256
EPS = 1e-6
NEG_INF = -1e30
IDX_SCALE = (IDX_HEADS * IDX_DIM) ** -0.5
MIX_WIDTHS = (D_CONV, D_CONV, D_CONV, N_HEADS * HEAD_DIM, N_KV_HEADS * HEAD_DIM, N_KV_HEADS * HEAD_DIM,
              IDX_HEADS * IDX_DIM, IDX_DIM, IDX_HEADS, D_MODEL, D_MODEL)
MIX_IN = sum(MIX_WIDTHS)

kernel_name = "hybrid_conv_dsa_gated_decoder_step"


def rms_norm(x, g):
    xf = x.astype(jnp.float32)
    y = xf * lax.rsqrt(jnp.mean(xf * xf, axis=-1, keepdims=True) + EPS)
    return (y * g.astype(jnp.float32)).astype(x.dtype)


def gather_rows(src, idx):
    return jax.vmap(lambda s, i: s[i])(src, idx)


def short_conv(u, prev, conv_w):
    T = u.shape[1]
    upad = jnp.concatenate([prev.astype(u.dtype), u], axis=1)
    y = sum(upad[:, j:j + T] * conv_w[j] for j in range(CONV_WIDTH))
    return y, upad[:, T:]


def indexer_topk(iq, iw, ik, q_pos, top_k):
    s = jnp.einsum('bthd,bsd->bths', iq.astype(jnp.float32), ik.astype(jnp.float32))
    score = jnp.einsum('bths,bth->bts', jax.nn.relu(s), iw.astype(jnp.float32)) * IDX_SCALE
    key_pos = jnp.arange(ik.shape[1])
    causal = key_pos[None, None, :] <= q_pos[None, :, None]
    score = jnp.where(causal, score, NEG_INF)
    _, idx = lax.top_k(score, top_k)
    valid = idx <= q_pos[None, :, None]
    return idx, valid


def sparse_attend(q, ks, vs, valid):
    B, T = q.shape[:2]
    qg = q.reshape(B, T, N_KV_HEADS, GROUP, HEAD_DIM).astype(jnp.float32)
    s = jnp.einsum('btkgd,btjkd->btkgj', qg, ks.astype(jnp.float32)) * HEAD_DIM ** -0.5
    s = jnp.where(valid[:, :, None, None, :], s, NEG_INF)
    p = jax.nn.softmax(s, axis=-1)
    o = jnp.einsum('btkgj,btjkd->btkgd', p, vs.astype(jnp.float32))
    return o.reshape(B, T, N_HEADS * HEAD_DIM).astype(q.dtype)


def prompt_sparse_attention(q, k, v, iq, ik, iw):
    B, S = q.shape[:2]
    top_k = min(TOP_K_MAX, S // 4)
    n_blocks = S // Q_BLOCK

    def block(i):
        start = i * Q_BLOCK
        qb = lax.dynamic_slice_in_dim(q, start, Q_BLOCK, axis=1)
        iqb = lax.dynamic_slice_in_dim(iq, start, Q_BLOCK, axis=1)
        iwb = lax.dynamic_slice_in_dim(iw, start, Q_BLOCK, axis=1)
        pos = start + jnp.arange(Q_BLOCK)
        idx, valid = indexer_topk(iqb, iwb, ik, pos, top_k)
        ks = gather_rows(k, idx)
        vs = gather_rows(v, idx)
        return sparse_attend(qb, ks, vs, valid)

    out = lax.map(block, jnp.arange(n_blocks))
    return out.transpose(1, 0, 2, 3).reshape(B, S, N_HEADS * HEAD_DIM)


def sample_sparse_attention(q, k_new, v_new, iq, ik_new, iw, cache_k, cache_v, cache_ik, page_table):
    Bd, T = q.shape[:2]
    page_size = cache_k.shape[1]
    past = page_table.shape[1] * page_size
    top_k = min(TOP_K_MAX, (past + T) // 4)
    ik_past = cache_ik[page_table].reshape(Bd, past, IDX_DIM)
    ik_all = jnp.concatenate([ik_past.astype(ik_new.dtype), ik_new], axis=1)
    q_pos = past + jnp.arange(T)
    idx, valid = indexer_topk(iq, iw, ik_all, q_pos, top_k)
    in_past = (idx < past)[..., None, None]
    pidx = jnp.minimum(idx, past - 1)
    phys = gather_rows(page_table, pidx // page_size)
    slot = pidx % page_size
    nidx = jnp.clip(idx - past, 0, T - 1)
    ks = jnp.where(in_past, cache_k[phys, slot].astype(k_new.dtype), gather_rows(k_new, nidx))
    vs = jnp.where(in_past, cache_v[phys, slot].astype(v_new.dtype), gather_rows(v_new, nidx))
    return sparse_attend(q, ks, vs, valid)


def memory_kv(mem, g, w_mk, w_mv):
    B, M, _ = mem.shape
    m = rms_norm(mem, g)
    k = jnp.einsum('bmd,de->bme', m, w_mk).reshape(B, M, MEM_HEADS, MEM_HEAD_DIM)
    v = jnp.einsum('bmd,de->bme', m, w_mv).reshape(B, M, MEM_HEADS, MEM_HEAD_DIM)
    return k, v


def memory_attend(h, mem_k, mem_v, w_mq, w_mo):
    B, T, _ = h.shape
    q = jnp.einsum('btd,de->bte', h, w_mq).reshape(B, T, MEM_HEADS, MEM_HEAD_DIM)
    s = jnp.einsum('bthd,bmhd->bhtm', q.astype(jnp.float32), mem_k.astype(jnp.float32)) * MEM_HEAD_DIM ** -0.5
    p = jax.nn.softmax(s, axis=-1)
    o = jnp.einsum('bhtm,bmhd->bthd', p, mem_v.astype(jnp.float32)).reshape(B, T, D_MODEL).astype(h.dtype)
    return jnp.einsum('btd,de->bte', o, w_mo)


def trunk_layer(x, conv_prev, attend, mem_k, mem_v, lw):
    (g_mix, w_in, conv_w, w_conv_out, w_attn_out, w_o,
     g_mem, w_mq, w_mo, g_ffn, w_gate, w_up, w_down) = lw
    B, T, _ = x.shape
    h = rms_norm(x, g_mix)
    z = jnp.einsum('btd,de->bte', h, w_in)
    splits = [int(s) for s in np.cumsum(MIX_WIDTHS)[:-1]]
    (c_in, c_b, c_c, q, k, v, iq, ik, iw, gate_a, gate_b) = jnp.split(z, splits, axis=-1)
    conv_out, conv_state = short_conv(c_c * c_in, conv_prev, conv_w)
    out_a = jnp.einsum('btc,cd->btd', c_b * conv_out, w_conv_out)
    q = q.reshape(B, T, N_HEADS, HEAD_DIM)
    k = k.reshape(B, T, N_KV_HEADS, HEAD_DIM)
    v = v.reshape(B, T, N_KV_HEADS, HEAD_DIM)
    iq = iq.reshape(B, T, IDX_HEADS, IDX_DIM)
    attn = attend(q, k, v, iq, ik, iw)
    out_b = jnp.einsum('bte,ed->btd', attn, w_attn_out)
    merged = jax.nn.sigmoid(gate_a) * out_a + jax.nn.sigmoid(gate_b) * out_b
    x = x + jnp.einsum('btd,de->bte', merged, w_o)
    x = x + memory_attend(rms_norm(x, g_mem), mem_k, mem_v, w_mq, w_mo)
    h = rms_norm(x, g_ffn)
    f = jax.nn.silu(jnp.einsum('btd,df->btf', h, w_gate)) * jnp.einsum('btd,df->btf', h, w_up)
    x = x + jnp.einsum('btf,fd->btd', f, w_down)
    return x, conv_state, k, v, ik


def setup_inputs(seed: int = 0) -> dict:
    key = jax.random.key(seed)
    ks = jax.random.split(key, 32)
    n_pages = PAST_LEN // PAGE_SIZE
    n_used = DEC_BATCH * n_pages
    n_phys = (5 * n_used + 3) // 4
    f32 = jnp.float32

    def nrm(k, shape, scale=1.0):
        return jax.random.normal(k, shape, f32) * scale

    def gain(k, shape):
        return 1.0 + 0.05 * jax.random.normal(k, shape, f32)

    page_table = jax.random.permutation(ks[0], n_phys)[:n_used].reshape(DEC_BATCH, n_pages).astype(jnp.int32)
    return {
        "x_prompt": nrm(ks[1], (BATCH, SEQ, D_MODEL)),
        "x_sample": nrm(ks[2], (DEC_BATCH, DEC_SEQ, D_MODEL)),
        "mem_prompt": nrm(ks[3], (BATCH, N_MEM, D_MODEL)),
        "cache_k": nrm(ks[4], (DEPTH, n_phys, PAGE_SIZE, N_KV_HEADS, HEAD_DIM)),
        "cache_v": nrm(ks[5], (DEPTH, n_phys, PAGE_SIZE, N_KV_HEADS, HEAD_DIM)),
        "cache_idx_k": nrm(ks[6], (DEPTH, n_phys, PAGE_SIZE, IDX_DIM)),
        "cache_mem_k": nrm(ks[7], (DEPTH, DEC_BATCH, N_MEM, MEM_HEADS, MEM_HEAD_DIM)),
        "cache_mem_v": nrm(ks[8], (DEPTH, DEC_BATCH, N_MEM, MEM_HEADS, MEM_HEAD_DIM)),
        "state_conv": nrm(ks[9], (DEPTH, DEC_BATCH, CONV_WIDTH - 1, D_CONV)),
        "page_table": page_table,
        "g_mix": gain(ks[10], (DEPTH, D_MODEL)),
        "w_in": nrm(ks[11], (DEPTH, D_MODEL, MIX_IN), D_MODEL ** -0.5),
        "conv_w": nrm(ks[12], (DEPTH, CONV_WIDTH, D_CONV), CONV_WIDTH ** -0.5),
        "w_conv_out": nrm(ks[13], (DEPTH, D_CONV, D_MODEL), D_CONV ** -0.5),
        "w_attn_out": nrm(ks[14], (DEPTH, N_HEADS * HEAD_DIM, D_MODEL), (N_HEADS * HEAD_DIM) ** -0.5),
        "w_o": nrm(ks[15], (DEPTH, D_MODEL, D_MODEL), D_MODEL ** -0.5),
        "g_mem": gain(ks[16], (DEPTH, D_MODEL)),
        "g_mem_kv": gain(ks[17], (DEPTH, D_MODEL)),
        "w_mq": nrm(ks[18], (DEPTH, D_MODEL, D_MODEL), D_MODEL ** -0.5),
        "w_mk": nrm(ks[19], (DEPTH, D_MODEL, D_MODEL), D_MODEL ** -0.5),
        "w_mv": nrm(ks[20], (DEPTH, D_MODEL, D_MODEL), D_MODEL ** -0.5),
        "w_mo": nrm(ks[21], (DEPTH, D_MODEL, D_MODEL), D_MODEL ** -0.5),
        "g_ffn": gain(ks[22], (DEPTH, D_MODEL)),
        "w_gate": nrm(ks[23], (DEPTH, D_MODEL, D_FF), D_MODEL ** -0.5),
        "w_up": nrm(ks[24], (DEPTH, D_MODEL, D_FF), D_MODEL ** -0.5),
        "w_down": nrm(ks[25], (DEPTH, D_FF, D_MODEL), D_FF ** -0.5),
        "g_final": gain(ks[26], (D_MODEL,)),
    }


def reference(x_prompt, x_sample, mem_prompt, cache_k, cache_v, cache_idx_k, cache_mem_k, cache_mem_v,
              state_conv, page_table, g_mix, w_in, conv_w, w_conv_out, w_attn_out, w_o, g_mem, g_mem_kv,
              w_mq, w_mk, w_mv, w_mo, g_ffn, w_gate, w_up, w_down, g_final):
    xp, xs = x_prompt, x_sample
    kp_l, vp_l, ikp_l, convp_l, mkp_l, mvp_l = [], [], [], [], [], []
    ks_l, vs_l, iks_l, convs_l = [], [], [], []
    for l in range(DEPTH):
        lw = (g_mix[l], w_in[l], conv_w[l], w_conv_out[l], w_attn_out[l], w_o[l],
              g_mem[l], w_mq[l], w_mo[l], g_ffn[l], w_gate[l], w_up[l], w_down[l])
        mk_p, mv_p = memory_kv(mem_prompt, g_mem_kv[l], w_mk[l], w_mv[l])
        conv_prev_p = jnp.zeros((xp.shape[0], CONV_WIDTH - 1, D_CONV), xp.dtype)
        xp, conv_p, k_p, v_p, ik_p = trunk_layer(xp, conv_prev_p, prompt_sparse_attention, mk_p, mv_p, lw)
        attend_s = functools.partial(sample_sparse_attention, cache_k=cache_k[l], cache_v=cache_v[l],
                                     cache_ik=cache_idx_k[l], page_table=page_table)
        xs, conv_s, k_s, v_s, ik_s = trunk_layer(xs, state_conv[l], attend_s, cache_mem_k[l], cache_mem_v[l], lw)
        kp_l.append(k_p); vp_l.append(v_p); ikp_l.append(ik_p); convp_l.append(conv_p)
        mkp_l.append(mk_p); mvp_l.append(mv_p)
        ks_l.append(k_s); vs_l.append(v_s); iks_l.append(ik_s); convs_l.append(conv_s)
    y_prompt = rms_norm(xp, g_final)
    y_sample = rms_norm(xs, g_final)
    return (y_prompt, y_sample,
            jnp.stack(kp_l), jnp.stack(vp_l), jnp.stack(ikp_l), jnp.stack(convp_l),
            jnp.stack(mkp_l), jnp.stack(mvp_l),
            jnp.stack(ks_l), jnp.stack(vs_l), jnp.stack(iks_l), jnp.stack(convs_l))
```

```python
import functools

import jax
import jax.numpy as jnp
from jax import lax
from jax.experimental import pallas as pl
from jax.experimental.pallas import tpu as pltpu

F32 = jnp.float32
BF16 = jnp.bfloat16
I32 = jnp.int32

D_MODEL = 1024
D_CONV = 1024
CONV_WIDTH = 3
N_HEADS = 8
HEAD_DIM = 128
N_KV_HEADS = 2
GROUP = N_HEADS // N_KV_HEADS
IDX_HEADS = 4
IDX_DIM = 64
TOP_K_MAX = 256
MEM_HEADS = 4
MEM_HEAD_DIM = D_MODEL // MEM_HEADS
D_FF = 2816
EPS = 1e-6
NEG_INF = -1e30
IDX_SCALE = (IDX_HEADS * IDX_DIM) ** -0.5
ATTN_SCALE = HEAD_DIM ** -0.5
MEM_SCALE = MEM_HEAD_DIM ** -0.5

LANES = 128
SUBLANES = 8
MXU_ROWS = 16
VMEM_LIMIT_BYTES = 56 * 1024 * 1024

COL_CIN = 0
COL_CB = COL_CIN + D_CONV
COL_CC = COL_CB + D_CONV
COL_Q = COL_CC + D_CONV
COL_K = COL_Q + N_HEADS * HEAD_DIM
COL_V = COL_K + N_KV_HEADS * HEAD_DIM
COL_IQ = COL_V + N_KV_HEADS * HEAD_DIM
COL_IK = COL_IQ + IDX_HEADS * IDX_DIM
COL_IW = COL_IK + IDX_DIM
IDX_RAW_END = COL_IW + IDX_HEADS
COL_GA = -(-IDX_RAW_END // LANES) * LANES
COL_GB = COL_GA + D_MODEL
W_COLS = COL_GB + D_MODEL

Q_BLOCK = 128
KEY_CHUNK = 512
FF_CHUNK = D_FF // 2
INT_MIN = -2 ** 31


def _rms(x, g):
    return x * lax.rsqrt(jnp.mean(x * x, axis=-1, keepdims=True) + EPS) * g


def _dot(a, b):
    return jnp.dot(a, b, preferred_element_type=F32)


def _dot_nt(a, b):
    return lax.dot_general(a, b, (((1,), (1,)), ((), ())), preferred_element_type=F32)


def _sigmoid(x):
    return 1.0 / (1.0 + jnp.exp(-x))


def _key_to_float(ukey):
    skey = ukey ^ INT_MIN
    bits = jnp.where(skey < 0, skey ^ 0x7FFFFFFF, skey)
    return lax.bitcast_convert_type(bits, F32)


def _kth_largest(count_ge, top_k, shape):
    def body(b, ukey):
        cand = ukey | jnp.left_shift(jnp.int32(1), 31 - b)
        ok = count_ge(_key_to_float(cand)) >= float(top_k)
        return jnp.where(ok, cand, ukey)

    ukey = lax.fori_loop(0, 32, body, jnp.zeros(shape, I32))
    return _key_to_float(ukey)


def _mixer_in_kernel(is_prompt, tm, *refs):
    if is_prompt:
        (x_ref, g_ref, w_ref, cw_ref, wco_ref,
         q_ref, kf_ref, vf_ref, kb_ref, vb_ref, iq_ref, ikf_ref, ikb_ref, iw_ref, oa_ref, sgb_ref,
         cs_ref, ubuf) = refs
    else:
        (x_ref, g_ref, w_ref, cw_ref, wco_ref, s0_ref, s1_ref,
         q_ref, kf_ref, vf_ref, kb_ref, vb_ref, iq_ref, ikf_ref, ikb_ref, iw_ref, oa_ref, sgb_ref,
         u_ref) = refs

    h = _rms(x_ref[...], g_ref[...]).astype(BF16)

    def proj(lo, hi):
        return _dot(h, w_ref[:, lo:hi])

    u = proj(COL_CC, COL_CC + D_CONV) * proj(COL_CIN, COL_CIN + D_CONV)
    cw = cw_ref[...]
    if is_prompt:
        @pl.when(pl.program_id(1) == 0)
        def _():
            ubuf[0:SUBLANES, :] = jnp.zeros((SUBLANES, D_CONV), F32)

        ubuf[SUBLANES:SUBLANES + tm, :] = u
        conv = (ubuf[SUBLANES - 2:SUBLANES - 2 + tm, :] * cw[0:1]
                + ubuf[SUBLANES - 1:SUBLANES - 1 + tm, :] * cw[1:2] + u * cw[2:3])
        ubuf[0:SUBLANES, :] = ubuf[tm:tm + SUBLANES, :]
        cs_ref[0] = u[tm - (CONV_WIDTH - 1):, :]
    else:
        conv = s0_ref[...] * cw[0:1] + s1_ref[...] * cw[1:2] + u * cw[2:3]
        u_ref[...] = u

    a_in = (proj(COL_CB, COL_CB + D_CONV) * conv).astype(BF16)
    out_a = _dot(a_in, wco_ref[...])
    oa_ref[...] = (_sigmoid(proj(COL_GA, COL_GA + D_MODEL)) * out_a).astype(BF16)
    sgb_ref[...] = _sigmoid(proj(COL_GB, COL_GB + D_MODEL)).astype(BF16)

    q_ref[...] = (proj(COL_Q, COL_K) * ATTN_SCALE).astype(BF16)
    kv = proj(COL_K, COL_IQ)
    k = kv[:, :N_KV_HEADS * HEAD_DIM]
    v = kv[:, N_KV_HEADS * HEAD_DIM:]
    kf_ref[...] = k
    vf_ref[...] = v
    kb_ref[...] = k.astype(BF16)
    vb_ref[...] = v.astype(BF16)
    idx = proj(COL_IQ, COL_GA)
    for hd in range(IDX_HEADS):
        iq_ref[hd] = idx[:, hd * IDX_DIM:(hd + 1) * IDX_DIM].astype(BF16)
    ik = idx[:, COL_IK - COL_IQ:COL_IW - COL_IQ]
    ikf_ref[...] = ik
    ikb_ref[...] = ik.astype(BF16)
    iw_ref[...] = idx[:, COL_IW - COL_IQ:IDX_RAW_END - COL_IQ] * IDX_SCALE


def _const_spec(shape):
    nd = len(shape)
    return pl.BlockSpec(shape, lambda *_: (0,) * nd, pipeline_mode=pl.Buffered(1))


def _mixer_in(x, g_mix, w_all, conv_w, w_conv_out, *, batch, seq, tm, state=None):
    is_prompt = state is None
    t_all = batch * seq
    nt = seq // tm if is_prompt else 1
    grid = (batch, nt) if is_prompt else (1, 1)
    tok = lambda width: pl.BlockSpec((tm, width), lambda b, t: (b * nt + t, 0))
    in_specs = [tok(D_MODEL), _const_spec((1, D_MODEL)), _const_spec((D_MODEL, W_COLS)),
                _const_spec((CONV_WIDTH, D_CONV)), _const_spec((D_CONV, D_MODEL))]
    args = [x, g_mix, w_all, conv_w, w_conv_out]
    kvw = N_KV_HEADS * HEAD_DIM
    out_shapes = [
        jax.ShapeDtypeStruct((t_all, N_HEADS * HEAD_DIM), BF16),
        jax.ShapeDtypeStruct((t_all, kvw), F32),
        jax.ShapeDtypeStruct((t_all, kvw), F32),
        jax.ShapeDtypeStruct((t_all, kvw), BF16),
        jax.ShapeDtypeStruct((t_all, kvw), BF16),
        jax.ShapeDtypeStruct((IDX_HEADS, t_all, IDX_DIM), BF16),
        jax.ShapeDtypeStruct((t_all, IDX_DIM), F32),
        jax.ShapeDtypeStruct((t_all, IDX_DIM), BF16),
        jax.ShapeDtypeStruct((t_all, IDX_HEADS), F32),
        jax.ShapeDtypeStruct((t_all, D_MODEL), BF16),
        jax.ShapeDtypeStruct((t_all, D_MODEL), BF16),
    ]
    out_specs = [tok(s.shape[1]) for s in out_shapes]
    out_specs[5] = pl.BlockSpec((IDX_HEADS, tm, IDX_DIM), lambda b, t: (0, b * nt + t, 0))
    scratch = []
    if is_prompt:
        out_shapes.append(jax.ShapeDtypeStruct((batch, CONV_WIDTH - 1, D_CONV), F32))
        out_specs.append(pl.BlockSpec((1, CONV_WIDTH - 1, D_CONV), lambda b, t: (b, 0, 0)))
        scratch.append(pltpu.VMEM((tm + SUBLANES, D_CONV), F32))
    else:
        in_specs += [tok(D_CONV), tok(D_CONV)]
        args += list(state)
        out_shapes.append(jax.ShapeDtypeStruct((t_all, D_CONV), F32))
        out_specs.append(tok(D_CONV))
    return pl.pallas_call(
        functools.partial(_mixer_in_kernel, is_prompt, tm),
        grid=grid, in_specs=in_specs, out_specs=out_specs, out_shape=out_shapes,
        scratch_shapes=scratch,
        compiler_params=pltpu.CompilerParams(
            dimension_semantics=("arbitrary", "arbitrary"), vmem_limit_bytes=VMEM_LIMIT_BYTES),
        name="mixer_in_prompt" if is_prompt else "mixer_in_sample",
    )(*args)


def _prompt_attn_kernel(top_k, q_ref, iq_ref, iw_ref, k_ref, v_ref, ik_ref, tri_ref, o_ref,
                        sc_ref, m_ref, l_ref, acc_ref):
    i = pl.program_id(1)
    nch = i // (KEY_CHUNK // Q_BLOCK) + 1
    qpos = i * Q_BLOCK + lax.broadcasted_iota(I32, (Q_BLOCK, 1), 0)

    iw = iw_ref[...]
    iqh = [iq_ref[h] for h in range(IDX_HEADS)]
    iwh = [iw[:, h:h + 1] for h in range(IDX_HEADS)]

    def score_chunk(c, carry):
        off = pl.multiple_of(c * KEY_CHUNK, KEY_CHUNK)
        ikc = ik_ref[pl.ds(off, KEY_CHUNK), :]
        acc = jnp.zeros((Q_BLOCK, KEY_CHUNK), F32)
        for h in range(IDX_HEADS):
            acc = acc + jnp.maximum(_dot_nt(iqh[h], ikc), 0.0) * iwh[h]
        kpos = off + lax.broadcasted_iota(I32, (Q_BLOCK, KEY_CHUNK), 1)
        sc_ref[c] = jnp.where(kpos <= qpos, acc, -jnp.inf)
        return carry

    lax.fori_loop(0, nch, score_chunk, 0)

    def count(cmp, t):
        tb = jnp.broadcast_to(t, (Q_BLOCK, LANES))

        def body(c, acc):
            blk = sc_ref[c]
            for j in range(KEY_CHUNK // LANES):
                acc = acc + jnp.where(cmp(blk[:, j * LANES:(j + 1) * LANES], tb), 1.0, 0.0)
            return acc

        acc = lax.fori_loop(0, nch, body, jnp.zeros((Q_BLOCK, LANES), F32))
        return jnp.sum(acc, axis=1, keepdims=True)

    t = _kth_largest(functools.partial(count, lambda a, b: a >= b), top_k, (Q_BLOCK, 1))
    n_gt = count(lambda a, b: a > b, t)
    few = qpos < top_k
    t = jnp.where(few, -jnp.inf, t)
    need = jnp.where(few, 0.0, float(top_k) - n_gt)

    q = q_ref[...]
    qg = [jnp.concatenate([q[:, (g * GROUP + hh) * HEAD_DIM:(g * GROUP + hh + 1) * HEAD_DIM]
                           for hh in range(GROUP)], axis=0) for g in range(N_KV_HEADS)]
    m_ref[...] = jnp.full(m_ref.shape, NEG_INF, F32)
    l_ref[...] = jnp.zeros(l_ref.shape, F32)
    acc_ref[...] = jnp.zeros(acc_ref.shape, F32)

    def attn_chunk(c, n_eq):
        off = pl.multiple_of(c * KEY_CHUNK, KEY_CHUNK)
        blk = sc_ref[c]
        eq = blk == t
        eqf = jnp.where(eq, 1.0, 0.0)
        before = n_eq + _dot(eqf.astype(BF16), tri_ref[...])
        sel = (blk > t) | (eq & (before < need))
        bias = jnp.where(sel, 0.0, NEG_INF)
        bias = jnp.concatenate([bias] * GROUP, axis=0)
        for g in range(N_KV_HEADS):
            kc = k_ref[pl.ds(off, KEY_CHUNK), g * HEAD_DIM:(g + 1) * HEAD_DIM]
            vc = v_ref[pl.ds(off, KEY_CHUNK), g * HEAD_DIM:(g + 1) * HEAD_DIM]
            s = _dot_nt(qg[g], kc) + bias
            m_old = m_ref[g]
            m_new = jnp.maximum(m_old, jnp.max(s, axis=1, keepdims=True))
            alpha = jnp.exp(m_old - m_new)
            p = jnp.exp(s - m_new)
            l_ref[g] = alpha * l_ref[g] + jnp.sum(p, axis=1, keepdims=True)
            acc_ref[g] = alpha * acc_ref[g] + _dot(p.astype(BF16), vc)
            m_ref[g] = m_new
        return n_eq + jnp.sum(eqf, axis=1, keepdims=True)

    lax.fori_loop(0, nch, attn_chunk, jnp.zeros((Q_BLOCK, 1), F32))

    for g in range(N_KV_HEADS):
        o = acc_ref[g] / l_ref[g]
        for hh in range(GROUP):
            col = (g * GROUP + hh) * HEAD_DIM
            o_ref[:, col:col + HEAD_DIM] = o[hh * Q_BLOCK:(hh + 1) * Q_BLOCK].astype(o_ref.dtype)


def _prompt_attn(q, iq, iw, kb, vb, ikb, *, batch, seq):
    nqb = seq // Q_BLOCK
    kvw = N_KV_HEADS * HEAD_DIM
    top_k = min(TOP_K_MAX, seq // 4)
    tri = jnp.triu(jnp.ones((KEY_CHUNK, KEY_CHUNK), BF16), k=1)
    blk = lambda width: pl.BlockSpec((Q_BLOCK, width), lambda b, i: (b * nqb + i, 0))
    per_batch = lambda width: pl.BlockSpec((seq, width), lambda b, i: (b, 0))
    iq_spec = pl.BlockSpec((IDX_HEADS, Q_BLOCK, IDX_DIM), lambda b, i: (0, b * nqb + i, 0))
    return pl.pallas_call(
        functools.partial(_prompt_attn_kernel, top_k),
        grid=(batch, nqb),
        in_specs=[blk(N_HEADS * HEAD_DIM), iq_spec, blk(IDX_HEADS),
                  per_batch(kvw), per_batch(kvw), per_batch(IDX_DIM),
                  _const_spec((KEY_CHUNK, KEY_CHUNK))],
        out_specs=blk(N_HEADS * HEAD_DIM),
        out_shape=jax.ShapeDtypeStruct((batch * seq, N_HEADS * HEAD_DIM), BF16),
        scratch_shapes=[
            pltpu.VMEM((seq // KEY_CHUNK, Q_BLOCK, KEY_CHUNK), F32),
            pltpu.VMEM((N_KV_HEADS, GROUP * Q_BLOCK, 1), F32),
            pltpu.VMEM((N_KV_HEADS, GROUP * Q_BLOCK, 1), F32),
            pltpu.VMEM((N_KV_HEADS, GROUP * Q_BLOCK, HEAD_DIM), F32),
        ],
        compiler_params=pltpu.CompilerParams(
            dimension_semantics=("arbitrary", "arbitrary"), vmem_limit_bytes=VMEM_LIMIT_BYTES),
        name="prompt_attn",
    )(q, iq, iw, kb, vb, ikb, tri)


def _sample_attn_kernel(n_pages, page, top_k, pt_ref, q_ref, iq_ref, iw_ref, kn_ref, vn_ref, ikn_ref,
                        ck_hbm, cv_hbm, cik_hbm, tri_ref, low_ref, o_ref,
                        kbuf, vbuf, ikbuf, sems, sc_ref):
    b = pl.program_id(0)
    nb = pl.num_programs(0)

    def page_copies(req, slot):
        copies = []
        for p in range(n_pages):
            phys = pt_ref[req, p]
            rows = pl.ds(p * page, page)
            copies.append(pltpu.make_async_copy(ck_hbm.at[phys], kbuf.at[slot, rows], sems.at[0, slot]))
            copies.append(pltpu.make_async_copy(cv_hbm.at[phys], vbuf.at[slot, rows], sems.at[1, slot]))
            copies.append(pltpu.make_async_copy(cik_hbm.at[phys], ikbuf.at[slot, rows], sems.at[2, slot]))
        return copies

    slot = b % 2

    @pl.when(b == 0)
    def _():
        for c in page_copies(0, 0):
            c.start()

    @pl.when(b + 1 < nb)
    def _():
        for c in page_copies(b + 1, 1 - slot):
            c.start()

    for c in page_copies(b, slot):
        c.wait()

    iq = iq_ref[0].astype(BF16)
    iw = iw_ref[0]
    ikp = ikbuf[slot].astype(BF16)
    sidx = _dot_nt(iq, ikp)
    for p in range(n_pages):
        blk = sidx[:, p * page:(p + 1) * page]
        sc_ref[p:p + 1, :] = jnp.sum(jnp.maximum(blk, 0.0) * iw, axis=0, keepdims=True)
    sc = sc_ref[...]
    ikn = ikn_ref[0].astype(BF16).astype(F32)
    s_new = jnp.sum(iq.astype(F32) * ikn, axis=1, keepdims=True)
    sc_new = jnp.sum(jnp.maximum(s_new, 0.0) * iw, axis=0, keepdims=True)

    def total(x):
        return jnp.sum(jnp.sum(x, axis=1, keepdims=True), axis=0, keepdims=True)

    def count(cmp, t):
        return total(jnp.where(cmp(sc, t), 1.0, 0.0)) + jnp.where(cmp(sc_new, t), 1.0, 0.0)

    t = _kth_largest(functools.partial(count, lambda a, c: a >= c), top_k, (1, 1))
    need = float(top_k) - count(lambda a, c: a > c, t)
    eq = sc == t
    eqf = jnp.where(eq, 1.0, 0.0)
    in_row = _dot(eqf.astype(BF16), tri_ref[...])
    row_tot = jnp.broadcast_to(jnp.sum(eqf, axis=1, keepdims=True), sc.shape)
    rows_before = _dot(low_ref[...], row_tot.astype(BF16))
    sel = (sc > t) | (eq & (in_row + rows_before < need))
    sel_new = (sc_new > t) | ((sc_new == t) & (total(eqf) < need))
    bias = jnp.where(sel, 0.0, NEG_INF)
    bias_row = jnp.concatenate([bias[p:p + 1, :] for p in range(n_pages)], axis=1)
    bias_new = jnp.where(sel_new, 0.0, NEG_INF)

    q = q_ref[0].astype(BF16)
    qf = q.astype(F32)
    for g in range(N_KV_HEADS):
        cols = slice(g * HEAD_DIM, (g + 1) * HEAD_DIM)
        kg = kbuf[slot, :, cols].astype(BF16)
        vg = vbuf[slot, :, cols].astype(BF16)
        kn = kn_ref[0, g:g + 1, :].astype(BF16).astype(F32)
        vn = vn_ref[0, g:g + 1, :].astype(BF16).astype(F32)
        s = _dot_nt(q, kg) + bias_row
        sn = jnp.sum(qf * kn, axis=1, keepdims=True) + bias_new
        m = jnp.maximum(jnp.max(s, axis=1, keepdims=True), sn)
        p = jnp.exp(s - m)
        pn = jnp.exp(sn - m)
        l = jnp.sum(p, axis=1, keepdims=True) + pn
        o = (_dot(p.astype(BF16), vg) + pn * vn) / l
        o_ref[0, g * GROUP:(g + 1) * GROUP, :] = o[g * GROUP:(g + 1) * GROUP]


def _sample_attn(page_table, q, iq, iw, k_new, v_new, ik_new, cache_k, cache_v, cache_ik):
    nreq, n_pages = page_table.shape
    n_phys, page = cache_k.shape[0], cache_k.shape[1]
    kvw = N_KV_HEADS * HEAD_DIM
    top_k = min(TOP_K_MAX, (n_pages * page + 1) // 4)
    assert top_k < n_pages * page + 1
    ck = cache_k.reshape(n_phys, page, kvw)
    cv = cache_v.reshape(n_phys, page, kvw)
    tri = jnp.triu(jnp.ones((page, page), BF16), k=1)
    low = jnp.tril(jnp.ones((n_pages, n_pages), BF16), k=-1)
    pad_rows = lambda a: jnp.pad(a.astype(F32), ((0, 0), (0, MXU_ROWS - a.shape[1]), (0, 0)))
    per_req = lambda *shape: pl.BlockSpec((1,) + shape, lambda b, pt: (b,) + (0,) * len(shape))
    const = lambda *shape: pl.BlockSpec(shape, lambda b, pt: (0,) * len(shape))
    any_spec = pl.BlockSpec(memory_space=pl.ANY)
    grid_spec = pltpu.PrefetchScalarGridSpec(
        num_scalar_prefetch=1,
        grid=(nreq,),
        in_specs=[per_req(MXU_ROWS, HEAD_DIM), per_req(MXU_ROWS, IDX_DIM), per_req(MXU_ROWS, 1),
                  per_req(N_KV_HEADS, HEAD_DIM), per_req(N_KV_HEADS, HEAD_DIM), per_req(1, IDX_DIM),
                  any_spec, any_spec, any_spec, const(page, page), const(n_pages, n_pages)],
        out_specs=per_req(N_HEADS, HEAD_DIM),
        scratch_shapes=[
            pltpu.VMEM((2, n_pages * page, kvw), F32),
            pltpu.VMEM((2, n_pages * page, kvw), F32),
            pltpu.VMEM((2, n_pages * page, IDX_DIM), F32),
            pltpu.SemaphoreType.DMA((3, 2)),
            pltpu.VMEM((n_pages, page), F32),
        ],
    )
    out = pl.pallas_call(
        functools.partial(_sample_attn_kernel, n_pages, page, top_k),
        grid_spec=grid_spec,
        out_shape=jax.ShapeDtypeStruct((nreq, N_HEADS, HEAD_DIM), F32),
        compiler_params=pltpu.CompilerParams(
            dimension_semantics=("arbitrary",), vmem_limit_bytes=VMEM_LIMIT_BYTES),
        name="sample_attn",
    )(page_table,
      pad_rows(q.reshape(nreq, N_HEADS, HEAD_DIM)),
      pad_rows(jnp.transpose(iq, (1, 0, 2))),
      pad_rows(iw.reshape(nreq, IDX_HEADS, 1)),
      k_new.reshape(nreq, N_KV_HEADS, HEAD_DIM), v_new.reshape(nreq, N_KV_HEADS, HEAD_DIM),
      ik_new.reshape(nreq, 1, IDX_DIM),
      ck, cv, cache_ik, tri, low)
    return out.reshape(nreq, N_HEADS * HEAD_DIM)


def _merge_stage(x, attn, oa, sgb, wao_ref, wo_ref, gmem_ref, wmq_ref):
    out_b = _dot(attn, wao_ref[...])
    merged = oa.astype(F32) + sgb.astype(F32) * out_b
    x1 = x + _dot(merged.astype(BF16), wo_ref[...])
    hm = _rms(x1, gmem_ref[...]).astype(BF16)
    return x1, _dot(hm, wmq_ref[...]) * MEM_SCALE


def _ffn_stage(x1, mem_o, wmo_ref, gffn_ref, wg_ref, wu_ref, wd_ref, gfin_ref):
    x2 = x1 + _dot(mem_o.astype(BF16), wmo_ref[...])
    hf = _rms(x2, gffn_ref[...]).astype(BF16)
    acc = jnp.zeros_like(x2)
    for c in range(D_FF // FF_CHUNK):
        cols = slice(c * FF_CHUNK, (c + 1) * FF_CHUNK)
        gate = _dot(hf, wg_ref[:, cols])
        f = gate * _sigmoid(gate) * _dot(hf, wu_ref[:, cols])
        acc = acc + _dot(f.astype(BF16), wd_ref[cols, :])
    return _rms(x2 + acc, gfin_ref[...])


def _post_prompt_kernel(x_ref, attn_ref, oa_ref, sgb_ref, mk_ref, mv_ref, wao_ref, wo_ref, gmem_ref,
                        wmq_ref, wmo_ref, gffn_ref, wg_ref, wu_ref, wd_ref, gfin_ref, y_ref, mo_ref):
    x1, qm = _merge_stage(x_ref[...], attn_ref[...], oa_ref[...], sgb_ref[...],
                          wao_ref, wo_ref, gmem_ref, wmq_ref)
    for h in range(MEM_HEADS):
        cols = slice(h * MEM_HEAD_DIM, (h + 1) * MEM_HEAD_DIM)
        s = _dot_nt(qm[:, cols].astype(BF16), mk_ref[:, cols])
        p = jnp.exp(s - jnp.max(s, axis=1, keepdims=True))
        o = _dot(p.astype(BF16), mv_ref[:, cols]) / jnp.sum(p, axis=1, keepdims=True)
        mo_ref[:, cols] = o.astype(BF16)
    y_ref[...] = _ffn_stage(x1, mo_ref[...], wmo_ref, gffn_ref, wg_ref, wu_ref, wd_ref, gfin_ref)


def _post_prompt(x, attn, oa, sgb, mk, mv, w, *, batch, seq, tm):
    nt = seq // tm
    n_mem = mk.shape[0] // batch
    tok = lambda width: pl.BlockSpec((tm, width), lambda b, t: (b * nt + t, 0))
    mem = pl.BlockSpec((n_mem, D_MODEL), lambda b, t: (b, 0))
    sq = _const_spec((D_MODEL, D_MODEL))
    gain = _const_spec((1, D_MODEL))
    return pl.pallas_call(
        _post_prompt_kernel,
        grid=(batch, nt),
        in_specs=[tok(D_MODEL), tok(D_MODEL), tok(D_MODEL), tok(D_MODEL), mem, mem,
                  sq, sq, gain, sq, sq, gain,
                  _const_spec((D_MODEL, D_FF)), _const_spec((D_MODEL, D_FF)), _const_spec((D_FF, D_MODEL)),
                  gain],
        out_specs=tok(D_MODEL),
        out_shape=jax.ShapeDtypeStruct((batch * seq, D_MODEL), F32),
        scratch_shapes=[pltpu.VMEM((tm, D_MODEL), BF16)],
        compiler_params=pltpu.CompilerParams(
            dimension_semantics=("arbitrary", "arbitrary"), vmem_limit_bytes=VMEM_LIMIT_BYTES),
        name="post_prompt",
    )(x, attn, oa, sgb, mk, mv, w["wao"], w["wo"], w["gmem"], w["wmq"], w["wmo"], w["gffn"],
      w["wg"], w["wu"], w["wd"], w["gfin"])


def _merge_sample_kernel(x_ref, attn_ref, oa_ref, sgb_ref, wao_ref, wo_ref, gmem_ref, wmq_ref,
                         x1_ref, qm_ref):
    x1, qm = _merge_stage(x_ref[...], attn_ref[...].astype(BF16), oa_ref[...], sgb_ref[...],
                          wao_ref, wo_ref, gmem_ref, wmq_ref)
    x1_ref[...] = x1
    qm_ref[...] = qm


def _mem_sample_kernel(rb, q_ref, mk_ref, mv_ref, o_ref):
    for r in range(rb):
        prod = mk_ref[r] * q_ref[r]
        vv = mv_ref[r]
        for h in range(MEM_HEADS):
            cols = slice(h * MEM_HEAD_DIM, (h + 1) * MEM_HEAD_DIM)
            s = jnp.sum(prod[:, cols], axis=1, keepdims=True)
            p = jnp.exp(s - jnp.max(s, axis=0, keepdims=True))
            o = jnp.sum(p * vv[:, cols], axis=0, keepdims=True) / jnp.sum(p, axis=0, keepdims=True)
            o_ref[r, :, cols] = o


def _ffn_sample_kernel(x1_ref, mo_ref, wmo_ref, gffn_ref, wg_ref, wu_ref, wd_ref, gfin_ref, y_ref):
    y_ref[...] = _ffn_stage(x1_ref[...], mo_ref[...], wmo_ref, gffn_ref, wg_ref, wu_ref, wd_ref, gfin_ref)


def _post_sample(x, attn, oa, sgb, mem_k, mem_v, w):
    n = x.shape[0]
    n_mem = mem_k.shape[1]
    full = lambda *shape: _const_spec(shape)
    act = full(n, D_MODEL)
    sq = full(D_MODEL, D_MODEL)
    gain = full(1, D_MODEL)
    params = pltpu.CompilerParams(dimension_semantics=("arbitrary",), vmem_limit_bytes=VMEM_LIMIT_BYTES)
    x1, qm = pl.pallas_call(
        _merge_sample_kernel, grid=(1,),
        in_specs=[act, act, act, act, sq, sq, gain, sq], out_specs=[act, act],
        out_shape=[jax.ShapeDtypeStruct((n, D_MODEL), F32)] * 2,
        compiler_params=params, name="merge_sample",
    )(x, attn, oa, sgb, w["wao"], w["wo"], w["gmem"], w["wmq"])

    rb = 4
    req = lambda rows: pl.BlockSpec((rb, rows, D_MODEL), lambda i: (i, 0, 0))
    mem_o = pl.pallas_call(
        functools.partial(_mem_sample_kernel, rb), grid=(n // rb,),
        in_specs=[req(1), req(n_mem), req(n_mem)], out_specs=req(1),
        out_shape=jax.ShapeDtypeStruct((n, 1, D_MODEL), F32),
        compiler_params=params, name="mem_sample",
    )(qm.reshape(n, 1, D_MODEL), mem_k, mem_v)

    return pl.pallas_call(
        _ffn_sample_kernel, grid=(1,),
        in_specs=[act, act, sq, gain, full(D_MODEL, D_FF), full(D_MODEL, D_FF), full(D_FF, D_MODEL), gain],
        out_specs=act, out_shape=jax.ShapeDtypeStruct((n, D_MODEL), F32),
        compiler_params=params, name="ffn_sample",
    )(x1, mem_o.reshape(n, D_MODEL), w["wmo"], w["gffn"], w["wg"], w["wu"], w["wd"], w["gfin"])


def _memory_kv_kernel(mem_ref, g_ref, wk_ref, wv_ref, kf_ref, vf_ref, kb_ref, vb_ref):
    m = _rms(mem_ref[...], g_ref[...]).astype(BF16)
    k = _dot(m, wk_ref[...])
    v = _dot(m, wv_ref[...])
    kf_ref[...] = k
    vf_ref[...] = v
    kb_ref[...] = k.astype(BF16)
    vb_ref[...] = v.astype(BF16)


def _memory_kv(mem, g, wk, wv):
    n = mem.shape[0]
    full = lambda *shape: _const_spec(shape)
    act = full(n, D_MODEL)
    sq = full(D_MODEL, D_MODEL)
    return pl.pallas_call(
        _memory_kv_kernel, grid=(1,),
        in_specs=[act, full(1, D_MODEL), sq, sq], out_specs=[act] * 4,
        out_shape=[jax.ShapeDtypeStruct((n, D_MODEL), F32)] * 2 + [jax.ShapeDtypeStruct((n, D_MODEL), BF16)] * 2,
        compiler_params=pltpu.CompilerParams(
            dimension_semantics=("arbitrary",), vmem_limit_bytes=VMEM_LIMIT_BYTES),
        name="memory_kv",
    )(mem, g, wk, wv)


def _prep_w_in(w_in):
    pad = jnp.zeros((w_in.shape[0], COL_GA - IDX_RAW_END), w_in.dtype)
    return jnp.concatenate([w_in[:, :IDX_RAW_END], pad, w_in[:, IDX_RAW_END:]], axis=1).astype(BF16)


def kernel(x_prompt, x_sample, mem_prompt, cache_k, cache_v, cache_idx_k, cache_mem_k, cache_mem_v, state_conv, page_table, g_mix, w_in, conv_w, w_conv_out, w_attn_out, w_o, g_mem, g_mem_kv, w_mq, w_mk, w_mv, w_mo, g_ffn, w_gate, w_up, w_down, g_final):
    depth = w_in.shape[0]
    assert depth == 1, "single-layer step"
    batch, seq, _ = x_prompt.shape
    nreq, dec_seq, _ = x_sample.shape
    assert dec_seq == 1
    n_mem = mem_prompt.shape[1]
    l = 0
    bf = lambda a: a.astype(BF16)
    row = lambda a: a.reshape(1, -1)

    w_all = _prep_w_in(w_in[l])
    wco = bf(w_conv_out[l])
    w = dict(wao=bf(w_attn_out[l]), wo=bf(w_o[l]), gmem=row(g_mem[l]), wmq=bf(w_mq[l]), wmo=bf(w_mo[l]),
             gffn=row(g_ffn[l]), wg=bf(w_gate[l]), wu=bf(w_up[l]), wd=bf(w_down[l]), gfin=row(g_final))

    xp = x_prompt.reshape(batch * seq, D_MODEL)
    mkf, mvf, mkb, mvb = _memory_kv(mem_prompt.reshape(batch * n_mem, D_MODEL), row(g_mem_kv[l]),
                                    bf(w_mk[l]), bf(w_mv[l]))
    (q, kf, vf, kb, vb, iq, ikf, ikb, iw, oa, sgb, conv_p) = _mixer_in(
        xp, row(g_mix[l]), w_all, conv_w[l], wco, batch=batch, seq=seq, tm=512)
    attn = _prompt_attn(q, iq, iw, kb, vb, ikb, batch=batch, seq=seq)
    yp = _post_prompt(xp, attn, oa, sgb, mkb, mvb, w, batch=batch, seq=seq, tm=256)

    xs = x_sample.reshape(nreq, D_MODEL)
    st = state_conv[l]
    (q_s, kf_s, vf_s, _, _, iq_s, ikf_s, _, iw_s, oa_s, sgb_s, u_s) = _mixer_in(
        xs, row(g_mix[l]), w_all, conv_w[l], wco, batch=nreq, seq=1, tm=nreq, state=(st[:, 0], st[:, 1]))
    attn_s = _sample_attn(page_table, q_s, iq_s, iw_s, kf_s, vf_s, ikf_s,
                          cache_k[l], cache_v[l], cache_idx_k[l])
    ys = _post_sample(xs, attn_s, oa_s, sgb_s,
                      cache_mem_k[l].reshape(nreq, n_mem, D_MODEL), cache_mem_v[l].reshape(nreq, n_mem, D_MODEL), w)

    return (
        yp.reshape(batch, seq, D_MODEL),
        ys.reshape(nreq, 1, D_MODEL),
        kf.reshape(1, batch, seq, N_KV_HEADS, HEAD_DIM),
        vf.reshape(1, batch, seq, N_KV_HEADS, HEAD_DIM),
        ikf.reshape(1, batch, seq, IDX_DIM),
        conv_p.reshape(1, batch, CONV_WIDTH - 1, D_CONV),
        mkf.reshape(1, batch, n_mem, MEM_HEADS, MEM_HEAD_DIM),
        mvf.reshape(1, batch, n_mem, MEM_HEADS, MEM_HEAD_DIM),
        kf_s.reshape(1, nreq, 1, N_KV_HEADS, HEAD_DIM),
        vf_s.reshape(1, nreq, 1, N_KV_HEADS, HEAD_DIM),
        ikf_s.reshape(1, nreq, 1, IDX_DIM),
        jnp.stack([st[:, 1], u_s], axis=1).reshape(1, nreq, CONV_WIDTH - 1, D_CONV),
    )
```

```python
import functools

import jax
import jax.numpy as jnp
from jax import lax
from jax.experimental import pallas as pl
from jax.experimental.pallas import tpu as pltpu

F32 = jnp.float32
BF16 = jnp.bfloat16
I32 = jnp.int32

D_MODEL = 1024
D_CONV = 1024
CONV_WIDTH = 3
N_HEADS = 8
HEAD_DIM = 128
N_KV_HEADS = 2
GROUP = N_HEADS // N_KV_HEADS
IDX_HEADS = 4
IDX_DIM = 64
TOP_K_MAX = 256
MEM_HEADS = 4
MEM_HEAD_DIM = D_MODEL // MEM_HEADS
D_FF = 2816
EPS = 1e-6
NEG_INF = -1e30
IDX_SCALE = (IDX_HEADS * IDX_DIM) ** -0.5
ATTN_SCALE = HEAD_DIM ** -0.5
MEM_SCALE = MEM_HEAD_DIM ** -0.5

LANES = 128
SUBLANES = 8
MXU_ROWS = 16
VMEM_LIMIT_BYTES = 56 * 1024 * 1024

COL_CIN = 0
COL_CB = COL_CIN + D_CONV
COL_CC = COL_CB + D_CONV
COL_Q = COL_CC + D_CONV
COL_K = COL_Q + N_HEADS * HEAD_DIM
COL_V = COL_K + N_KV_HEADS * HEAD_DIM
COL_IQ = COL_V + N_KV_HEADS * HEAD_DIM
COL_IK = COL_IQ + IDX_HEADS * IDX_DIM
COL_IW = COL_IK + IDX_DIM
IDX_RAW_END = COL_IW + IDX_HEADS
COL_GA = -(-IDX_RAW_END // LANES) * LANES
COL_GB = COL_GA + D_MODEL
W_COLS = COL_GB + D_MODEL

Q_BLOCK = 128
KEY_CHUNK = 512
FF_CHUNK = D_FF // 2
INT_MIN = -2 ** 31


def _rms(x, g):
    return x * lax.rsqrt(jnp.mean(x * x, axis=-1, keepdims=True) + EPS) * g


def _dot(a, b):
    return jnp.dot(a, b, preferred_element_type=F32)


def _dot_nt(a, b):
    return lax.dot_general(a, b, (((1,), (1,)), ((), ())), preferred_element_type=F32)


def _sigmoid(x):
    return 1.0 / (1.0 + jnp.exp(-x))


def _key_to_float(ukey):
    skey = ukey ^ INT_MIN
    bits = jnp.where(skey < 0, skey ^ 0x7FFFFFFF, skey)
    return lax.bitcast_convert_type(bits, F32)


def _kth_largest(count_ge, top_k, shape):
    def body(b, ukey):
        cand = ukey | jnp.left_shift(jnp.int32(1), 31 - b)
        ok = count_ge(_key_to_float(cand)) >= float(top_k)
        return jnp.where(ok, cand, ukey)

    ukey = lax.fori_loop(0, 32, body, jnp.zeros(shape, I32))
    return _key_to_float(ukey)


def _mixer_in_kernel(is_prompt, tm, *refs):
    if is_prompt:
        (x_ref, g_ref, w_ref, cw_ref, wco_ref,
         q_ref, kf_ref, vf_ref, kb_ref, vb_ref, iq_ref, ikf_ref, ikb_ref, iw_ref, oa_ref, sgb_ref,
         cs_ref, ubuf) = refs
    else:
        (x_ref, g_ref, w_ref, cw_ref, wco_ref, s0_ref, s1_ref,
         q_ref, kf_ref, vf_ref, kb_ref, vb_ref, iq_ref, ikf_ref, ikb_ref, iw_ref, oa_ref, sgb_ref,
         u_ref) = refs

    h = _rms(x_ref[...], g_ref[...]).astype(BF16)

    def proj(lo, hi):
        return _dot(h, w_ref[:, lo:hi])

    u = proj(COL_CC, COL_CC + D_CONV) * proj(COL_CIN, COL_CIN + D_CONV)
    cw = cw_ref[...]
    if is_prompt:
        @pl.when(pl.program_id(1) == 0)
        def _():
            ubuf[0:SUBLANES, :] = jnp.zeros((SUBLANES, D_CONV), F32)

        ubuf[SUBLANES:SUBLANES + tm, :] = u
        conv = (ubuf[SUBLANES - 2:SUBLANES - 2 + tm, :] * cw[0:1]
                + ubuf[SUBLANES - 1:SUBLANES - 1 + tm, :] * cw[1:2] + u * cw[2:3])
        ubuf[0:SUBLANES, :] = ubuf[tm:tm + SUBLANES, :]
        cs_ref[0] = u[tm - (CONV_WIDTH - 1):, :]
    else:
        conv = s0_ref[...] * cw[0:1] + s1_ref[...] * cw[1:2] + u * cw[2:3]
        u_ref[...] = u

    a_in = (proj(COL_CB, COL_CB + D_CONV) * conv).astype(BF16)
    out_a = _dot(a_in, wco_ref[...])
    oa_ref[...] = (_sigmoid(proj(COL_GA, COL_GA + D_MODEL)) * out_a).astype(BF16)
    sgb_ref[...] = _sigmoid(proj(COL_GB, COL_GB + D_MODEL)).astype(BF16)

    q_ref[...] = (proj(COL_Q, COL_K) * ATTN_SCALE).astype(BF16)
    kv = proj(COL_K, COL_IQ)
    k = kv[:, :N_KV_HEADS * HEAD_DIM]
    v = kv[:, N_KV_HEADS * HEAD_DIM:]
    kf_ref[...] = k
    vf_ref[...] = v
    kb_ref[...] = k.astype(BF16)
    vb_ref[...] = v.astype(BF16)
    idx = proj(COL_IQ, COL_GA)
    for hd in range(IDX_HEADS):
        iq_ref[hd] = idx[:, hd * IDX_DIM:(hd + 1) * IDX_DIM].astype(BF16)
    ik = idx[:, COL_IK - COL_IQ:COL_IW - COL_IQ]
    ikf_ref[...] = ik
    ikb_ref[...] = ik.astype(BF16)
    iw_ref[...] = idx[:, COL_IW - COL_IQ:IDX_RAW_END - COL_IQ] * IDX_SCALE


def _const_spec(shape):
    nd = len(shape)
    return pl.BlockSpec(shape, lambda *_: (0,) * nd, pipeline_mode=pl.Buffered(1))


def _mixer_in(x, g_mix, w_all, conv_w, w_conv_out, *, batch, seq, tm, state=None):
    is_prompt = state is None
    t_all = batch * seq
    nt = seq // tm if is_prompt else 1
    grid = (batch, nt) if is_prompt else (1, 1)
    tok = lambda width: pl.BlockSpec((tm, width), lambda b, t: (b * nt + t, 0))
    in_specs = [tok(D_MODEL), _const_spec((1, D_MODEL)), _const_spec((D_MODEL, W_COLS)),
                _const_spec((CONV_WIDTH, D_CONV)), _const_spec((D_CONV, D_MODEL))]
    args = [x, g_mix, w_all, conv_w, w_conv_out]
    kvw = N_KV_HEADS * HEAD_DIM
    out_shapes = [
        jax.ShapeDtypeStruct((t_all, N_HEADS * HEAD_DIM), BF16),
        jax.ShapeDtypeStruct((t_all, kvw), F32),
        jax.ShapeDtypeStruct((t_all, kvw), F32),
        jax.ShapeDtypeStruct((t_all, kvw), BF16),
        jax.ShapeDtypeStruct((t_all, kvw), BF16),
        jax.ShapeDtypeStruct((IDX_HEADS, t_all, IDX_DIM), BF16),
        jax.ShapeDtypeStruct((t_all, IDX_DIM), F32),
        jax.ShapeDtypeStruct((t_all, IDX_DIM), BF16),
        jax.ShapeDtypeStruct((t_all, IDX_HEADS), F32),
        jax.ShapeDtypeStruct((t_all, D_MODEL), BF16),
        jax.ShapeDtypeStruct((t_all, D_MODEL), BF16),
    ]
    out_specs = [tok(s.shape[1]) for s in out_shapes]
    out_specs[5] = pl.BlockSpec((IDX_HEADS, tm, IDX_DIM), lambda b, t: (0, b * nt + t, 0))
    scratch = []
    if is_prompt:
        out_shapes.append(jax.ShapeDtypeStruct((batch, CONV_WIDTH - 1, D_CONV), F32))
        out_specs.append(pl.BlockSpec((1, CONV_WIDTH - 1, D_CONV), lambda b, t: (b, 0, 0)))
        scratch.append(pltpu.VMEM((tm + SUBLANES, D_CONV), F32))
    else:
        in_specs += [tok(D_CONV), tok(D_CONV)]
        args += list(state)
        out_shapes.append(jax.ShapeDtypeStruct((t_all, D_CONV), F32))
        out_specs.append(tok(D_CONV))
    return pl.pallas_call(
        functools.partial(_mixer_in_kernel, is_prompt, tm),
        grid=grid, in_specs=in_specs, out_specs=out_specs, out_shape=out_shapes,
        scratch_shapes=scratch,
        compiler_params=pltpu.CompilerParams(
            dimension_semantics=("arbitrary", "arbitrary"), vmem_limit_bytes=VMEM_LIMIT_BYTES),
        name="mixer_in_prompt" if is_prompt else "mixer_in_sample",
    )(*args)


def _prompt_attn_kernel(top_k, q_ref, iq_ref, iw_ref, k_ref, v_ref, ik_ref, tri_ref, o_ref,
                        sc_ref, m_ref, l_ref, acc_ref):
    i = pl.program_id(1)
    nch = i // (KEY_CHUNK // Q_BLOCK) + 1
    qpos = i * Q_BLOCK + lax.broadcasted_iota(I32, (Q_BLOCK, 1), 0)

    iw = iw_ref[...]
    iqh = [iq_ref[h] for h in range(IDX_HEADS)]
    iwh = [iw[:, h:h + 1] for h in range(IDX_HEADS)]

    def score_chunk(c, carry):
        off = pl.multiple_of(c * KEY_CHUNK, KEY_CHUNK)
        ikc = ik_ref[pl.ds(off, KEY_CHUNK), :]
        acc = jnp.zeros((Q_BLOCK, KEY_CHUNK), F32)
        for h in range(IDX_HEADS):
            acc = acc + jnp.maximum(_dot_nt(iqh[h], ikc), 0.0) * iwh[h]
        kpos = off + lax.broadcasted_iota(I32, (Q_BLOCK, KEY_CHUNK), 1)
        sc_ref[c] = jnp.where(kpos <= qpos, acc, -jnp.inf)
        return carry

    lax.fori_loop(0, nch, score_chunk, 0)

    def count(cmp, t):
        tb = jnp.broadcast_to(t, (Q_BLOCK, LANES))

        def body(c, acc):
            blk = sc_ref[c]
            for j in range(KEY_CHUNK // LANES):
                acc = acc + jnp.where(cmp(blk[:, j * LANES:(j + 1) * LANES], tb), 1.0, 0.0)
            return acc

        acc = lax.fori_loop(0, nch, body, jnp.zeros((Q_BLOCK, LANES), F32))
        return jnp.sum(acc, axis=1, keepdims=True)

    t = _kth_largest(functools.partial(count, lambda a, b: a >= b), top_k, (Q_BLOCK, 1))
    n_gt = count(lambda a, b: a > b, t)
    few = qpos < top_k
    t = jnp.where(few, -jnp.inf, t)
    need = jnp.where(few, 0.0, float(top_k) - n_gt)

    q = q_ref[...]
    qg = [jnp.concatenate([q[:, (g * GROUP + hh) * HEAD_DIM:(g * GROUP + hh + 1) * HEAD_DIM]
                           for hh in range(GROUP)], axis=0) for g in range(N_KV_HEADS)]
    m_ref[...] = jnp.full(m_ref.shape, NEG_INF, F32)
    l_ref[...] = jnp.zeros(l_ref.shape, F32)
    acc_ref[...] = jnp.zeros(acc_ref.shape, F32)

    def attn_chunk(c, n_eq):
        off = pl.multiple_of(c * KEY_CHUNK, KEY_CHUNK)
        blk = sc_ref[c]
        eq = blk == t
        eqf = jnp.where(eq, 1.0, 0.0)
        before = n_eq + _dot(eqf.astype(BF16), tri_ref[...])
        sel = (blk > t) | (eq & (before < need))
        bias = jnp.where(sel, 0.0, NEG_INF)
        bias = jnp.concatenate([bias] * GROUP, axis=0)
        for g in range(N_KV_HEADS):
            kc = k_ref[pl.ds(off, KEY_CHUNK), g * HEAD_DIM:(g + 1) * HEAD_DIM]
            vc = v_ref[pl.ds(off, KEY_CHUNK), g * HEAD_DIM:(g + 1) * HEAD_DIM]
            s = _dot_nt(qg[g], kc) + bias
            m_old = m_ref[g]
            m_new = jnp.maximum(m_old, jnp.max(s, axis=1, keepdims=True))
            alpha = jnp.exp(m_old - m_new)
            p = jnp.exp(s - m_new)
            l_ref[g] = alpha * l_ref[g] + jnp.sum(p, axis=1, keepdims=True)
            acc_ref[g] = alpha * acc_ref[g] + _dot(p.astype(BF16), vc)
            m_ref[g] = m_new
        return n_eq + jnp.sum(eqf, axis=1, keepdims=True)

    lax.fori_loop(0, nch, attn_chunk, jnp.zeros((Q_BLOCK, 1), F32))

    for g in range(N_KV_HEADS):
        o = acc_ref[g] / l_ref[g]
        for hh in range(GROUP):
            col = (g * GROUP + hh) * HEAD_DIM
            o_ref[:, col:col + HEAD_DIM] = o[hh * Q_BLOCK:(hh + 1) * Q_BLOCK].astype(o_ref.dtype)


def _prompt_attn(q, iq, iw, kb, vb, ikb, *, batch, seq):
    nqb = seq // Q_BLOCK
    kvw = N_KV_HEADS * HEAD_DIM
    top_k = min(TOP_K_MAX, seq // 4)
    tri = jnp.triu(jnp.ones((KEY_CHUNK, KEY_CHUNK), BF16), k=1)
    blk = lambda width: pl.BlockSpec((Q_BLOCK, width), lambda b, i: (b * nqb + i, 0))
    per_batch = lambda width: pl.BlockSpec((seq, width), lambda b, i: (b, 0))
    iq_spec = pl.BlockSpec((IDX_HEADS, Q_BLOCK, IDX_DIM), lambda b, i: (0, b * nqb + i, 0))
    return pl.pallas_call(
        functools.partial(_prompt_attn_kernel, top_k),
        grid=(batch, nqb),
        in_specs=[blk(N_HEADS * HEAD_DIM), iq_spec, blk(IDX_HEADS),
                  per_batch(kvw), per_batch(kvw), per_batch(IDX_DIM),
                  _const_spec((KEY_CHUNK, KEY_CHUNK))],
        out_specs=blk(N_HEADS * HEAD_DIM),
        out_shape=jax.ShapeDtypeStruct((batch * seq, N_HEADS * HEAD_DIM), BF16),
        scratch_shapes=[
            pltpu.VMEM((seq // KEY_CHUNK, Q_BLOCK, KEY_CHUNK), F32),
            pltpu.VMEM((N_KV_HEADS, GROUP * Q_BLOCK, 1), F32),
            pltpu.VMEM((N_KV_HEADS, GROUP * Q_BLOCK, 1), F32),
            pltpu.VMEM((N_KV_HEADS, GROUP * Q_BLOCK, HEAD_DIM), F32),
        ],
        compiler_params=pltpu.CompilerParams(
            dimension_semantics=("arbitrary", "arbitrary"), vmem_limit_bytes=VMEM_LIMIT_BYTES),
        name="prompt_attn",
    )(q, iq, iw, kb, vb, ikb, tri)


def _sample_attn_kernel(n_pages, page, top_k, pt_ref, q_ref, iq_ref, iw_ref, kn_ref, vn_ref, ikn_ref,
                        ck_hbm, cv_hbm, cik_hbm, tri_ref, low_ref, o_ref,
                        kbuf, vbuf, ikbuf, sems, sc_ref):
    b = pl.program_id(0)
    nb = pl.num_programs(0)

    def page_copies(req, slot):
        copies = []
        for p in range(n_pages):
            phys = pt_ref[req, p]
            rows = pl.ds(p * page * N_KV_HEADS, page * N_KV_HEADS)
            lanes = pl.ds(p * page, page)
            copies.append(pltpu.make_async_copy(ck_hbm.at[phys], kbuf.at[slot, rows], sems.at[0, slot]))
            copies.append(pltpu.make_async_copy(cv_hbm.at[phys], vbuf.at[slot, rows], sems.at[1, slot]))
            copies.append(pltpu.make_async_copy(cik_hbm.at[phys], ikbuf.at[slot, :, lanes], sems.at[2, slot]))
        return copies

    slot = b % 2

    @pl.when(b == 0)
    def _():
        for c in page_copies(0, 0):
            c.start()

    @pl.when(b + 1 < nb)
    def _():
        for c in page_copies(b + 1, 1 - slot):
            c.start()

    for c in page_copies(b, slot):
        c.wait()

    iq = iq_ref[0].astype(BF16)
    iw = iw_ref[0]
    ikp = ikbuf[slot].astype(BF16)
    sidx = _dot(iq, ikp)
    for p in range(n_pages):
        blk = sidx[:, p * page:(p + 1) * page]
        sc_ref[p:p + 1, :] = jnp.sum(jnp.maximum(blk, 0.0) * iw, axis=0, keepdims=True)
    sc = sc_ref[...]
    ikn = ikn_ref[0].astype(BF16).astype(F32)
    s_new = jnp.sum(iq.astype(F32) * ikn, axis=1, keepdims=True)
    sc_new = jnp.sum(jnp.maximum(s_new, 0.0) * iw, axis=0, keepdims=True)

    def total(x):
        return jnp.sum(jnp.sum(x, axis=1, keepdims=True), axis=0, keepdims=True)

    def count(cmp, t):
        return total(jnp.where(cmp(sc, t), 1.0, 0.0)) + jnp.where(cmp(sc_new, t), 1.0, 0.0)

    t = _kth_largest(functools.partial(count, lambda a, c: a >= c), top_k, (1, 1))
    need = float(top_k) - count(lambda a, c: a > c, t)
    eq = sc == t
    eqf = jnp.where(eq, 1.0, 0.0)
    in_row = _dot(eqf.astype(BF16), tri_ref[...])
    row_tot = jnp.broadcast_to(jnp.sum(eqf, axis=1, keepdims=True), sc.shape)
    rows_before = _dot(low_ref[...], row_tot.astype(BF16))
    sel = (sc > t) | (eq & (in_row + rows_before < need))
    sel_new = (sc_new > t) | ((sc_new == t) & (total(eqf) < need))
    bias = jnp.where(sel, 0.0, NEG_INF)
    bias_row = jnp.concatenate([bias[p:p + 1, :] for p in range(n_pages)], axis=1)
    bias_new = jnp.where(sel_new, 0.0, NEG_INF)

    q = q_ref[0].astype(BF16)
    qf = q.astype(F32)
    for g in range(N_KV_HEADS):
        head_rows = pl.ds(g, n_pages * page, stride=N_KV_HEADS)
        kg = kbuf[slot, head_rows, :].astype(BF16)
        vg = vbuf[slot, head_rows, :].astype(BF16)
        kn = kn_ref[0, g:g + 1, :].astype(BF16).astype(F32)
        vn = vn_ref[0, g:g + 1, :].astype(BF16).astype(F32)
        s = _dot_nt(q, kg) + bias_row
        sn = jnp.sum(qf * kn, axis=1, keepdims=True) + bias_new
        m = jnp.maximum(jnp.max(s, axis=1, keepdims=True), sn)
        p = jnp.exp(s - m)
        pn = jnp.exp(sn - m)
        l = jnp.sum(p, axis=1, keepdims=True) + pn
        o = (_dot(p.astype(BF16), vg) + pn * vn) / l
        o_ref[0, g * GROUP:(g + 1) * GROUP, :] = o[g * GROUP:(g + 1) * GROUP]


def _sample_attn(page_table, q, iq, iw, k_new, v_new, ik_new, cache_k, cache_v, cache_ik):
    nreq, n_pages = page_table.shape
    n_phys, page = cache_k.shape[0], cache_k.shape[1]
    kvw = N_KV_HEADS * HEAD_DIM
    top_k = min(TOP_K_MAX, (n_pages * page + 1) // 4)
    assert top_k < n_pages * page + 1
    ck = cache_k.reshape(n_phys, page * N_KV_HEADS, HEAD_DIM)
    cv = cache_v.reshape(n_phys, page * N_KV_HEADS, HEAD_DIM)
    cik = jnp.swapaxes(cache_ik, 1, 2)
    tri = jnp.triu(jnp.ones((page, page), BF16), k=1)
    low = jnp.tril(jnp.ones((n_pages, n_pages), BF16), k=-1)
    pad_rows = lambda a: jnp.pad(a.astype(F32), ((0, 0), (0, MXU_ROWS - a.shape[1]), (0, 0)))
    per_req = lambda *shape: pl.BlockSpec((1,) + shape, lambda b, pt: (b,) + (0,) * len(shape))
    const = lambda *shape: pl.BlockSpec(shape, lambda b, pt: (0,) * len(shape))
    any_spec = pl.BlockSpec(memory_space=pl.ANY)
    grid_spec = pltpu.PrefetchScalarGridSpec(
        num_scalar_prefetch=1,
        grid=(nreq,),
        in_specs=[per_req(MXU_ROWS, HEAD_DIM), per_req(MXU_ROWS, IDX_DIM), per_req(MXU_ROWS, 1),
                  per_req(N_KV_HEADS, HEAD_DIM), per_req(N_KV_HEADS, HEAD_DIM), per_req(1, IDX_DIM),
                  any_spec, any_spec, any_spec, const(page, page), const(n_pages, n_pages)],
        out_specs=per_req(N_HEADS, HEAD_DIM),
        scratch_shapes=[
            pltpu.VMEM((2, n_pages * page * N_KV_HEADS, HEAD_DIM), F32),
            pltpu.VMEM((2, n_pages * page * N_KV_HEADS, HEAD_DIM), F32),
            pltpu.VMEM((2, IDX_DIM, n_pages * page), F32),
            pltpu.SemaphoreType.DMA((3, 2)),
            pltpu.VMEM((n_pages, page), F32),
        ],
    )
    out = pl.pallas_call(
        functools.partial(_sample_attn_kernel, n_pages, page, top_k),
        grid_spec=grid_spec,
        out_shape=jax.ShapeDtypeStruct((nreq, N_HEADS, HEAD_DIM), F32),
        compiler_params=pltpu.CompilerParams(
            dimension_semantics=("arbitrary",), vmem_limit_bytes=VMEM_LIMIT_BYTES),
        name="sample_attn",
    )(page_table,
      pad_rows(q.reshape(nreq, N_HEADS, HEAD_DIM)),
      pad_rows(jnp.transpose(iq, (1, 0, 2))),
      pad_rows(iw.reshape(nreq, IDX_HEADS, 1)),
      k_new.reshape(nreq, N_KV_HEADS, HEAD_DIM), v_new.reshape(nreq, N_KV_HEADS, HEAD_DIM),
      ik_new.reshape(nreq, 1, IDX_DIM),
      ck, cv, cik, tri, low)
    return out.reshape(nreq, N_HEADS * HEAD_DIM)


def _merge_stage(x, attn, oa, sgb, wao_ref, wo_ref, gmem_ref, wmq_ref):
    out_b = _dot(attn, wao_ref[...])
    merged = oa.astype(F32) + sgb.astype(F32) * out_b
    x1 = x + _dot(merged.astype(BF16), wo_ref[...])
    hm = _rms(x1, gmem_ref[...]).astype(BF16)
    return x1, _dot(hm, wmq_ref[...]) * MEM_SCALE


def _ffn_stage(x1, mem_o, wmo_ref, gffn_ref, wg_ref, wu_ref, wd_ref, gfin_ref):
    x2 = x1 + _dot(mem_o.astype(BF16), wmo_ref[...])
    hf = _rms(x2, gffn_ref[...]).astype(BF16)
    acc = jnp.zeros_like(x2)
    for c in range(D_FF // FF_CHUNK):
        cols = slice(c * FF_CHUNK, (c + 1) * FF_CHUNK)
        gate = _dot(hf, wg_ref[:, cols])
        f = gate * _sigmoid(gate) * _dot(hf, wu_ref[:, cols])
        acc = acc + _dot(f.astype(BF16), wd_ref[cols, :])
    return _rms(x2 + acc, gfin_ref[...])


def _post_prompt_kernel(x_ref, attn_ref, oa_ref, sgb_ref, mk_ref, mv_ref, wao_ref, wo_ref, gmem_ref,
                        wmq_ref, wmo_ref, gffn_ref, wg_ref, wu_ref, wd_ref, gfin_ref, y_ref, mo_ref):
    x1, qm = _merge_stage(x_ref[...], attn_ref[...], oa_ref[...], sgb_ref[...],
                          wao_ref, wo_ref, gmem_ref, wmq_ref)
    for h in range(MEM_HEADS):
        cols = slice(h * MEM_HEAD_DIM, (h + 1) * MEM_HEAD_DIM)
        s = _dot_nt(qm[:, cols].astype(BF16), mk_ref[:, cols])
        p = jnp.exp(s - jnp.max(s, axis=1, keepdims=True))
        o = _dot(p.astype(BF16), mv_ref[:, cols]) / jnp.sum(p, axis=1, keepdims=True)
        mo_ref[:, cols] = o.astype(BF16)
    y_ref[...] = _ffn_stage(x1, mo_ref[...], wmo_ref, gffn_ref, wg_ref, wu_ref, wd_ref, gfin_ref)


def _post_prompt(x, attn, oa, sgb, mk, mv, w, *, batch, seq, tm):
    nt = seq // tm
    n_mem = mk.shape[0] // batch
    tok = lambda width: pl.BlockSpec((tm, width), lambda b, t: (b * nt + t, 0))
    mem = pl.BlockSpec((n_mem, D_MODEL), lambda b, t: (b, 0))
    sq = _const_spec((D_MODEL, D_MODEL))
    gain = _const_spec((1, D_MODEL))
    return pl.pallas_call(
        _post_prompt_kernel,
        grid=(batch, nt),
        in_specs=[tok(D_MODEL), tok(D_MODEL), tok(D_MODEL), tok(D_MODEL), mem, mem,
                  sq, sq, gain, sq, sq, gain,
                  _const_spec((D_MODEL, D_FF)), _const_spec((D_MODEL, D_FF)), _const_spec((D_FF, D_MODEL)),
                  gain],
        out_specs=tok(D_MODEL),
        out_shape=jax.ShapeDtypeStruct((batch * seq, D_MODEL), F32),
        scratch_shapes=[pltpu.VMEM((tm, D_MODEL), BF16)],
        compiler_params=pltpu.CompilerParams(
            dimension_semantics=("arbitrary", "arbitrary"), vmem_limit_bytes=VMEM_LIMIT_BYTES),
        name="post_prompt",
    )(x, attn, oa, sgb, mk, mv, w["wao"], w["wo"], w["gmem"], w["wmq"], w["wmo"], w["gffn"],
      w["wg"], w["wu"], w["wd"], w["gfin"])


def _merge_sample_kernel(x_ref, attn_ref, oa_ref, sgb_ref, wao_ref, wo_ref, gmem_ref, wmq_ref,
                         x1_ref, qm_ref):
    x1, qm = _merge_stage(x_ref[...], attn_ref[...].astype(BF16), oa_ref[...], sgb_ref[...],
                          wao_ref, wo_ref, gmem_ref, wmq_ref)
    x1_ref[...] = x1
    qm_ref[...] = qm


def _mem_sample_kernel(rb, q_ref, mk_ref, mv_ref, o_ref):
    for r in range(rb):
        s = jnp.sum(mk_ref[r] * q_ref[r][None], axis=2, keepdims=True)
        p = jnp.exp(s - jnp.max(s, axis=0, keepdims=True))
        o_ref[r] = jnp.sum(p * mv_ref[r], axis=0) / jnp.sum(p, axis=0)


def _ffn_sample_kernel(x1_ref, mo_ref, wmo_ref, gffn_ref, wg_ref, wu_ref, wd_ref, gfin_ref, y_ref):
    y_ref[...] = _ffn_stage(x1_ref[...], mo_ref[...], wmo_ref, gffn_ref, wg_ref, wu_ref, wd_ref, gfin_ref)


def _post_sample(x, attn, oa, sgb, mem_k, mem_v, w):
    n = x.shape[0]
    n_mem = mem_k.shape[1]
    full = lambda *shape: _const_spec(shape)
    act = full(n, D_MODEL)
    sq = full(D_MODEL, D_MODEL)
    gain = full(1, D_MODEL)
    params = pltpu.CompilerParams(dimension_semantics=("arbitrary",), vmem_limit_bytes=VMEM_LIMIT_BYTES)
    x1, qm = pl.pallas_call(
        _merge_sample_kernel, grid=(1,),
        in_specs=[act, act, act, act, sq, sq, gain, sq], out_specs=[act, act],
        out_shape=[jax.ShapeDtypeStruct((n, D_MODEL), F32)] * 2,
        compiler_params=params, name="merge_sample",
    )(x, attn, oa, sgb, w["wao"], w["wo"], w["gmem"], w["wmq"])

    rb = 4
    heads = pl.BlockSpec((rb, MEM_HEADS, MEM_HEAD_DIM), lambda i: (i, 0, 0))
    mem = pl.BlockSpec((rb, n_mem, MEM_HEADS, MEM_HEAD_DIM), lambda i: (i, 0, 0, 0))
    mem_o = pl.pallas_call(
        functools.partial(_mem_sample_kernel, rb), grid=(n // rb,),
        in_specs=[heads, mem, mem], out_specs=heads,
        out_shape=jax.ShapeDtypeStruct((n, MEM_HEADS, MEM_HEAD_DIM), F32),
        compiler_params=params, name="mem_sample",
    )(qm.reshape(n, MEM_HEADS, MEM_HEAD_DIM), mem_k, mem_v)

    return pl.pallas_call(
        _ffn_sample_kernel, grid=(1,),
        in_specs=[act, act, sq, gain, full(D_MODEL, D_FF), full(D_MODEL, D_FF), full(D_FF, D_MODEL), gain],
        out_specs=act, out_shape=jax.ShapeDtypeStruct((n, D_MODEL), F32),
        compiler_params=params, name="ffn_sample",
    )(x1, mem_o.reshape(n, D_MODEL), w["wmo"], w["gffn"], w["wg"], w["wu"], w["wd"], w["gfin"])


def _memory_kv_kernel(mem_ref, g_ref, wk_ref, wv_ref, kf_ref, vf_ref, kb_ref, vb_ref):
    m = _rms(mem_ref[...], g_ref[...]).astype(BF16)
    k = _dot(m, wk_ref[...])
    v = _dot(m, wv_ref[...])
    kf_ref[...] = k
    vf_ref[...] = v
    kb_ref[...] = k.astype(BF16)
    vb_ref[...] = v.astype(BF16)


def _memory_kv(mem, g, wk, wv):
    n = mem.shape[0]
    full = lambda *shape: _const_spec(shape)
    act = full(n, D_MODEL)
    sq = full(D_MODEL, D_MODEL)
    return pl.pallas_call(
        _memory_kv_kernel, grid=(1,),
        in_specs=[act, full(1, D_MODEL), sq, sq], out_specs=[act] * 4,
        out_shape=[jax.ShapeDtypeStruct((n, D_MODEL), F32)] * 2 + [jax.ShapeDtypeStruct((n, D_MODEL), BF16)] * 2,
        compiler_params=pltpu.CompilerParams(
            dimension_semantics=("arbitrary",), vmem_limit_bytes=VMEM_LIMIT_BYTES),
        name="memory_kv",
    )(mem, g, wk, wv)


def _prep_w_in(w_in):
    pad = jnp.zeros((w_in.shape[0], COL_GA - IDX_RAW_END), w_in.dtype)
    return jnp.concatenate([w_in[:, :IDX_RAW_END], pad, w_in[:, IDX_RAW_END:]], axis=1).astype(BF16)


def kernel(x_prompt, x_sample, mem_prompt, cache_k, cache_v, cache_idx_k, cache_mem_k, cache_mem_v, state_conv, page_table, g_mix, w_in, conv_w, w_conv_out, w_attn_out, w_o, g_mem, g_mem_kv, w_mq, w_mk, w_mv, w_mo, g_ffn, w_gate, w_up, w_down, g_final):
    depth = w_in.shape[0]
    assert depth == 1, "single-layer step"
    batch, seq, _ = x_prompt.shape
    nreq, dec_seq, _ = x_sample.shape
    assert dec_seq == 1
    n_mem = mem_prompt.shape[1]
    l = 0
    bf = lambda a: a.astype(BF16)
    row = lambda a: a.reshape(1, -1)

    w_all = _prep_w_in(w_in[l])
    wco = bf(w_conv_out[l])
    w = dict(wao=bf(w_attn_out[l]), wo=bf(w_o[l]), gmem=row(g_mem[l]), wmq=bf(w_mq[l]), wmo=bf(w_mo[l]),
             gffn=row(g_ffn[l]), wg=bf(w_gate[l]), wu=bf(w_up[l]), wd=bf(w_down[l]), gfin=row(g_final))

    xp = x_prompt.reshape(batch * seq, D_MODEL)
    mkf, mvf, mkb, mvb = _memory_kv(mem_prompt.reshape(batch * n_mem, D_MODEL), row(g_mem_kv[l]),
                                    bf(w_mk[l]), bf(w_mv[l]))
    (q, kf, vf, kb, vb, iq, ikf, ikb, iw, oa, sgb, conv_p) = _mixer_in(
        xp, row(g_mix[l]), w_all, conv_w[l], wco, batch=batch, seq=seq, tm=512)
    attn = _prompt_attn(q, iq, iw, kb, vb, ikb, batch=batch, seq=seq)
    yp = _post_prompt(xp, attn, oa, sgb, mkb, mvb, w, batch=batch, seq=seq, tm=256)

    xs = x_sample.reshape(nreq, D_MODEL)
    st = state_conv[l]
    (q_s, kf_s, vf_s, _, _, iq_s, ikf_s, _, iw_s, oa_s, sgb_s, u_s) = _mixer_in(
        xs, row(g_mix[l]), w_all, conv_w[l], wco, batch=nreq, seq=1, tm=nreq, state=(st[:, 0], st[:, 1]))
    attn_s = _sample_attn(page_table, q_s, iq_s, iw_s, kf_s, vf_s, ikf_s,
                          cache_k[l], cache_v[l], cache_idx_k[l])
    ys = _post_sample(xs, attn_s, oa_s, sgb_s, cache_mem_k[l], cache_mem_v[l], w)

    return (
        yp.reshape(batch, seq, D_MODEL),
        ys.reshape(nreq, 1, D_MODEL),
        kf.reshape(1, batch, seq, N_KV_HEADS, HEAD_DIM),
        vf.reshape(1, batch, seq, N_KV_HEADS, HEAD_DIM),
        ikf.reshape(1, batch, seq, IDX_DIM),
        conv_p.reshape(1, batch, CONV_WIDTH - 1, D_CONV),
        mkf.reshape(1, batch, n_mem, MEM_HEADS, MEM_HEAD_DIM),
        mvf.reshape(1, batch, n_mem, MEM_HEADS, MEM_HEAD_DIM),
        kf_s.reshape(1, nreq, 1, N_KV_HEADS, HEAD_DIM),
        vf_s.reshape(1, nreq, 1, N_KV_HEADS, HEAD_DIM),
        ikf_s.reshape(1, nreq, 1, IDX_DIM),
        jnp.stack([st[:, 1], u_s], axis=1).reshape(1, nreq, CONV_WIDTH - 1, D_CONV),
    )
```

```python
import functools

import jax
import jax.numpy as jnp
from jax import lax
from jax.experimental import pallas as pl
from jax.experimental.pallas import tpu as pltpu

F32 = jnp.float32
BF16 = jnp.bfloat16
I32 = jnp.int32

D_MODEL = 1024
D_CONV = 1024
CONV_WIDTH = 3
N_HEADS = 8
HEAD_DIM = 128
N_KV_HEADS = 2
GROUP = N_HEADS // N_KV_HEADS
IDX_HEADS = 4
IDX_DIM = 64
TOP_K_MAX = 256
MEM_HEADS = 4
MEM_HEAD_DIM = D_MODEL // MEM_HEADS
D_FF = 2816
EPS = 1e-6
NEG_INF = -1e30
IDX_SCALE = (IDX_HEADS * IDX_DIM) ** -0.5
ATTN_SCALE = HEAD_DIM ** -0.5
MEM_SCALE = MEM_HEAD_DIM ** -0.5

LANES = 128
SUBLANES = 8
MXU_ROWS = 16
VMEM_LIMIT_BYTES = 56 * 1024 * 1024

COL_CIN = 0
COL_CB = COL_CIN + D_CONV
COL_CC = COL_CB + D_CONV
COL_Q = COL_CC + D_CONV
COL_K = COL_Q + N_HEADS * HEAD_DIM
COL_V = COL_K + N_KV_HEADS * HEAD_DIM
COL_IQ = COL_V + N_KV_HEADS * HEAD_DIM
COL_IK = COL_IQ + IDX_HEADS * IDX_DIM
COL_IW = COL_IK + IDX_DIM
IDX_RAW_END = COL_IW + IDX_HEADS
COL_GA = -(-IDX_RAW_END // LANES) * LANES
COL_GB = COL_GA + D_MODEL
W_COLS = COL_GB + D_MODEL

Q_BLOCK = 128
KEY_CHUNK = 512
COUNT_ROWS = 64
FF_CHUNK = D_FF // 2
INT_MIN = -2 ** 31


def _rms(x, g):
    return x * lax.rsqrt(jnp.mean(x * x, axis=-1, keepdims=True) + EPS) * g


def _dot(a, b):
    return jnp.dot(a, b, preferred_element_type=F32)


def _dot_nt(a, b):
    return lax.dot_general(a, b, (((1,), (1,)), ((), ())), preferred_element_type=F32)


def _sigmoid(x):
    return 1.0 / (1.0 + jnp.exp(-x))


def _key_to_float(ukey):
    skey = ukey ^ INT_MIN
    bits = jnp.where(skey < 0, skey ^ 0x7FFFFFFF, skey)
    return lax.bitcast_convert_type(bits, F32)


def _kth_largest(count_ge, top_k, shape):
    def body(b, ukey):
        cand = ukey | jnp.left_shift(jnp.int32(1), 31 - b)
        ok = count_ge(_key_to_float(cand)) >= float(top_k)
        return jnp.where(ok, cand, ukey)

    ukey = lax.fori_loop(0, 32, body, jnp.zeros(shape, I32))
    return _key_to_float(ukey)


def _mixer_in_kernel(is_prompt, tm, *refs):
    if is_prompt:
        (x_ref, g_ref, w_ref, cw_ref, wco_ref,
         kf_ref, vf_ref, ikf_ref, oa_ref, sgb_ref,
         qt_ref, kb_ref, vt_ref, iqt_ref, ikb_ref, iwt_ref, cs_ref, ubuf) = refs
    else:
        (x_ref, g_ref, w_ref, cw_ref, wco_ref, s0_ref, s1_ref,
         kf_ref, vf_ref, ikf_ref, oa_ref, sgb_ref,
         q_ref, iq_ref, iw_ref, u_ref) = refs

    h = _rms(x_ref[...], g_ref[...]).astype(BF16)

    def proj(lo, hi):
        return _dot(h, w_ref[:, lo:hi])

    u = proj(COL_CC, COL_CC + D_CONV) * proj(COL_CIN, COL_CIN + D_CONV)
    cw = cw_ref[...]
    if is_prompt:
        @pl.when(pl.program_id(1) == 0)
        def _():
            ubuf[0:SUBLANES, :] = jnp.zeros((SUBLANES, D_CONV), F32)

        ubuf[SUBLANES:SUBLANES + tm, :] = u
        conv = (ubuf[SUBLANES - 2:SUBLANES - 2 + tm, :] * cw[0:1]
                + ubuf[SUBLANES - 1:SUBLANES - 1 + tm, :] * cw[1:2] + u * cw[2:3])
        ubuf[0:SUBLANES, :] = ubuf[tm:tm + SUBLANES, :]
        cs_ref[0] = u[tm - (CONV_WIDTH - 1):, :]
    else:
        conv = s0_ref[...] * cw[0:1] + s1_ref[...] * cw[1:2] + u * cw[2:3]
        u_ref[...] = u

    a_in = (proj(COL_CB, COL_CB + D_CONV) * conv).astype(BF16)
    out_a = _dot(a_in, wco_ref[...])
    oa_ref[...] = (_sigmoid(proj(COL_GA, COL_GA + D_MODEL)) * out_a).astype(BF16)
    sgb_ref[...] = _sigmoid(proj(COL_GB, COL_GB + D_MODEL)).astype(BF16)

    q = proj(COL_Q, COL_K) * ATTN_SCALE
    kv = proj(COL_K, COL_IQ)
    k = kv[:, :N_KV_HEADS * HEAD_DIM]
    v = kv[:, N_KV_HEADS * HEAD_DIM:]
    kf_ref[...] = k
    vf_ref[...] = v
    idx = proj(COL_IQ, COL_GA)
    iq = idx[:, :IDX_HEADS * IDX_DIM]
    ikw = idx[:, COL_IK - COL_IQ:]
    ik = ikw[:, :IDX_DIM]
    ikf_ref[...] = ik
    if is_prompt:
        kb_ref[...] = k.astype(BF16)
        ikb_ref[...] = ik.astype(BF16)
        vt_ref[0] = v.T.astype(BF16)
        for j in range(tm // Q_BLOCK):
            rows = slice(j * Q_BLOCK, (j + 1) * Q_BLOCK)
            for head in range(N_HEADS):
                g, hh = divmod(head, GROUP)
                qt_ref[j, g, :, hh * Q_BLOCK:(hh + 1) * Q_BLOCK] = (
                    q[rows, head * HEAD_DIM:(head + 1) * HEAD_DIM].T.astype(BF16))
            iqt_ref[j] = iq[rows].T.astype(BF16)
            iwt_ref[j] = ikw[rows].T[IDX_DIM:IDX_DIM + SUBLANES] * IDX_SCALE
    else:
        q_ref[...] = q.astype(BF16)
        for hd in range(IDX_HEADS):
            iq_ref[hd] = iq[:, hd * IDX_DIM:(hd + 1) * IDX_DIM].astype(BF16)
        iw_ref[...] = ikw[:, IDX_DIM:IDX_DIM + IDX_HEADS] * IDX_SCALE


def _const_spec(shape):
    nd = len(shape)
    return pl.BlockSpec(shape, lambda *_: (0,) * nd, pipeline_mode=pl.Buffered(1))


def _mixer_in(x, g_mix, w_all, conv_w, w_conv_out, *, batch, seq, tm, state=None):
    is_prompt = state is None
    t_all = batch * seq
    nt = seq // tm if is_prompt else 1
    grid = (batch, nt) if is_prompt else (1, 1)
    tok = lambda width: pl.BlockSpec((tm, width), lambda b, t: (b * nt + t, 0))
    in_specs = [tok(D_MODEL), _const_spec((1, D_MODEL)), _const_spec((D_MODEL, W_COLS)),
                _const_spec((CONV_WIDTH, D_CONV)), _const_spec((D_CONV, D_MODEL))]
    args = [x, g_mix, w_all, conv_w, w_conv_out]
    kvw = N_KV_HEADS * HEAD_DIM
    out_shapes = [
        jax.ShapeDtypeStruct((t_all, kvw), F32),
        jax.ShapeDtypeStruct((t_all, kvw), F32),
        jax.ShapeDtypeStruct((t_all, IDX_DIM), F32),
        jax.ShapeDtypeStruct((t_all, D_MODEL), BF16),
        jax.ShapeDtypeStruct((t_all, D_MODEL), BF16),
    ]
    out_specs = [tok(s.shape[1]) for s in out_shapes]
    scratch = []
    if is_prompt:
        assert tm == KEY_CHUNK and tm % Q_BLOCK == 0
        qb = tm // Q_BLOCK
        nblk = t_all // Q_BLOCK
        per_qblock = lambda *shape: pl.BlockSpec((qb,) + shape, lambda b, t: (b * nt + t,) + (0,) * len(shape))
        out_shapes += [
            jax.ShapeDtypeStruct((nblk, N_KV_HEADS, HEAD_DIM, GROUP * Q_BLOCK), BF16),
            jax.ShapeDtypeStruct((t_all, kvw), BF16),
            jax.ShapeDtypeStruct((t_all // tm, kvw, tm), BF16),
            jax.ShapeDtypeStruct((nblk, IDX_HEADS * IDX_DIM, Q_BLOCK), BF16),
            jax.ShapeDtypeStruct((t_all, IDX_DIM), BF16),
            jax.ShapeDtypeStruct((nblk, SUBLANES, Q_BLOCK), F32),
            jax.ShapeDtypeStruct((batch, CONV_WIDTH - 1, D_CONV), F32),
        ]
        out_specs += [
            per_qblock(N_KV_HEADS, HEAD_DIM, GROUP * Q_BLOCK), tok(kvw),
            pl.BlockSpec((1, kvw, tm), lambda b, t: (b * nt + t, 0, 0)),
            per_qblock(IDX_HEADS * IDX_DIM, Q_BLOCK), tok(IDX_DIM), per_qblock(SUBLANES, Q_BLOCK),
            pl.BlockSpec((1, CONV_WIDTH - 1, D_CONV), lambda b, t: (b, 0, 0)),
        ]
        scratch.append(pltpu.VMEM((tm + SUBLANES, D_CONV), F32))
    else:
        in_specs += [tok(D_CONV), tok(D_CONV)]
        args += list(state)
        out_shapes += [
            jax.ShapeDtypeStruct((t_all, N_HEADS * HEAD_DIM), BF16),
            jax.ShapeDtypeStruct((IDX_HEADS, t_all, IDX_DIM), BF16),
            jax.ShapeDtypeStruct((t_all, IDX_HEADS), F32),
            jax.ShapeDtypeStruct((t_all, D_CONV), F32),
        ]
        out_specs += [tok(N_HEADS * HEAD_DIM),
                      pl.BlockSpec((IDX_HEADS, tm, IDX_DIM), lambda b, t: (0, b * nt + t, 0)),
                      tok(IDX_HEADS), tok(D_CONV)]
    return pl.pallas_call(
        functools.partial(_mixer_in_kernel, is_prompt, tm),
        grid=grid, in_specs=in_specs, out_specs=out_specs, out_shape=out_shapes,
        scratch_shapes=scratch,
        compiler_params=pltpu.CompilerParams(
            dimension_semantics=("arbitrary", "arbitrary"), vmem_limit_bytes=VMEM_LIMIT_BYTES),
        name="mixer_in_prompt" if is_prompt else "mixer_in_sample",
    )(*args)


def _prompt_attn_kernel(top_k, qt_ref, iqt_ref, iwt_ref, k_ref, vt_ref, ik_ref, low_ref, o_ref,
                        sc_ref, m_ref, acc_ref):
    i = pl.program_id(1)
    nch = i // (KEY_CHUNK // Q_BLOCK) + 1
    qpos = i * Q_BLOCK + lax.broadcasted_iota(I32, (1, Q_BLOCK), 1)

    iw = iwt_ref[0]
    iqt = iqt_ref[0]
    iq_pairs = [jnp.concatenate([iqt[h * IDX_DIM:(h + 1) * IDX_DIM] for h in (2 * pr, 2 * pr + 1)], axis=1)
                for pr in range(IDX_HEADS // 2)]

    def score_chunk(c, carry):
        off = pl.multiple_of(c * KEY_CHUNK, KEY_CHUNK)
        ikc = ik_ref[pl.ds(off, KEY_CHUNK), :]
        acc = jnp.zeros((KEY_CHUNK, Q_BLOCK), F32)
        for pr in range(IDX_HEADS // 2):
            s2 = _dot(ikc, iq_pairs[pr])
            for e in range(2):
                h = 2 * pr + e
                acc = acc + jnp.maximum(s2[:, e * Q_BLOCK:(e + 1) * Q_BLOCK], 0.0) * iw[h:h + 1]
        kpos = off + lax.broadcasted_iota(I32, (KEY_CHUNK, Q_BLOCK), 0)
        sc_ref[c] = jnp.where(kpos <= qpos, acc, -jnp.inf)
        return carry

    lax.fori_loop(0, nch, score_chunk, 0)

    def count(cmp, t):
        def body(c, acc):
            hit = jnp.where(cmp(sc_ref[c], t), 1.0, 0.0)
            return acc + jnp.sum(hit.reshape(KEY_CHUNK // COUNT_ROWS, COUNT_ROWS, Q_BLOCK), axis=0)

        acc = lax.fori_loop(0, nch, body, jnp.zeros((COUNT_ROWS, Q_BLOCK), F32))
        return jnp.sum(acc, axis=0, keepdims=True)

    ge = lambda a, b: a >= b
    t = _kth_largest(functools.partial(count, ge), top_k, (1, Q_BLOCK))
    n_gt = count(lambda a, b: a > b, t)
    few = qpos < top_k
    surplus = jnp.where(few, 0.0, count(ge, t) - float(top_k))
    ties_matter = jnp.max(surplus) > 0.0
    t = jnp.where(few, -jnp.inf, t)
    need = jnp.where(few, 0.0, float(top_k) - n_gt)

    m_ref[...] = jnp.full(m_ref.shape, NEG_INF, F32)
    acc_ref[...] = jnp.zeros(acc_ref.shape, F32)
    ones_rows = jnp.ones((MXU_ROWS, KEY_CHUNK), BF16)

    def attn_chunk(c, n_eq):
        off = pl.multiple_of(c * KEY_CHUNK, KEY_CHUNK)
        blk = sc_ref[c]
        eq = blk == t
        eqf = jnp.where(eq, 1.0, 0.0)
        before = lax.cond(ties_matter,
                          lambda: n_eq + _dot(low_ref[...], eqf.astype(BF16)),
                          lambda: jnp.zeros((KEY_CHUNK, Q_BLOCK), F32))
        sel = (blk > t) | (eq & (before < need))
        bias = jnp.where(sel, 0.0, NEG_INF)
        bias = jnp.concatenate([bias] * GROUP, axis=1)
        for g in range(N_KV_HEADS):
            kc = k_ref[pl.ds(off, KEY_CHUNK), g * HEAD_DIM:(g + 1) * HEAD_DIM]
            s = _dot(kc, qt_ref[0, g]) + bias
            m_old = m_ref[g]
            m_new = jnp.maximum(m_old, jnp.max(s, axis=0, keepdims=True))
            alpha = jnp.exp(m_old - m_new)
            p = jnp.exp(s - m_new).astype(BF16)
            vext = jnp.concatenate([vt_ref[c, g * HEAD_DIM:(g + 1) * HEAD_DIM, :], ones_rows], axis=0)
            acc_ref[g] = alpha * acc_ref[g] + _dot(vext, p)
            m_ref[g] = m_new
        return n_eq + jnp.sum(eqf, axis=0, keepdims=True)

    lax.fori_loop(0, nch, attn_chunk, jnp.zeros((1, Q_BLOCK), F32))

    for g in range(N_KV_HEADS):
        acc = acc_ref[g]
        o = acc[:HEAD_DIM] / acc[HEAD_DIM:HEAD_DIM + 1]
        for hh in range(GROUP):
            col = (g * GROUP + hh) * HEAD_DIM
            o_ref[:, col:col + HEAD_DIM] = o[:, hh * Q_BLOCK:(hh + 1) * Q_BLOCK].T.astype(o_ref.dtype)


def _prompt_attn(qt, iqt, iwt, kb, vt, ikb, *, batch, seq):
    nqb = seq // Q_BLOCK
    nch = seq // KEY_CHUNK
    kvw = N_KV_HEADS * HEAD_DIM
    top_k = min(TOP_K_MAX, seq // 4)
    low = jnp.tril(jnp.ones((KEY_CHUNK, KEY_CHUNK), BF16), k=-1)
    per_qblock = lambda *shape: pl.BlockSpec((1,) + shape, lambda b, i: (b * nqb + i,) + (0,) * len(shape))
    per_batch = lambda width: pl.BlockSpec((seq, width), lambda b, i: (b, 0))
    return pl.pallas_call(
        functools.partial(_prompt_attn_kernel, top_k),
        grid=(batch, nqb),
        in_specs=[per_qblock(N_KV_HEADS, HEAD_DIM, GROUP * Q_BLOCK),
                  per_qblock(IDX_HEADS * IDX_DIM, Q_BLOCK), per_qblock(SUBLANES, Q_BLOCK),
                  per_batch(kvw), pl.BlockSpec((nch, kvw, KEY_CHUNK), lambda b, i: (b, 0, 0)),
                  per_batch(IDX_DIM), _const_spec((KEY_CHUNK, KEY_CHUNK))],
        out_specs=pl.BlockSpec((Q_BLOCK, N_HEADS * HEAD_DIM), lambda b, i: (b * nqb + i, 0)),
        out_shape=jax.ShapeDtypeStruct((batch * seq, N_HEADS * HEAD_DIM), BF16),
        scratch_shapes=[
            pltpu.VMEM((nch, KEY_CHUNK, Q_BLOCK), F32),
            pltpu.VMEM((N_KV_HEADS, 1, GROUP * Q_BLOCK), F32),
            pltpu.VMEM((N_KV_HEADS, HEAD_DIM + MXU_ROWS, GROUP * Q_BLOCK), F32),
        ],
        compiler_params=pltpu.CompilerParams(
            dimension_semantics=("arbitrary", "arbitrary"), vmem_limit_bytes=VMEM_LIMIT_BYTES),
        name="prompt_attn",
    )(qt, iqt, iwt, kb, vt, ikb, low)


def _sample_attn_kernel(n_pages, page, top_k, pt_ref, q_ref, iq_ref, iw_ref, kn_ref, vn_ref, ikn_ref,
                        ck_hbm, cv_hbm, cik_hbm, tri_ref, low_ref, o_ref,
                        kbuf, vbuf, ikbuf, sems, sc_ref):
    b = pl.program_id(0)
    nb = pl.num_programs(0)

    def page_copies(req, slot):
        copies = []
        for p in range(n_pages):
            phys = pt_ref[req, p]
            rows = pl.ds(p * page * N_KV_HEADS, page * N_KV_HEADS)
            lanes = pl.ds(p * page, page)
            copies.append(pltpu.make_async_copy(ck_hbm.at[phys], kbuf.at[slot, rows], sems.at[0, slot]))
            copies.append(pltpu.make_async_copy(cv_hbm.at[phys], vbuf.at[slot, rows], sems.at[1, slot]))
            copies.append(pltpu.make_async_copy(cik_hbm.at[phys], ikbuf.at[slot, :, lanes], sems.at[2, slot]))
        return copies

    slot = b % 2

    @pl.when(b == 0)
    def _():
        for c in page_copies(0, 0):
            c.start()

    @pl.when(b + 1 < nb)
    def _():
        for c in page_copies(b + 1, 1 - slot):
            c.start()

    for c in page_copies(b, slot):
        c.wait()

    iq = iq_ref[0].astype(BF16)
    iw = iw_ref[0]
    ikp = ikbuf[slot].astype(BF16)
    sidx = _dot(iq, ikp)
    for p in range(n_pages):
        blk = sidx[:, p * page:(p + 1) * page]
        sc_ref[p:p + 1, :] = jnp.sum(jnp.maximum(blk, 0.0) * iw, axis=0, keepdims=True)
    sc = sc_ref[...]
    ikn = ikn_ref[0].astype(BF16).astype(F32)
    s_new = jnp.sum(iq.astype(F32) * ikn, axis=1, keepdims=True)
    sc_new = jnp.sum(jnp.maximum(s_new, 0.0) * iw, axis=0, keepdims=True)

    def total(x):
        return jnp.sum(jnp.sum(x, axis=1, keepdims=True), axis=0, keepdims=True)

    def count(cmp, t):
        return total(jnp.where(cmp(sc, t), 1.0, 0.0)) + jnp.where(cmp(sc_new, t), 1.0, 0.0)

    t = _kth_largest(functools.partial(count, lambda a, c: a >= c), top_k, (1, 1))
    need = float(top_k) - count(lambda a, c: a > c, t)
    eq = sc == t
    eqf = jnp.where(eq, 1.0, 0.0)
    in_row = _dot(eqf.astype(BF16), tri_ref[...])
    row_tot = jnp.broadcast_to(jnp.sum(eqf, axis=1, keepdims=True), sc.shape)
    rows_before = _dot(low_ref[...], row_tot.astype(BF16))
    sel = (sc > t) | (eq & (in_row + rows_before < need))
    sel_new = (sc_new > t) | ((sc_new == t) & (total(eqf) < need))
    bias = jnp.where(sel, 0.0, NEG_INF)
    bias_row = jnp.concatenate([bias[p:p + 1, :] for p in range(n_pages)], axis=1)
    bias_new = jnp.where(sel_new, 0.0, NEG_INF)

    q = q_ref[0].astype(BF16)
    qf = q.astype(F32)
    for g in range(N_KV_HEADS):
        head_rows = pl.ds(g, n_pages * page, stride=N_KV_HEADS)
        kg = kbuf[slot, head_rows, :].astype(BF16)
        vg = vbuf[slot, head_rows, :].astype(BF16)
        kn = kn_ref[0, g:g + 1, :].astype(BF16).astype(F32)
        vn = vn_ref[0, g:g + 1, :].astype(BF16).astype(F32)
        s = _dot_nt(q, kg) + bias_row
        sn = jnp.sum(qf * kn, axis=1, keepdims=True) + bias_new
        m = jnp.maximum(jnp.max(s, axis=1, keepdims=True), sn)
        p = jnp.exp(s - m)
        pn = jnp.exp(sn - m)
        l = jnp.sum(p, axis=1, keepdims=True) + pn
        o = (_dot(p.astype(BF16), vg) + pn * vn) / l
        o_ref[0, g * GROUP:(g + 1) * GROUP, :] = o[g * GROUP:(g + 1) * GROUP]


def _sample_attn(page_table, q, iq, iw, k_new, v_new, ik_new, cache_k, cache_v, cache_ik):
    nreq, n_pages = page_table.shape
    n_phys, page = cache_k.shape[0], cache_k.shape[1]
    kvw = N_KV_HEADS * HEAD_DIM
    top_k = min(TOP_K_MAX, (n_pages * page + 1) // 4)
    assert top_k < n_pages * page + 1
    ck = cache_k.reshape(n_phys, page * N_KV_HEADS, HEAD_DIM)
    cv = cache_v.reshape(n_phys, page * N_KV_HEADS, HEAD_DIM)
    cik = jnp.swapaxes(cache_ik, 1, 2)
    tri = jnp.triu(jnp.ones((page, page), BF16), k=1)
    low = jnp.tril(jnp.ones((n_pages, n_pages), BF16), k=-1)
    pad_rows = lambda a: jnp.pad(a.astype(F32), ((0, 0), (0, MXU_ROWS - a.shape[1]), (0, 0)))
    per_req = lambda *shape: pl.BlockSpec((1,) + shape, lambda b, pt: (b,) + (0,) * len(shape))
    const = lambda *shape: pl.BlockSpec(shape, lambda b, pt: (0,) * len(shape))
    any_spec = pl.BlockSpec(memory_space=pl.ANY)
    grid_spec = pltpu.PrefetchScalarGridSpec(
        num_scalar_prefetch=1,
        grid=(nreq,),
        in_specs=[per_req(MXU_ROWS, HEAD_DIM), per_req(MXU_ROWS, IDX_DIM), per_req(MXU_ROWS, 1),
                  per_req(N_KV_HEADS, HEAD_DIM), per_req(N_KV_HEADS, HEAD_DIM), per_req(1, IDX_DIM),
                  any_spec, any_spec, any_spec, const(page, page), const(n_pages, n_pages)],
        out_specs=per_req(N_HEADS, HEAD_DIM),
        scratch_shapes=[
            pltpu.VMEM((2, n_pages * page * N_KV_HEADS, HEAD_DIM), F32),
            pltpu.VMEM((2, n_pages * page * N_KV_HEADS, HEAD_DIM), F32),
            pltpu.VMEM((2, IDX_DIM, n_pages * page), F32),
            pltpu.SemaphoreType.DMA((3, 2)),
            pltpu.VMEM((n_pages, page), F32),
        ],
    )
    out = pl.pallas_call(
        functools.partial(_sample_attn_kernel, n_pages, page, top_k),
        grid_spec=grid_spec,
        out_shape=jax.ShapeDtypeStruct((nreq, N_HEADS, HEAD_DIM), F32),
        compiler_params=pltpu.CompilerParams(
            dimension_semantics=("arbitrary",), vmem_limit_bytes=VMEM_LIMIT_BYTES),
        name="sample_attn",
    )(page_table,
      pad_rows(q.reshape(nreq, N_HEADS, HEAD_DIM)),
      pad_rows(jnp.transpose(iq, (1, 0, 2))),
      pad_rows(iw.reshape(nreq, IDX_HEADS, 1)),
      k_new.reshape(nreq, N_KV_HEADS, HEAD_DIM), v_new.reshape(nreq, N_KV_HEADS, HEAD_DIM),
      ik_new.reshape(nreq, 1, IDX_DIM),
      ck, cv, cik, tri, low)
    return out.reshape(nreq, N_HEADS * HEAD_DIM)


def _merge_stage(x, attn, oa, sgb, wao_ref, wo_ref, gmem_ref, wmq_ref):
    out_b = _dot(attn, wao_ref[...])
    merged = oa.astype(F32) + sgb.astype(F32) * out_b
    x1 = x + _dot(merged.astype(BF16), wo_ref[...])
    hm = _rms(x1, gmem_ref[...]).astype(BF16)
    return x1, _dot(hm, wmq_ref[...]) * MEM_SCALE


def _ffn_stage(x1, mem_o, wmo_ref, gffn_ref, wg_ref, wu_ref, wd_ref, gfin_ref):
    x2 = x1 + _dot(mem_o.astype(BF16), wmo_ref[...])
    hf = _rms(x2, gffn_ref[...]).astype(BF16)
    acc = jnp.zeros_like(x2)
    for c in range(D_FF // FF_CHUNK):
        cols = slice(c * FF_CHUNK, (c + 1) * FF_CHUNK)
        gate = _dot(hf, wg_ref[:, cols])
        f = gate * _sigmoid(gate) * _dot(hf, wu_ref[:, cols])
        acc = acc + _dot(f.astype(BF16), wd_ref[cols, :])
    return _rms(x2 + acc, gfin_ref[...])


def _post_prompt_kernel(x_ref, attn_ref, oa_ref, sgb_ref, mk_ref, mv_ref, wao_ref, wo_ref, gmem_ref,
                        wmq_ref, wmo_ref, gffn_ref, wg_ref, wu_ref, wd_ref, gfin_ref, y_ref, mo_ref):
    x1, qm = _merge_stage(x_ref[...], attn_ref[...], oa_ref[...], sgb_ref[...],
                          wao_ref, wo_ref, gmem_ref, wmq_ref)
    for h in range(MEM_HEADS):
        cols = slice(h * MEM_HEAD_DIM, (h + 1) * MEM_HEAD_DIM)
        s = _dot_nt(qm[:, cols].astype(BF16), mk_ref[:, cols])
        p = jnp.exp(s - jnp.max(s, axis=1, keepdims=True))
        o = _dot(p.astype(BF16), mv_ref[:, cols]) / jnp.sum(p, axis=1, keepdims=True)
        mo_ref[:, cols] = o.astype(BF16)
    y_ref[...] = _ffn_stage(x1, mo_ref[...], wmo_ref, gffn_ref, wg_ref, wu_ref, wd_ref, gfin_ref)


def _post_prompt(x, attn, oa, sgb, mk, mv, w, *, batch, seq, tm):
    nt = seq // tm
    n_mem = mk.shape[0] // batch
    tok = lambda width: pl.BlockSpec((tm, width), lambda b, t: (b * nt + t, 0))
    mem = pl.BlockSpec((n_mem, D_MODEL), lambda b, t: (b, 0))
    sq = _const_spec((D_MODEL, D_MODEL))
    gain = _const_spec((1, D_MODEL))
    return pl.pallas_call(
        _post_prompt_kernel,
        grid=(batch, nt),
        in_specs=[tok(D_MODEL), tok(D_MODEL), tok(D_MODEL), tok(D_MODEL), mem, mem,
                  sq, sq, gain, sq, sq, gain,
                  _const_spec((D_MODEL, D_FF)), _const_spec((D_MODEL, D_FF)), _const_spec((D_FF, D_MODEL)),
                  gain],
        out_specs=tok(D_MODEL),
        out_shape=jax.ShapeDtypeStruct((batch * seq, D_MODEL), F32),
        scratch_shapes=[pltpu.VMEM((tm, D_MODEL), BF16)],
        compiler_params=pltpu.CompilerParams(
            dimension_semantics=("arbitrary", "arbitrary"), vmem_limit_bytes=VMEM_LIMIT_BYTES),
        name="post_prompt",
    )(x, attn, oa, sgb, mk, mv, w["wao"], w["wo"], w["gmem"], w["wmq"], w["wmo"], w["gffn"],
      w["wg"], w["wu"], w["wd"], w["gfin"])


def _merge_sample_kernel(x_ref, attn_ref, oa_ref, sgb_ref, wao_ref, wo_ref, gmem_ref, wmq_ref,
                         x1_ref, qm_ref):
    x1, qm = _merge_stage(x_ref[...], attn_ref[...].astype(BF16), oa_ref[...], sgb_ref[...],
                          wao_ref, wo_ref, gmem_ref, wmq_ref)
    x1_ref[...] = x1
    qm_ref[...] = qm


def _mem_sample_kernel(rb, q_ref, mk_ref, mv_ref, o_ref):
    for r in range(rb):
        s = jnp.sum(mk_ref[r] * q_ref[r][None], axis=2, keepdims=True)
        p = jnp.exp(s - jnp.max(s, axis=0, keepdims=True))
        o_ref[r] = jnp.sum(p * mv_ref[r], axis=0) / jnp.sum(p, axis=0)


def _ffn_sample_kernel(x1_ref, mo_ref, wmo_ref, gffn_ref, wg_ref, wu_ref, wd_ref, gfin_ref, y_ref):
    y_ref[...] = _ffn_stage(x1_ref[...], mo_ref[...], wmo_ref, gffn_ref, wg_ref, wu_ref, wd_ref, gfin_ref)


def _post_sample(x, attn, oa, sgb, mem_k, mem_v, w):
    n = x.shape[0]
    n_mem = mem_k.shape[1]
    full = lambda *shape: _const_spec(shape)
    act = full(n, D_MODEL)
    sq = full(D_MODEL, D_MODEL)
    gain = full(1, D_MODEL)
    params = pltpu.CompilerParams(dimension_semantics=("arbitrary",), vmem_limit_bytes=VMEM_LIMIT_BYTES)
    x1, qm = pl.pallas_call(
        _merge_sample_kernel, grid=(1,),
        in_specs=[act, act, act, act, sq, sq, gain, sq], out_specs=[act, act],
        out_shape=[jax.ShapeDtypeStruct((n, D_MODEL), F32)] * 2,
        compiler_params=params, name="merge_sample",
    )(x, attn, oa, sgb, w["wao"], w["wo"], w["gmem"], w["wmq"])

    rb = 4
    heads = pl.BlockSpec((rb, MEM_HEADS, MEM_HEAD_DIM), lambda i: (i, 0, 0))
    mem = pl.BlockSpec((rb, n_mem, MEM_HEADS, MEM_HEAD_DIM), lambda i: (i, 0, 0, 0))
    mem_o = pl.pallas_call(
        functools.partial(_mem_sample_kernel, rb), grid=(n // rb,),
        in_specs=[heads, mem, mem], out_specs=heads,
        out_shape=jax.ShapeDtypeStruct((n, MEM_HEADS, MEM_HEAD_DIM), F32),
        compiler_params=params, name="mem_sample",
    )(qm.reshape(n, MEM_HEADS, MEM_HEAD_DIM), mem_k, mem_v)

    return pl.pallas_call(
        _ffn_sample_kernel, grid=(1,),
        in_specs=[act, act, sq, gain, full(D_MODEL, D_FF), full(D_MODEL, D_FF), full(D_FF, D_MODEL), gain],
        out_specs=act, out_shape=jax.ShapeDtypeStruct((n, D_MODEL), F32),
        compiler_params=params, name="ffn_sample",
    )(x1, mem_o.reshape(n, D_MODEL), w["wmo"], w["gffn"], w["wg"], w["wu"], w["wd"], w["gfin"])


def _memory_kv_kernel(mem_ref, g_ref, wk_ref, wv_ref, kf_ref, vf_ref, kb_ref, vb_ref):
    m = _rms(mem_ref[...], g_ref[...]).astype(BF16)
    k = _dot(m, wk_ref[...])
    v = _dot(m, wv_ref[...])
    kf_ref[...] = k
    vf_ref[...] = v
    kb_ref[...] = k.astype(BF16)
    vb_ref[...] = v.astype(BF16)


def _memory_kv(mem, g, wk, wv):
    n = mem.shape[0]
    full = lambda *shape: _const_spec(shape)
    act = full(n, D_MODEL)
    sq = full(D_MODEL, D_MODEL)
    return pl.pallas_call(
        _memory_kv_kernel, grid=(1,),
        in_specs=[act, full(1, D_MODEL), sq, sq], out_specs=[act] * 4,
        out_shape=[jax.ShapeDtypeStruct((n, D_MODEL), F32)] * 2 + [jax.ShapeDtypeStruct((n, D_MODEL), BF16)] * 2,
        compiler_params=pltpu.CompilerParams(
            dimension_semantics=("arbitrary",), vmem_limit_bytes=VMEM_LIMIT_BYTES),
        name="memory_kv",
    )(mem, g, wk, wv)


def _prep_w_in(w_in):
    pad = jnp.zeros((w_in.shape[0], COL_GA - IDX_RAW_END), w_in.dtype)
    return jnp.concatenate([w_in[:, :IDX_RAW_END], pad, w_in[:, IDX_RAW_END:]], axis=1).astype(BF16)


def kernel(x_prompt, x_sample, mem_prompt, cache_k, cache_v, cache_idx_k, cache_mem_k, cache_mem_v, state_conv, page_table, g_mix, w_in, conv_w, w_conv_out, w_attn_out, w_o, g_mem, g_mem_kv, w_mq, w_mk, w_mv, w_mo, g_ffn, w_gate, w_up, w_down, g_final):
    depth = w_in.shape[0]
    assert depth == 1, "single-layer step"
    batch, seq, _ = x_prompt.shape
    nreq, dec_seq, _ = x_sample.shape
    assert dec_seq == 1
    n_mem = mem_prompt.shape[1]
    l = 0
    bf = lambda a: a.astype(BF16)
    row = lambda a: a.reshape(1, -1)

    w_all = _prep_w_in(w_in[l])
    wco = bf(w_conv_out[l])
    w = dict(wao=bf(w_attn_out[l]), wo=bf(w_o[l]), gmem=row(g_mem[l]), wmq=bf(w_mq[l]), wmo=bf(w_mo[l]),
             gffn=row(g_ffn[l]), wg=bf(w_gate[l]), wu=bf(w_up[l]), wd=bf(w_down[l]), gfin=row(g_final))

    xp = x_prompt.reshape(batch * seq, D_MODEL)
    mkf, mvf, mkb, mvb = _memory_kv(mem_prompt.reshape(batch * n_mem, D_MODEL), row(g_mem_kv[l]),
                                    bf(w_mk[l]), bf(w_mv[l]))
    (kf, vf, ikf, oa, sgb, qt, kb, vt, iqt, ikb, iwt, conv_p) = _mixer_in(
        xp, row(g_mix[l]), w_all, conv_w[l], wco, batch=batch, seq=seq, tm=KEY_CHUNK)
    attn = _prompt_attn(qt, iqt, iwt, kb, vt, ikb, batch=batch, seq=seq)
    yp = _post_prompt(xp, attn, oa, sgb, mkb, mvb, w, batch=batch, seq=seq, tm=256)

    xs = x_sample.reshape(nreq, D_MODEL)
    st = state_conv[l]
    (kf_s, vf_s, ikf_s, oa_s, sgb_s, q_s, iq_s, iw_s, u_s) = _mixer_in(
        xs, row(g_mix[l]), w_all, conv_w[l], wco, batch=nreq, seq=1, tm=nreq, state=(st[:, 0], st[:, 1]))
    attn_s = _sample_attn(page_table, q_s, iq_s, iw_s, kf_s, vf_s, ikf_s,
                          cache_k[l], cache_v[l], cache_idx_k[l])
    ys = _post_sample(xs, attn_s, oa_s, sgb_s, cache_mem_k[l], cache_mem_v[l], w)

    return (
        yp.reshape(batch, seq, D_MODEL),
        ys.reshape(nreq, 1, D_MODEL),
        kf.reshape(1, batch, seq, N_KV_HEADS, HEAD_DIM),
        vf.reshape(1, batch, seq, N_KV_HEADS, HEAD_DIM),
        ikf.reshape(1, batch, seq, IDX_DIM),
        conv_p.reshape(1, batch, CONV_WIDTH - 1, D_CONV),
        mkf.reshape(1, batch, n_mem, MEM_HEADS, MEM_HEAD_DIM),
        mvf.reshape(1, batch, n_mem, MEM_HEADS, MEM_HEAD_DIM),
        kf_s.reshape(1, nreq, 1, N_KV_HEADS, HEAD_DIM),
        vf_s.reshape(1, nreq, 1, N_KV_HEADS, HEAD_DIM),
        ikf_s.reshape(1, nreq, 1, IDX_DIM),
        jnp.stack([st[:, 1], u_s], axis=1).reshape(1, nreq, CONV_WIDTH - 1, D_CONV),
    )
```

```python
import functools

import jax
import jax.numpy as jnp
from jax import lax
from jax.experimental import pallas as pl
from jax.experimental.pallas import tpu as pltpu

F32 = jnp.float32
BF16 = jnp.bfloat16
I32 = jnp.int32

D_MODEL = 1024
D_CONV = 1024
CONV_WIDTH = 3
N_HEADS = 8
HEAD_DIM = 128
N_KV_HEADS = 2
GROUP = N_HEADS // N_KV_HEADS
IDX_HEADS = 4
IDX_DIM = 64
TOP_K_MAX = 256
MEM_HEADS = 4
MEM_HEAD_DIM = D_MODEL // MEM_HEADS
D_FF = 2816
EPS = 1e-6
NEG_INF = -1e30
IDX_SCALE = (IDX_HEADS * IDX_DIM) ** -0.5
ATTN_SCALE = HEAD_DIM ** -0.5
MEM_SCALE = MEM_HEAD_DIM ** -0.5

LANES = 128
SUBLANES = 8
MXU_ROWS = 16
VMEM_LIMIT_BYTES = 56 * 1024 * 1024

COL_CIN = 0
COL_CB = COL_CIN + D_CONV
COL_CC = COL_CB + D_CONV
COL_Q = COL_CC + D_CONV
COL_K = COL_Q + N_HEADS * HEAD_DIM
COL_V = COL_K + N_KV_HEADS * HEAD_DIM
COL_IQ = COL_V + N_KV_HEADS * HEAD_DIM
COL_IK = COL_IQ + IDX_HEADS * IDX_DIM
COL_IW = COL_IK + IDX_DIM
IDX_RAW_END = COL_IW + IDX_HEADS
COL_GA = -(-IDX_RAW_END // LANES) * LANES
COL_GB = COL_GA + D_MODEL
W_COLS = COL_GB + D_MODEL

Q_BLOCK = 128
KEY_CHUNK = 512
COUNT_ROWS = 64
PLANE_ROWS = KEY_CHUNK // 32
SAMPLE_SEARCH_BITS = 4
FF_CHUNK = D_FF // 2
INT_MIN = -2 ** 31


def _rms(x, g):
    return x * lax.rsqrt(jnp.mean(x * x, axis=-1, keepdims=True) + EPS) * g


def _dot(a, b):
    return jnp.dot(a, b, preferred_element_type=F32)


def _dot_nt(a, b):
    return lax.dot_general(a, b, (((1,), (1,)), ((), ())), preferred_element_type=F32)


def _sigmoid(x):
    return 1.0 / (1.0 + jnp.exp(-x))


def _key_to_float(ukey):
    skey = ukey ^ INT_MIN
    bits = jnp.where(skey < 0, skey ^ 0x7FFFFFFF, skey)
    return lax.bitcast_convert_type(bits, F32)


def _float_to_key(x):
    bits = lax.bitcast_convert_type(x, I32)
    return bits ^ (lax.shift_right_arithmetic(bits, jnp.full_like(bits, 31)) & 0x7FFFFFFF) ^ INT_MIN


def _kth_largest(count_ge, top_k, shape, bits_per_round=1):
    rounds = 32 // bits_per_round

    def body(r, ukey):
        shift = 32 - bits_per_round * (r + 1)
        digit = jnp.zeros(shape, I32)
        for j in range(1, 2 ** bits_per_round):
            cand = ukey | jnp.left_shift(jnp.int32(j), shift)
            ok = count_ge(_key_to_float(cand)) >= float(top_k)
            digit = jnp.where(ok, j, digit)
        return ukey | jnp.left_shift(digit, shift)

    ukey = lax.fori_loop(0, rounds, body, jnp.zeros(shape, I32))
    return _key_to_float(ukey)


def _bit_transpose32(words):
    a = list(words)
    j, m = 16, 0x0000FFFF
    while j:
        k = 0
        while k < 32:
            t = (a[k] ^ lax.shift_right_logical(a[k + j], jnp.full_like(a[k], j))) & m
            a[k] = a[k] ^ t
            a[k + j] = a[k + j] ^ lax.shift_left(t, jnp.full_like(t, j))
            k = (k + j + 1) & ~j
        j >>= 1
        m = (m ^ (m << j)) & 0xFFFFFFFF
    return a


def _radix_select(planes_ref, live, top_k):
    def body(i, carry):
        live, ukey, n_above = carry
        hit = live & planes_ref[i]
        cnt = jnp.sum(lax.population_count(hit).astype(F32), axis=0, keepdims=True)
        take = n_above + cnt >= float(top_k)
        live = jnp.where(take, hit, live ^ hit)
        ukey = jnp.where(take, ukey | jnp.left_shift(jnp.int32(1), 31 - i), ukey)
        n_above = jnp.where(take, n_above, n_above + cnt)
        return live, ukey, n_above

    lanes = live.shape[1]
    init = (live, jnp.zeros((1, lanes), I32), jnp.zeros((1, lanes), F32))
    return lax.fori_loop(0, 32, body, init)[1]


def _mixer_in_kernel(is_prompt, tm, *refs):
    if is_prompt:
        (x_ref, g_ref, w_ref, cw_ref, wco_ref,
         kf_ref, vf_ref, ikf_ref, oa_ref, sgb_ref,
         qt_ref, kb_ref, vt_ref, iqt_ref, ikb_ref, iwt_ref, cs_ref, ubuf) = refs
    else:
        (x_ref, g_ref, w_ref, cw_ref, wco_ref, s0_ref, s1_ref,
         kf_ref, vf_ref, ikf_ref, oa_ref, sgb_ref,
         q_ref, iq_ref, iw_ref, u_ref) = refs

    h = _rms(x_ref[...], g_ref[...]).astype(BF16)

    def proj(lo, hi):
        return _dot(h, w_ref[:, lo:hi])

    u = proj(COL_CC, COL_CC + D_CONV) * proj(COL_CIN, COL_CIN + D_CONV)
    cw = cw_ref[...]
    if is_prompt:
        @pl.when(pl.program_id(1) == 0)
        def _():
            ubuf[0:SUBLANES, :] = jnp.zeros((SUBLANES, D_CONV), F32)

        ubuf[SUBLANES:SUBLANES + tm, :] = u
        conv = (ubuf[SUBLANES - 2:SUBLANES - 2 + tm, :] * cw[0:1]
                + ubuf[SUBLANES - 1:SUBLANES - 1 + tm, :] * cw[1:2] + u * cw[2:3])
        ubuf[0:SUBLANES, :] = ubuf[tm:tm + SUBLANES, :]
        cs_ref[0] = u[tm - (CONV_WIDTH - 1):, :]
    else:
        conv = s0_ref[...] * cw[0:1] + s1_ref[...] * cw[1:2] + u * cw[2:3]
        u_ref[...] = u

    a_in = (proj(COL_CB, COL_CB + D_CONV) * conv).astype(BF16)
    out_a = _dot(a_in, wco_ref[...])
    oa_ref[...] = (_sigmoid(proj(COL_GA, COL_GA + D_MODEL)) * out_a).astype(BF16)
    sgb_ref[...] = _sigmoid(proj(COL_GB, COL_GB + D_MODEL)).astype(BF16)

    q = proj(COL_Q, COL_K) * ATTN_SCALE
    kv = proj(COL_K, COL_IQ)
    k = kv[:, :N_KV_HEADS * HEAD_DIM]
    v = kv[:, N_KV_HEADS * HEAD_DIM:]
    kf_ref[...] = k
    vf_ref[...] = v
    idx = proj(COL_IQ, COL_GA)
    iq = idx[:, :IDX_HEADS * IDX_DIM]
    ikw = idx[:, COL_IK - COL_IQ:]
    ik = ikw[:, :IDX_DIM]
    ikf_ref[...] = ik
    if is_prompt:
        kb_ref[...] = k.astype(BF16)
        ikb_ref[...] = ik.astype(BF16)
        vt_ref[0] = v.T.astype(BF16)
        for j in range(tm // Q_BLOCK):
            rows = slice(j * Q_BLOCK, (j + 1) * Q_BLOCK)
            for head in range(N_HEADS):
                g, hh = divmod(head, GROUP)
                qt_ref[j, g, :, hh * Q_BLOCK:(hh + 1) * Q_BLOCK] = (
                    q[rows, head * HEAD_DIM:(head + 1) * HEAD_DIM].T.astype(BF16))
            iqt_ref[j] = iq[rows].T.astype(BF16)
            iwt_ref[j] = ikw[rows].T[IDX_DIM:IDX_DIM + SUBLANES] * IDX_SCALE
    else:
        q_ref[...] = q.astype(BF16)
        for hd in range(IDX_HEADS):
            iq_ref[hd] = iq[:, hd * IDX_DIM:(hd + 1) * IDX_DIM].astype(BF16)
        iw_ref[...] = ikw[:, IDX_DIM:IDX_DIM + IDX_HEADS] * IDX_SCALE


def _const_spec(shape):
    nd = len(shape)
    return pl.BlockSpec(shape, lambda *_: (0,) * nd, pipeline_mode=pl.Buffered(1))


def _mixer_in(x, g_mix, w_all, conv_w, w_conv_out, *, batch, seq, tm, state=None):
    is_prompt = state is None
    t_all = batch * seq
    nt = seq // tm if is_prompt else 1
    grid = (batch, nt) if is_prompt else (1, 1)
    tok = lambda width: pl.BlockSpec((tm, width), lambda b, t: (b * nt + t, 0))
    in_specs = [tok(D_MODEL), _const_spec((1, D_MODEL)), _const_spec((D_MODEL, W_COLS)),
                _const_spec((CONV_WIDTH, D_CONV)), _const_spec((D_CONV, D_MODEL))]
    args = [x, g_mix, w_all, conv_w, w_conv_out]
    kvw = N_KV_HEADS * HEAD_DIM
    out_shapes = [
        jax.ShapeDtypeStruct((t_all, kvw), F32),
        jax.ShapeDtypeStruct((t_all, kvw), F32),
        jax.ShapeDtypeStruct((t_all, IDX_DIM), F32),
        jax.ShapeDtypeStruct((t_all, D_MODEL), BF16),
        jax.ShapeDtypeStruct((t_all, D_MODEL), BF16),
    ]
    out_specs = [tok(s.shape[1]) for s in out_shapes]
    scratch = []
    if is_prompt:
        assert tm == KEY_CHUNK and tm % Q_BLOCK == 0
        qb = tm // Q_BLOCK
        nblk = t_all // Q_BLOCK
        per_qblock = lambda *shape: pl.BlockSpec((qb,) + shape, lambda b, t: (b * nt + t,) + (0,) * len(shape))
        out_shapes += [
            jax.ShapeDtypeStruct((nblk, N_KV_HEADS, HEAD_DIM, GROUP * Q_BLOCK), BF16),
            jax.ShapeDtypeStruct((t_all, kvw), BF16),
            jax.ShapeDtypeStruct((t_all // tm, kvw, tm), BF16),
            jax.ShapeDtypeStruct((nblk, IDX_HEADS * IDX_DIM, Q_BLOCK), BF16),
            jax.ShapeDtypeStruct((t_all, IDX_DIM), BF16),
            jax.ShapeDtypeStruct((nblk, SUBLANES, Q_BLOCK), F32),
            jax.ShapeDtypeStruct((batch, CONV_WIDTH - 1, D_CONV), F32),
        ]
        out_specs += [
            per_qblock(N_KV_HEADS, HEAD_DIM, GROUP * Q_BLOCK), tok(kvw),
            pl.BlockSpec((1, kvw, tm), lambda b, t: (b * nt + t, 0, 0)),
            per_qblock(IDX_HEADS * IDX_DIM, Q_BLOCK), tok(IDX_DIM), per_qblock(SUBLANES, Q_BLOCK),
            pl.BlockSpec((1, CONV_WIDTH - 1, D_CONV), lambda b, t: (b, 0, 0)),
        ]
        scratch.append(pltpu.VMEM((tm + SUBLANES, D_CONV), F32))
    else:
        in_specs += [tok(D_CONV), tok(D_CONV)]
        args += list(state)
        out_shapes += [
            jax.ShapeDtypeStruct((t_all, N_HEADS * HEAD_DIM), BF16),
            jax.ShapeDtypeStruct((IDX_HEADS, t_all, IDX_DIM), BF16),
            jax.ShapeDtypeStruct((t_all, IDX_HEADS), F32),
            jax.ShapeDtypeStruct((t_all, D_CONV), F32),
        ]
        out_specs += [tok(N_HEADS * HEAD_DIM),
                      pl.BlockSpec((IDX_HEADS, tm, IDX_DIM), lambda b, t: (0, b * nt + t, 0)),
                      tok(IDX_HEADS), tok(D_CONV)]
    return pl.pallas_call(
        functools.partial(_mixer_in_kernel, is_prompt, tm),
        grid=grid, in_specs=in_specs, out_specs=out_specs, out_shape=out_shapes,
        scratch_shapes=scratch,
        compiler_params=pltpu.CompilerParams(
            dimension_semantics=("arbitrary", "arbitrary"), vmem_limit_bytes=VMEM_LIMIT_BYTES),
        name="mixer_in_prompt" if is_prompt else "mixer_in_sample",
    )(*args)


def _prompt_attn_kernel(top_k, qt_ref, iqt_ref, iwt_ref, k_ref, vt_ref, ik_ref, low_ref, o_ref,
                        sc_ref, planes_ref, m_ref, acc_ref):
    i = pl.program_id(1)
    nch = i // (KEY_CHUNK // Q_BLOCK) + 1
    qpos = i * Q_BLOCK + lax.broadcasted_iota(I32, (1, Q_BLOCK), 1)

    @pl.when(i == 0)
    def _():
        planes_ref[...] = jnp.zeros(planes_ref.shape, I32)

    iw = iwt_ref[0]
    iqt = iqt_ref[0]
    iq_pairs = [jnp.concatenate([iqt[h * IDX_DIM:(h + 1) * IDX_DIM] for h in (2 * pr, 2 * pr + 1)], axis=1)
                for pr in range(IDX_HEADS // 2)]

    def score_chunk(c, carry):
        off = pl.multiple_of(c * KEY_CHUNK, KEY_CHUNK)
        ikc = ik_ref[pl.ds(off, KEY_CHUNK), :]
        acc = jnp.zeros((KEY_CHUNK, Q_BLOCK), F32)
        for pr in range(IDX_HEADS // 2):
            s2 = _dot(ikc, iq_pairs[pr])
            for e in range(2):
                h = 2 * pr + e
                acc = acc + jnp.maximum(s2[:, e * Q_BLOCK:(e + 1) * Q_BLOCK], 0.0) * iw[h:h + 1]
        kpos = off + lax.broadcasted_iota(I32, (KEY_CHUNK, Q_BLOCK), 0)
        sc = jnp.where(kpos <= qpos, acc, -jnp.inf)
        sc_ref[c] = sc
        keys = _float_to_key(sc)
        for wd in range(PLANE_ROWS // SUBLANES):
            words = [keys[(wd * 32 + j) * SUBLANES:(wd * 32 + j + 1) * SUBLANES] for j in range(32)]
            row = pl.multiple_of(c * PLANE_ROWS + wd * SUBLANES, SUBLANES)
            for b, plane in enumerate(_bit_transpose32(words)):
                planes_ref[b, pl.ds(row, SUBLANES), :] = plane
        return carry

    lax.fori_loop(0, nch, score_chunk, 0)

    def count(cmp, t):
        def body(c, acc):
            hit = jnp.where(cmp(sc_ref[c], t), 1.0, 0.0)
            return acc + jnp.sum(hit.reshape(KEY_CHUNK // COUNT_ROWS, COUNT_ROWS, Q_BLOCK), axis=0)

        acc = lax.fori_loop(0, nch, body, jnp.zeros((COUNT_ROWS, Q_BLOCK), F32))
        return jnp.sum(acc, axis=0, keepdims=True)

    ge = lambda a, b: a >= b
    gt = lambda a, b: a > b
    few = qpos < top_k
    plane_row = lax.broadcasted_iota(I32, (planes_ref.shape[1], Q_BLOCK), 0)
    t = _key_to_float(_radix_select(planes_ref, jnp.where(plane_row < nch * PLANE_ROWS, -1, 0), top_k))
    n_gt = count(gt, t)
    n_ge = count(ge, t)
    is_kth = few | ((n_gt < float(top_k)) & (n_ge >= float(top_k)))

    def recount():
        t2 = _kth_largest(functools.partial(count, ge), top_k, (1, Q_BLOCK))
        return t2, count(gt, t2), count(ge, t2)

    t, n_gt, n_ge = lax.cond(jnp.min(jnp.where(is_kth, 1.0, 0.0)) > 0.0, lambda: (t, n_gt, n_ge), recount)
    surplus = jnp.where(few, 0.0, n_ge - float(top_k))
    ties_matter = jnp.max(surplus) > 0.0
    t = jnp.where(few, -jnp.inf, t)
    need = jnp.where(few, 0.0, float(top_k) - n_gt)

    m_ref[...] = jnp.full(m_ref.shape, NEG_INF, F32)
    acc_ref[...] = jnp.zeros(acc_ref.shape, F32)
    ones_rows = jnp.ones((MXU_ROWS, KEY_CHUNK), BF16)

    def attn_chunk(c, n_eq):
        off = pl.multiple_of(c * KEY_CHUNK, KEY_CHUNK)
        blk = sc_ref[c]
        eq = blk == t
        eqf = jnp.where(eq, 1.0, 0.0)
        before = lax.cond(ties_matter,
                          lambda: n_eq + _dot(low_ref[...], eqf.astype(BF16)),
                          lambda: jnp.zeros((KEY_CHUNK, Q_BLOCK), F32))
        sel = (blk > t) | (eq & (before < need))
        bias = jnp.where(sel, 0.0, NEG_INF)
        bias = jnp.concatenate([bias] * GROUP, axis=1)
        for g in range(N_KV_HEADS):
            kc = k_ref[pl.ds(off, KEY_CHUNK), g * HEAD_DIM:(g + 1) * HEAD_DIM]
            s = _dot(kc, qt_ref[0, g]) + bias
            m_old = m_ref[g]
            m_new = jnp.maximum(m_old, jnp.max(s, axis=0, keepdims=True))
            alpha = jnp.exp(m_old - m_new)
            p = jnp.exp(s - m_new).astype(BF16)
            vext = jnp.concatenate([vt_ref[c, g * HEAD_DIM:(g + 1) * HEAD_DIM, :], ones_rows], axis=0)
            acc_ref[g] = alpha * acc_ref[g] + _dot(vext, p)
            m_ref[g] = m_new
        return n_eq + jnp.sum(eqf, axis=0, keepdims=True)

    lax.fori_loop(0, nch, attn_chunk, jnp.zeros((1, Q_BLOCK), F32))

    for g in range(N_KV_HEADS):
        acc = acc_ref[g]
        o = acc[:HEAD_DIM] / acc[HEAD_DIM:HEAD_DIM + 1]
        for hh in range(GROUP):
            col = (g * GROUP + hh) * HEAD_DIM
            o_ref[:, col:col + HEAD_DIM] = o[:, hh * Q_BLOCK:(hh + 1) * Q_BLOCK].T.astype(o_ref.dtype)


def _prompt_attn(qt, iqt, iwt, kb, vt, ikb, *, batch, seq):
    nqb = seq // Q_BLOCK
    nch = seq // KEY_CHUNK
    kvw = N_KV_HEADS * HEAD_DIM
    top_k = min(TOP_K_MAX, seq // 4)
    low = jnp.tril(jnp.ones((KEY_CHUNK, KEY_CHUNK), BF16), k=-1)
    per_qblock = lambda *shape: pl.BlockSpec((1,) + shape, lambda b, i: (b * nqb + i,) + (0,) * len(shape))
    per_batch = lambda width: pl.BlockSpec((seq, width), lambda b, i: (b, 0))
    return pl.pallas_call(
        functools.partial(_prompt_attn_kernel, top_k),
        grid=(batch, nqb),
        in_specs=[per_qblock(N_KV_HEADS, HEAD_DIM, GROUP * Q_BLOCK),
                  per_qblock(IDX_HEADS * IDX_DIM, Q_BLOCK), per_qblock(SUBLANES, Q_BLOCK),
                  per_batch(kvw), pl.BlockSpec((nch, kvw, KEY_CHUNK), lambda b, i: (b, 0, 0)),
                  per_batch(IDX_DIM), _const_spec((KEY_CHUNK, KEY_CHUNK))],
        out_specs=pl.BlockSpec((Q_BLOCK, N_HEADS * HEAD_DIM), lambda b, i: (b * nqb + i, 0)),
        out_shape=jax.ShapeDtypeStruct((batch * seq, N_HEADS * HEAD_DIM), BF16),
        scratch_shapes=[
            pltpu.VMEM((nch, KEY_CHUNK, Q_BLOCK), F32),
            pltpu.VMEM((32, nch * PLANE_ROWS, Q_BLOCK), I32),
            pltpu.VMEM((N_KV_HEADS, 1, GROUP * Q_BLOCK), F32),
            pltpu.VMEM((N_KV_HEADS, HEAD_DIM + MXU_ROWS, GROUP * Q_BLOCK), F32),
        ],
        compiler_params=pltpu.CompilerParams(
            dimension_semantics=("arbitrary", "arbitrary"), vmem_limit_bytes=VMEM_LIMIT_BYTES),
        name="prompt_attn",
    )(qt, iqt, iwt, kb, vt, ikb, low)


def _sample_attn_kernel(n_pages, page, top_k, pt_ref, q_ref, iq_ref, iw_ref, kn_ref, vn_ref, ikn_ref,
                        ck_hbm, cv_hbm, cik_hbm, tri_ref, low_ref, o_ref,
                        kbuf, vbuf, ikbuf, sems, sc_ref):
    b = pl.program_id(0)
    nb = pl.num_programs(0)

    def page_copies(req, slot):
        copies = []
        for p in range(n_pages):
            phys = pt_ref[req, p]
            rows = pl.ds(p * page * N_KV_HEADS, page * N_KV_HEADS)
            lanes = pl.ds(p * page, page)
            copies.append(pltpu.make_async_copy(ck_hbm.at[phys], kbuf.at[slot, rows], sems.at[0, slot]))
            copies.append(pltpu.make_async_copy(cv_hbm.at[phys], vbuf.at[slot, rows], sems.at[1, slot]))
            copies.append(pltpu.make_async_copy(cik_hbm.at[phys], ikbuf.at[slot, :, lanes], sems.at[2, slot]))
        return copies

    slot = b % 2

    @pl.when(b == 0)
    def _():
        for c in page_copies(0, 0):
            c.start()

    @pl.when(b + 1 < nb)
    def _():
        for c in page_copies(b + 1, 1 - slot):
            c.start()

    for c in page_copies(b, slot):
        c.wait()

    iq = iq_ref[0].astype(BF16)
    iw = iw_ref[0]
    ikp = ikbuf[slot].astype(BF16)
    sidx = _dot(iq, ikp)
    for p in range(n_pages):
        blk = sidx[:, p * page:(p + 1) * page]
        sc_ref[p:p + 1, :] = jnp.sum(jnp.maximum(blk, 0.0) * iw, axis=0, keepdims=True)
    sc = sc_ref[...]
    ikn = ikn_ref[0].astype(BF16).astype(F32)
    s_new = jnp.sum(iq.astype(F32) * ikn, axis=1, keepdims=True)
    sc_new = jnp.sum(jnp.maximum(s_new, 0.0) * iw, axis=0, keepdims=True)

    def total(x):
        return jnp.sum(jnp.sum(x, axis=1, keepdims=True), axis=0, keepdims=True)

    def count(cmp, t):
        return total(jnp.where(cmp(sc, t), 1.0, 0.0)) + jnp.where(cmp(sc_new, t), 1.0, 0.0)

    t = _kth_largest(functools.partial(count, lambda a, c: a >= c), top_k, (1, 1), SAMPLE_SEARCH_BITS)
    need = float(top_k) - count(lambda a, c: a > c, t)
    eq = sc == t
    eqf = jnp.where(eq, 1.0, 0.0)
    in_row = _dot(eqf.astype(BF16), tri_ref[...])
    row_tot = jnp.broadcast_to(jnp.sum(eqf, axis=1, keepdims=True), sc.shape)
    rows_before = _dot(low_ref[...], row_tot.astype(BF16))
    sel = (sc > t) | (eq & (in_row + rows_before < need))
    sel_new = (sc_new > t) | ((sc_new == t) & (total(eqf) < need))
    bias = jnp.where(sel, 0.0, NEG_INF)
    bias_row = jnp.concatenate([bias[p:p + 1, :] for p in range(n_pages)], axis=1)
    bias_new = jnp.where(sel_new, 0.0, NEG_INF)

    q = q_ref[0].astype(BF16)
    qf = q.astype(F32)
    for g in range(N_KV_HEADS):
        head_rows = pl.ds(g, n_pages * page, stride=N_KV_HEADS)
        kg = kbuf[slot, head_rows, :].astype(BF16)
        vg = vbuf[slot, head_rows, :].astype(BF16)
        kn = kn_ref[0, g:g + 1, :].astype(BF16).astype(F32)
        vn = vn_ref[0, g:g + 1, :].astype(BF16).astype(F32)
        s = _dot_nt(q, kg) + bias_row
        sn = jnp.sum(qf * kn, axis=1, keepdims=True) + bias_new
        m = jnp.maximum(jnp.max(s, axis=1, keepdims=True), sn)
        p = jnp.exp(s - m)
        pn = jnp.exp(sn - m)
        l = jnp.sum(p, axis=1, keepdims=True) + pn
        o = (_dot(p.astype(BF16), vg) + pn * vn) / l
        o_ref[0, g * GROUP:(g + 1) * GROUP, :] = o[g * GROUP:(g + 1) * GROUP]


def _sample_attn(page_table, q, iq, iw, k_new, v_new, ik_new, cache_k, cache_v, cache_ik):
    nreq, n_pages = page_table.shape
    n_phys, page = cache_k.shape[0], cache_k.shape[1]
    kvw = N_KV_HEADS * HEAD_DIM
    top_k = min(TOP_K_MAX, (n_pages * page + 1) // 4)
    assert top_k < n_pages * page + 1
    ck = cache_k.reshape(n_phys, page * N_KV_HEADS, HEAD_DIM)
    cv = cache_v.reshape(n_phys, page * N_KV_HEADS, HEAD_DIM)
    cik = jnp.swapaxes(cache_ik, 1, 2)
    tri = jnp.triu(jnp.ones((page, page), BF16), k=1)
    low = jnp.tril(jnp.ones((n_pages, n_pages), BF16), k=-1)
    pad_rows = lambda a: jnp.pad(a.astype(F32), ((0, 0), (0, MXU_ROWS - a.shape[1]), (0, 0)))
    per_req = lambda *shape: pl.BlockSpec((1,) + shape, lambda b, pt: (b,) + (0,) * len(shape))
    const = lambda *shape: pl.BlockSpec(shape, lambda b, pt: (0,) * len(shape))
    any_spec = pl.BlockSpec(memory_space=pl.ANY)
    grid_spec = pltpu.PrefetchScalarGridSpec(
        num_scalar_prefetch=1,
        grid=(nreq,),
        in_specs=[per_req(MXU_ROWS, HEAD_DIM), per_req(MXU_ROWS, IDX_DIM), per_req(MXU_ROWS, 1),
                  per_req(N_KV_HEADS, HEAD_DIM), per_req(N_KV_HEADS, HEAD_DIM), per_req(1, IDX_DIM),
                  any_spec, any_spec, any_spec, const(page, page), const(n_pages, n_pages)],
        out_specs=per_req(N_HEADS, HEAD_DIM),
        scratch_shapes=[
            pltpu.VMEM((2, n_pages * page * N_KV_HEADS, HEAD_DIM), F32),
            pltpu.VMEM((2, n_pages * page * N_KV_HEADS, HEAD_DIM), F32),
            pltpu.VMEM((2, IDX_DIM, n_pages * page), F32),
            pltpu.SemaphoreType.DMA((3, 2)),
            pltpu.VMEM((n_pages, page), F32),
        ],
    )
    out = pl.pallas_call(
        functools.partial(_sample_attn_kernel, n_pages, page, top_k),
        grid_spec=grid_spec,
        out_shape=jax.ShapeDtypeStruct((nreq, N_HEADS, HEAD_DIM), F32),
        compiler_params=pltpu.CompilerParams(
            dimension_semantics=("arbitrary",), vmem_limit_bytes=VMEM_LIMIT_BYTES),
        name="sample_attn",
    )(page_table,
      pad_rows(q.reshape(nreq, N_HEADS, HEAD_DIM)),
      pad_rows(jnp.transpose(iq, (1, 0, 2))),
      pad_rows(iw.reshape(nreq, IDX_HEADS, 1)),
      k_new.reshape(nreq, N_KV_HEADS, HEAD_DIM), v_new.reshape(nreq, N_KV_HEADS, HEAD_DIM),
      ik_new.reshape(nreq, 1, IDX_DIM),
      ck, cv, cik, tri, low)
    return out.reshape(nreq, N_HEADS * HEAD_DIM)


def _merge_stage(x, attn, oa, sgb, wao_ref, wo_ref, gmem_ref, wmq_ref):
    out_b = _dot(attn, wao_ref[...])
    merged = oa.astype(F32) + sgb.astype(F32) * out_b
    x1 = x + _dot(merged.astype(BF16), wo_ref[...])
    hm = _rms(x1, gmem_ref[...]).astype(BF16)
    return x1, _dot(hm, wmq_ref[...]) * MEM_SCALE


def _ffn_stage(x1, mem_o, wmo_ref, gffn_ref, wg_ref, wu_ref, wd_ref, gfin_ref):
    x2 = x1 + _dot(mem_o.astype(BF16), wmo_ref[...])
    hf = _rms(x2, gffn_ref[...]).astype(BF16)
    acc = jnp.zeros_like(x2)
    for c in range(D_FF // FF_CHUNK):
        cols = slice(c * FF_CHUNK, (c + 1) * FF_CHUNK)
        gate = _dot(hf, wg_ref[:, cols])
        f = gate * _sigmoid(gate) * _dot(hf, wu_ref[:, cols])
        acc = acc + _dot(f.astype(BF16), wd_ref[cols, :])
    return _rms(x2 + acc, gfin_ref[...])


def _post_prompt_kernel(x_ref, attn_ref, oa_ref, sgb_ref, mk_ref, mv_ref, wao_ref, wo_ref, gmem_ref,
                        wmq_ref, wmo_ref, gffn_ref, wg_ref, wu_ref, wd_ref, gfin_ref, y_ref, mo_ref):
    x1, qm = _merge_stage(x_ref[...], attn_ref[...], oa_ref[...], sgb_ref[...],
                          wao_ref, wo_ref, gmem_ref, wmq_ref)
    for h in range(MEM_HEADS):
        cols = slice(h * MEM_HEAD_DIM, (h + 1) * MEM_HEAD_DIM)
        s = _dot_nt(qm[:, cols].astype(BF16), mk_ref[:, cols])
        p = jnp.exp(s - jnp.max(s, axis=1, keepdims=True))
        o = _dot(p.astype(BF16), mv_ref[:, cols]) / jnp.sum(p, axis=1, keepdims=True)
        mo_ref[:, cols] = o.astype(BF16)
    y_ref[...] = _ffn_stage(x1, mo_ref[...], wmo_ref, gffn_ref, wg_ref, wu_ref, wd_ref, gfin_ref)


def _post_prompt(x, attn, oa, sgb, mk, mv, w, *, batch, seq, tm):
    nt = seq // tm
    n_mem = mk.shape[0] // batch
    tok = lambda width: pl.BlockSpec((tm, width), lambda b, t: (b * nt + t, 0))
    mem = pl.BlockSpec((n_mem, D_MODEL), lambda b, t: (b, 0))
    sq = _const_spec((D_MODEL, D_MODEL))
    gain = _const_spec((1, D_MODEL))
    return pl.pallas_call(
        _post_prompt_kernel,
        grid=(batch, nt),
        in_specs=[tok(D_MODEL), tok(D_MODEL), tok(D_MODEL), tok(D_MODEL), mem, mem,
                  sq, sq, gain, sq, sq, gain,
                  _const_spec((D_MODEL, D_FF)), _const_spec((D_MODEL, D_FF)), _const_spec((D_FF, D_MODEL)),
                  gain],
        out_specs=tok(D_MODEL),
        out_shape=jax.ShapeDtypeStruct((batch * seq, D_MODEL), F32),
        scratch_shapes=[pltpu.VMEM((tm, D_MODEL), BF16)],
        compiler_params=pltpu.CompilerParams(
            dimension_semantics=("arbitrary", "arbitrary"), vmem_limit_bytes=VMEM_LIMIT_BYTES),
        name="post_prompt",
    )(x, attn, oa, sgb, mk, mv, w["wao"], w["wo"], w["gmem"], w["wmq"], w["wmo"], w["gffn"],
      w["wg"], w["wu"], w["wd"], w["gfin"])


def _merge_sample_kernel(x_ref, attn_ref, oa_ref, sgb_ref, wao_ref, wo_ref, gmem_ref, wmq_ref,
                         x1_ref, qm_ref):
    x1, qm = _merge_stage(x_ref[...], attn_ref[...].astype(BF16), oa_ref[...], sgb_ref[...],
                          wao_ref, wo_ref, gmem_ref, wmq_ref)
    x1_ref[...] = x1
    qm_ref[...] = qm


def _mem_sample_kernel(rb, q_ref, mk_ref, mv_ref, o_ref):
    for r in range(rb):
        s = jnp.sum(mk_ref[r] * q_ref[r][None], axis=2, keepdims=True)
        p = jnp.exp(s - jnp.max(s, axis=0, keepdims=True))
        o_ref[r] = jnp.sum(p * mv_ref[r], axis=0) / jnp.sum(p, axis=0)


def _ffn_sample_kernel(x1_ref, mo_ref, wmo_ref, gffn_ref, wg_ref, wu_ref, wd_ref, gfin_ref, y_ref):
    y_ref[...] = _ffn_stage(x1_ref[...], mo_ref[...], wmo_ref, gffn_ref, wg_ref, wu_ref, wd_ref, gfin_ref)


def _post_sample(x, attn, oa, sgb, mem_k, mem_v, w):
    n = x.shape[0]
    n_mem = mem_k.shape[1]
    full = lambda *shape: _const_spec(shape)
    act = full(n, D_MODEL)
    sq = full(D_MODEL, D_MODEL)
    gain = full(1, D_MODEL)
    params = pltpu.CompilerParams(dimension_semantics=("arbitrary",), vmem_limit_bytes=VMEM_LIMIT_BYTES)
    x1, qm = pl.pallas_call(
        _merge_sample_kernel, grid=(1,),
        in_specs=[act, act, act, act, sq, sq, gain, sq], out_specs=[act, act],
        out_shape=[jax.ShapeDtypeStruct((n, D_MODEL), F32)] * 2,
        compiler_params=params, name="merge_sample",
    )(x, attn, oa, sgb, w["wao"], w["wo"], w["gmem"], w["wmq"])

    rb = 4
    heads = pl.BlockSpec((rb, MEM_HEADS, MEM_HEAD_DIM), lambda i: (i, 0, 0))
    mem = pl.BlockSpec((rb, n_mem, MEM_HEADS, MEM_HEAD_DIM), lambda i: (i, 0, 0, 0))
    mem_o = pl.pallas_call(
        functools.partial(_mem_sample_kernel, rb), grid=(n // rb,),
        in_specs=[heads, mem, mem], out_specs=heads,
        out_shape=jax.ShapeDtypeStruct((n, MEM_HEADS, MEM_HEAD_DIM), F32),
        compiler_params=params, name="mem_sample",
    )(qm.reshape(n, MEM_HEADS, MEM_HEAD_DIM), mem_k, mem_v)

    return pl.pallas_call(
        _ffn_sample_kernel, grid=(1,),
        in_specs=[act, act, sq, gain, full(D_MODEL, D_FF), full(D_MODEL, D_FF), full(D_FF, D_MODEL), gain],
        out_specs=act, out_shape=jax.ShapeDtypeStruct((n, D_MODEL), F32),
        compiler_params=params, name="ffn_sample",
    )(x1, mem_o.reshape(n, D_MODEL), w["wmo"], w["gffn"], w["wg"], w["wu"], w["wd"], w["gfin"])


def _memory_kv_kernel(mem_ref, g_ref, wk_ref, wv_ref, kf_ref, vf_ref, kb_ref, vb_ref):
    m = _rms(mem_ref[...], g_ref[...]).astype(BF16)
    k = _dot(m, wk_ref[...])
    v = _dot(m, wv_ref[...])
    kf_ref[...] = k
    vf_ref[...] = v
    kb_ref[...] = k.astype(BF16)
    vb_ref[...] = v.astype(BF16)


def _memory_kv(mem, g, wk, wv):
    n = mem.shape[0]
    full = lambda *shape: _const_spec(shape)
    act = full(n, D_MODEL)
    sq = full(D_MODEL, D_MODEL)
    return pl.pallas_call(
        _memory_kv_kernel, grid=(1,),
        in_specs=[act, full(1, D_MODEL), sq, sq], out_specs=[act] * 4,
        out_shape=[jax.ShapeDtypeStruct((n, D_MODEL), F32)] * 2 + [jax.ShapeDtypeStruct((n, D_MODEL), BF16)] * 2,
        compiler_params=pltpu.CompilerParams(
            dimension_semantics=("arbitrary",), vmem_limit_bytes=VMEM_LIMIT_BYTES),
        name="memory_kv",
    )(mem, g, wk, wv)


def _prep_w_in(w_in):
    pad = jnp.zeros((w_in.shape[0], COL_GA - IDX_RAW_END), w_in.dtype)
    return jnp.concatenate([w_in[:, :IDX_RAW_END], pad, w_in[:, IDX_RAW_END:]], axis=1).astype(BF16)


def kernel(x_prompt, x_sample, mem_prompt, cache_k, cache_v, cache_idx_k, cache_mem_k, cache_mem_v, state_conv, page_table, g_mix, w_in, conv_w, w_conv_out, w_attn_out, w_o, g_mem, g_mem_kv, w_mq, w_mk, w_mv, w_mo, g_ffn, w_gate, w_up, w_down, g_final):
    depth = w_in.shape[0]
    assert depth == 1, "single-layer step"
    batch, seq, _ = x_prompt.shape
    nreq, dec_seq, _ = x_sample.shape
    assert dec_seq == 1
    n_mem = mem_prompt.shape[1]
    l = 0
    bf = lambda a: a.astype(BF16)
    row = lambda a: a.reshape(1, -1)

    w_all = _prep_w_in(w_in[l])
    wco = bf(w_conv_out[l])
    w = dict(wao=bf(w_attn_out[l]), wo=bf(w_o[l]), gmem=row(g_mem[l]), wmq=bf(w_mq[l]), wmo=bf(w_mo[l]),
             gffn=row(g_ffn[l]), wg=bf(w_gate[l]), wu=bf(w_up[l]), wd=bf(w_down[l]), gfin=row(g_final))

    xp = x_prompt.reshape(batch * seq, D_MODEL)
    mkf, mvf, mkb, mvb = _memory_kv(mem_prompt.reshape(batch * n_mem, D_MODEL), row(g_mem_kv[l]),
                                    bf(w_mk[l]), bf(w_mv[l]))
    (kf, vf, ikf, oa, sgb, qt, kb, vt, iqt, ikb, iwt, conv_p) = _mixer_in(
        xp, row(g_mix[l]), w_all, conv_w[l], wco, batch=batch, seq=seq, tm=KEY_CHUNK)
    attn = _prompt_attn(qt, iqt, iwt, kb, vt, ikb, batch=batch, seq=seq)
    yp = _post_prompt(xp, attn, oa, sgb, mkb, mvb, w, batch=batch, seq=seq, tm=256)

    xs = x_sample.reshape(nreq, D_MODEL)
    st = state_conv[l]
    (kf_s, vf_s, ikf_s, oa_s, sgb_s, q_s, iq_s, iw_s, u_s) = _mixer_in(
        xs, row(g_mix[l]), w_all, conv_w[l], wco, batch=nreq, seq=1, tm=nreq, state=(st[:, 0], st[:, 1]))
    attn_s = _sample_attn(page_table, q_s, iq_s, iw_s, kf_s, vf_s, ikf_s,
                          cache_k[l], cache_v[l], cache_idx_k[l])
    ys = _post_sample(xs, attn_s, oa_s, sgb_s, cache_mem_k[l], cache_mem_v[l], w)

    return (
        yp.reshape(batch, seq, D_MODEL),
        ys.reshape(nreq, 1, D_MODEL),
        kf.reshape(1, batch, seq, N_KV_HEADS, HEAD_DIM),
        vf.reshape(1, batch, seq, N_KV_HEADS, HEAD_DIM),
        ikf.reshape(1, batch, seq, IDX_DIM),
        conv_p.reshape(1, batch, CONV_WIDTH - 1, D_CONV),
        mkf.reshape(1, batch, n_mem, MEM_HEADS, MEM_HEAD_DIM),
        mvf.reshape(1, batch, n_mem, MEM_HEADS, MEM_HEAD_DIM),
        kf_s.reshape(1, nreq, 1, N_KV_HEADS, HEAD_DIM),
        vf_s.reshape(1, nreq, 1, N_KV_HEADS, HEAD_DIM),
        ikf_s.reshape(1, nreq, 1, IDX_DIM),
        jnp.stack([st[:, 1], u_s], axis=1).reshape(1, nreq, CONV_WIDTH - 1, D_CONV),
    )
```

```python
import functools

import jax
import jax.numpy as jnp
from jax import lax
from jax.experimental import pallas as pl
from jax.experimental.pallas import tpu as pltpu

F32 = jnp.float32
BF16 = jnp.bfloat16
I32 = jnp.int32

D_MODEL = 1024
D_CONV = 1024
CONV_WIDTH = 3
N_HEADS = 8
HEAD_DIM = 128
N_KV_HEADS = 2
GROUP = N_HEADS // N_KV_HEADS
IDX_HEADS = 4
IDX_DIM = 64
TOP_K_MAX = 256
MEM_HEADS = 4
MEM_HEAD_DIM = D_MODEL // MEM_HEADS
D_FF = 2816
EPS = 1e-6
NEG_INF = -1e30
IDX_SCALE = (IDX_HEADS * IDX_DIM) ** -0.5
ATTN_SCALE = HEAD_DIM ** -0.5
MEM_SCALE = MEM_HEAD_DIM ** -0.5

LANES = 128
SUBLANES = 8
MXU_ROWS = 16
VMEM_LIMIT_BYTES = 56 * 1024 * 1024

COL_CIN = 0
COL_CB = COL_CIN + D_CONV
COL_CC = COL_CB + D_CONV
COL_Q = COL_CC + D_CONV
COL_K = COL_Q + N_HEADS * HEAD_DIM
COL_V = COL_K + N_KV_HEADS * HEAD_DIM
COL_IQ = COL_V + N_KV_HEADS * HEAD_DIM
COL_IK = COL_IQ + IDX_HEADS * IDX_DIM
COL_IW = COL_IK + IDX_DIM
IDX_RAW_END = COL_IW + IDX_HEADS
W_COLS = -(-IDX_RAW_END // LANES) * LANES

Q_BLOCK = 128
KEY_CHUNK = 512
COUNT_ROWS = 64
PLANE_ROWS = KEY_CHUNK // 32
SAMPLE_SEARCH_BITS = 4
SAMPLE_REQS = 4
FF_CHUNK = D_FF // 2
INT_MIN = -2 ** 31


def _rms(x, g):
    return x * lax.rsqrt(jnp.mean(x * x, axis=-1, keepdims=True) + EPS) * g


def _dot(a, b):
    return jnp.dot(a, b, preferred_element_type=F32)


def _dot_nt(a, b):
    return lax.dot_general(a, b, (((1,), (1,)), ((), ())), preferred_element_type=F32)


def _sigmoid(x):
    return 1.0 / (1.0 + jnp.exp(-x))


def _key_to_float(ukey):
    skey = ukey ^ INT_MIN
    bits = jnp.where(skey < 0, skey ^ 0x7FFFFFFF, skey)
    return lax.bitcast_convert_type(bits, F32)


def _float_to_key(x):
    bits = lax.bitcast_convert_type(x, I32)
    return bits ^ (lax.shift_right_arithmetic(bits, jnp.full_like(bits, 31)) & 0x7FFFFFFF) ^ INT_MIN


def _kth_largest(count_ge, top_k, shape):
    def body(b, ukey):
        cand = ukey | jnp.left_shift(jnp.int32(1), 31 - b)
        ok = count_ge(_key_to_float(cand)) >= float(top_k)
        return jnp.where(ok, cand, ukey)

    ukey = lax.fori_loop(0, 32, body, jnp.zeros(shape, I32))
    return _key_to_float(ukey)


def _kth_largest_by_digits(count_ge_fns, top_k, bits):
    digits = lax.broadcasted_iota(I32, (2 ** bits, 1), 0)
    digits_f = digits.astype(F32)

    def body(rd, ukeys):
        shift = 32 - bits * (rd + 1)
        out = []
        for count_ge, ukey in zip(count_ge_fns, ukeys):
            cand = ukey | jnp.left_shift(digits, shift)
            ok = count_ge(_key_to_float(cand)) >= float(top_k)
            digit = jnp.max(jnp.where(ok, digits_f, 0.0), axis=0, keepdims=True).astype(I32)
            out.append(ukey | jnp.left_shift(digit, shift))
        return tuple(out)

    ukeys = lax.fori_loop(0, 32 // bits, body, tuple(jnp.zeros((1, 1), I32) for _ in count_ge_fns))
    return [_key_to_float(u) for u in ukeys]


def _bit_transpose32(words):
    a = list(words)
    j, m = 16, 0x0000FFFF
    while j:
        k = 0
        while k < 32:
            t = (a[k] ^ lax.shift_right_logical(a[k + j], jnp.full_like(a[k], j))) & m
            a[k] = a[k] ^ t
            a[k + j] = a[k + j] ^ lax.shift_left(t, jnp.full_like(t, j))
            k = (k + j + 1) & ~j
        j >>= 1
        m = (m ^ (m << j)) & 0xFFFFFFFF
    return a


def _radix_select(planes_ref, live, top_k):
    def body(i, carry):
        live, ukey, n_above = carry
        hit = live & planes_ref[i]
        cnt = jnp.sum(lax.population_count(hit).astype(F32), axis=0, keepdims=True)
        take = n_above + cnt >= float(top_k)
        live = jnp.where(take, hit, live ^ hit)
        ukey = jnp.where(take, ukey | jnp.left_shift(jnp.int32(1), 31 - i), ukey)
        n_above = jnp.where(take, n_above, n_above + cnt)
        return live, ukey, n_above

    lanes = live.shape[1]
    init = (live, jnp.zeros((1, lanes), I32), jnp.zeros((1, lanes), F32))
    return lax.fori_loop(0, 32, body, init)[1]


def _mixer_in_kernel(is_prompt, tm, *refs):
    if is_prompt:
        (x_ref, g_ref, w_ref, wgate_ref, cw_ref, wco_ref,
         kf_ref, vf_ref, ikf_ref, oa_ref, sgb_ref,
         qt_ref, kb_ref, vt_ref, iqt_ref, ikb_ref, iwt_ref, cs_ref, ubuf) = refs
    else:
        (x_ref, g_ref, w_ref, wgate_ref, cw_ref, wco_ref, s0_ref, s1_ref,
         kf_ref, vf_ref, ikf_ref, oa_ref, sgb_ref,
         q_ref, iq_ref, iw_ref, u_ref) = refs

    h = _rms(x_ref[...], g_ref[...]).astype(BF16)

    def proj(lo, hi):
        return _dot_nt(h, w_ref[lo:hi, :])

    u = proj(COL_CC, COL_CC + D_CONV) * proj(COL_CIN, COL_CIN + D_CONV)
    cw = cw_ref[...]
    if is_prompt:
        @pl.when(pl.program_id(1) == 0)
        def _():
            ubuf[0:SUBLANES, :] = jnp.zeros((SUBLANES, D_CONV), F32)

        ubuf[SUBLANES:SUBLANES + tm, :] = u
        conv = (ubuf[SUBLANES - 2:SUBLANES - 2 + tm, :] * cw[0:1]
                + ubuf[SUBLANES - 1:SUBLANES - 1 + tm, :] * cw[1:2] + u * cw[2:3])
        ubuf[0:SUBLANES, :] = ubuf[tm:tm + SUBLANES, :]
        cs_ref[0] = u[tm - (CONV_WIDTH - 1):, :]
    else:
        conv = s0_ref[...] * cw[0:1] + s1_ref[...] * cw[1:2] + u * cw[2:3]
        u_ref[...] = u

    a_in = (proj(COL_CB, COL_CB + D_CONV) * conv).astype(BF16)
    out_a = _dot(a_in, wco_ref[...])
    oa_ref[...] = (_sigmoid(_dot_nt(h, wgate_ref[:D_MODEL, :])) * out_a).astype(BF16)
    sgb_ref[...] = _sigmoid(_dot_nt(h, wgate_ref[D_MODEL:, :])).astype(BF16)

    q = proj(COL_Q, COL_K) * ATTN_SCALE
    kv = proj(COL_K, COL_IQ)
    k = kv[:, :N_KV_HEADS * HEAD_DIM]
    v = kv[:, N_KV_HEADS * HEAD_DIM:]
    kf_ref[...] = k
    vf_ref[...] = v
    idx = proj(COL_IQ, W_COLS)
    iq = idx[:, :IDX_HEADS * IDX_DIM]
    ikw = idx[:, COL_IK - COL_IQ:]
    ik = ikw[:, :IDX_DIM]
    ikf_ref[...] = ik
    if is_prompt:
        kb_ref[...] = k.astype(BF16)
        ikb_ref[...] = ik.astype(BF16)
        vt_ref[0] = v.T.astype(BF16)
        for j in range(tm // Q_BLOCK):
            rows = slice(j * Q_BLOCK, (j + 1) * Q_BLOCK)
            for head in range(N_HEADS):
                g, hh = divmod(head, GROUP)
                qt_ref[j, g, :, hh * Q_BLOCK:(hh + 1) * Q_BLOCK] = (
                    q[rows, head * HEAD_DIM:(head + 1) * HEAD_DIM].T.astype(BF16))
            iqt_ref[j] = iq[rows].T.astype(BF16)
            iwt_ref[j] = ikw[rows].T[IDX_DIM:IDX_DIM + SUBLANES] * IDX_SCALE
    else:
        q_ref[...] = q.astype(BF16)
        for hd in range(IDX_HEADS):
            iq_ref[hd] = iq[:, hd * IDX_DIM:(hd + 1) * IDX_DIM].astype(BF16)
        iw_ref[...] = ikw[:, IDX_DIM:IDX_DIM + IDX_HEADS] * IDX_SCALE


def _const_spec(shape):
    nd = len(shape)
    return pl.BlockSpec(shape, lambda *_: (0,) * nd, pipeline_mode=pl.Buffered(1))


def _mixer_in(x, g_mix, w_main, w_gates, conv_w, w_conv_out, *, batch, seq, tm, state=None):
    is_prompt = state is None
    t_all = batch * seq
    nt = seq // tm if is_prompt else 1
    grid = (batch, nt) if is_prompt else (1, 1)
    tok = lambda width: pl.BlockSpec((tm, width), lambda b, t: (b * nt + t, 0))
    in_specs = [tok(D_MODEL), _const_spec((1, D_MODEL)), _const_spec((W_COLS, D_MODEL)), _const_spec((2 * D_MODEL, D_MODEL)),
                _const_spec((CONV_WIDTH, D_CONV)), _const_spec((D_CONV, D_MODEL))]
    args = [x, g_mix, w_main, w_gates, conv_w, w_conv_out]
    kvw = N_KV_HEADS * HEAD_DIM
    out_shapes = [
        jax.ShapeDtypeStruct((t_all, kvw), F32),
        jax.ShapeDtypeStruct((t_all, kvw), F32),
        jax.ShapeDtypeStruct((t_all, IDX_DIM), F32),
        jax.ShapeDtypeStruct((t_all, D_MODEL), BF16),
        jax.ShapeDtypeStruct((t_all, D_MODEL), BF16),
    ]
    out_specs = [tok(s.shape[1]) for s in out_shapes]
    scratch = []
    if is_prompt:
        assert tm == KEY_CHUNK and tm % Q_BLOCK == 0
        qb = tm // Q_BLOCK
        nblk = t_all // Q_BLOCK
        per_qblock = lambda *shape: pl.BlockSpec((qb,) + shape, lambda b, t: (b * nt + t,) + (0,) * len(shape))
        out_shapes += [
            jax.ShapeDtypeStruct((nblk, N_KV_HEADS, HEAD_DIM, GROUP * Q_BLOCK), BF16),
            jax.ShapeDtypeStruct((t_all, kvw), BF16),
            jax.ShapeDtypeStruct((t_all // tm, kvw, tm), BF16),
            jax.ShapeDtypeStruct((nblk, IDX_HEADS * IDX_DIM, Q_BLOCK), BF16),
            jax.ShapeDtypeStruct((t_all, IDX_DIM), BF16),
            jax.ShapeDtypeStruct((nblk, SUBLANES, Q_BLOCK), F32),
            jax.ShapeDtypeStruct((batch, CONV_WIDTH - 1, D_CONV), F32),
        ]
        out_specs += [
            per_qblock(N_KV_HEADS, HEAD_DIM, GROUP * Q_BLOCK), tok(kvw),
            pl.BlockSpec((1, kvw, tm), lambda b, t: (b * nt + t, 0, 0)),
            per_qblock(IDX_HEADS * IDX_DIM, Q_BLOCK), tok(IDX_DIM), per_qblock(SUBLANES, Q_BLOCK),
            pl.BlockSpec((1, CONV_WIDTH - 1, D_CONV), lambda b, t: (b, 0, 0)),
        ]
        scratch.append(pltpu.VMEM((tm + SUBLANES, D_CONV), F32))
    else:
        in_specs += [tok(D_CONV), tok(D_CONV)]
        args += list(state)
        out_shapes += [
            jax.ShapeDtypeStruct((t_all, N_HEADS * HEAD_DIM), BF16),
            jax.ShapeDtypeStruct((IDX_HEADS, t_all, IDX_DIM), BF16),
            jax.ShapeDtypeStruct((t_all, IDX_HEADS), F32),
            jax.ShapeDtypeStruct((t_all, D_CONV), F32),
        ]
        out_specs += [tok(N_HEADS * HEAD_DIM),
                      pl.BlockSpec((IDX_HEADS, tm, IDX_DIM), lambda b, t: (0, b * nt + t, 0)),
                      tok(IDX_HEADS), tok(D_CONV)]
    return pl.pallas_call(
        functools.partial(_mixer_in_kernel, is_prompt, tm),
        grid=grid, in_specs=in_specs, out_specs=out_specs, out_shape=out_shapes,
        scratch_shapes=scratch,
        compiler_params=pltpu.CompilerParams(
            dimension_semantics=("arbitrary", "arbitrary"), vmem_limit_bytes=VMEM_LIMIT_BYTES),
        name="mixer_in_prompt" if is_prompt else "mixer_in_sample",
    )(*args)


def _prompt_attn_kernel(top_k, qt_ref, iqt_ref, iwt_ref, k_ref, vt_ref, ik_ref, low_ref, o_ref,
                        sc_ref, planes_ref, m_ref, acc_ref):
    i = pl.program_id(1)
    nch = i // (KEY_CHUNK // Q_BLOCK) + 1
    qpos = i * Q_BLOCK + lax.broadcasted_iota(I32, (1, Q_BLOCK), 1)

    @pl.when(i == 0)
    def _():
        planes_ref[...] = jnp.zeros(planes_ref.shape, I32)

    iw = iwt_ref[0]
    iqt = iqt_ref[0]
    iq_pairs = [jnp.concatenate([iqt[h * IDX_DIM:(h + 1) * IDX_DIM] for h in (2 * pr, 2 * pr + 1)], axis=1)
                for pr in range(IDX_HEADS // 2)]

    def score_chunk(c, carry):
        off = pl.multiple_of(c * KEY_CHUNK, KEY_CHUNK)
        ikc = ik_ref[pl.ds(off, KEY_CHUNK), :]
        acc = jnp.zeros((KEY_CHUNK, Q_BLOCK), F32)
        for pr in range(IDX_HEADS // 2):
            s2 = _dot(ikc, iq_pairs[pr])
            for e in range(2):
                h = 2 * pr + e
                acc = acc + jnp.maximum(s2[:, e * Q_BLOCK:(e + 1) * Q_BLOCK], 0.0) * iw[h:h + 1]
        kpos = off + lax.broadcasted_iota(I32, (KEY_CHUNK, Q_BLOCK), 0)
        sc = jnp.where(kpos <= qpos, acc, -jnp.inf)
        sc_ref[c] = sc
        keys = _float_to_key(sc)
        for wd in range(PLANE_ROWS // SUBLANES):
            words = [keys[(wd * 32 + j) * SUBLANES:(wd * 32 + j + 1) * SUBLANES] for j in range(32)]
            row = pl.multiple_of(c * PLANE_ROWS + wd * SUBLANES, SUBLANES)
            for b, plane in enumerate(_bit_transpose32(words)):
                planes_ref[b, pl.ds(row, SUBLANES), :] = plane
        return carry

    lax.fori_loop(0, nch, score_chunk, 0)

    def count(cmp, t):
        def body(c, acc):
            hit = jnp.where(cmp(sc_ref[c], t), 1.0, 0.0)
            return acc + jnp.sum(hit.reshape(KEY_CHUNK // COUNT_ROWS, COUNT_ROWS, Q_BLOCK), axis=0)

        acc = lax.fori_loop(0, nch, body, jnp.zeros((COUNT_ROWS, Q_BLOCK), F32))
        return jnp.sum(acc, axis=0, keepdims=True)

    ge = lambda a, b: a >= b
    gt = lambda a, b: a > b
    few = qpos < top_k
    plane_row = lax.broadcasted_iota(I32, (planes_ref.shape[1], Q_BLOCK), 0)
    t = _key_to_float(_radix_select(planes_ref, jnp.where(plane_row < nch * PLANE_ROWS, -1, 0), top_k))
    n_gt = count(gt, t)
    n_ge = count(ge, t)
    is_kth = few | ((n_gt < float(top_k)) & (n_ge >= float(top_k)))

    def recount():
        t2 = _kth_largest(functools.partial(count, ge), top_k, (1, Q_BLOCK))
        return t2, count(gt, t2), count(ge, t2)

    t, n_gt, n_ge = lax.cond(jnp.min(jnp.where(is_kth, 1.0, 0.0)) > 0.0, lambda: (t, n_gt, n_ge), recount)
    surplus = jnp.where(few, 0.0, n_ge - float(top_k))
    ties_matter = jnp.max(surplus) > 0.0
    t = jnp.where(few, -jnp.inf, t)
    need = jnp.where(few, 0.0, float(top_k) - n_gt)

    m_ref[...] = jnp.full(m_ref.shape, NEG_INF, F32)
    acc_ref[...] = jnp.zeros(acc_ref.shape, F32)
    ones_rows = jnp.ones((MXU_ROWS, KEY_CHUNK), BF16)

    def attn_chunk(c, n_eq):
        off = pl.multiple_of(c * KEY_CHUNK, KEY_CHUNK)
        blk = sc_ref[c]
        eq = blk == t
        eqf = jnp.where(eq, 1.0, 0.0)
        before = lax.cond(ties_matter,
                          lambda: n_eq + _dot(low_ref[...], eqf.astype(BF16)),
                          lambda: jnp.zeros((KEY_CHUNK, Q_BLOCK), F32))
        sel = (blk > t) | (eq & (before < need))
        bias = jnp.where(sel, 0.0, NEG_INF)
        bias = jnp.concatenate([bias] * GROUP, axis=1)
        for g in range(N_KV_HEADS):
            kc = k_ref[pl.ds(off, KEY_CHUNK), g * HEAD_DIM:(g + 1) * HEAD_DIM]
            s = _dot(kc, qt_ref[0, g]) + bias
            m_old = m_ref[g]
            m_new = jnp.maximum(m_old, jnp.max(s, axis=0, keepdims=True))
            alpha = jnp.exp(m_old - m_new)
            p = jnp.exp(s - m_new).astype(BF16)
            vext = jnp.concatenate([vt_ref[c, g * HEAD_DIM:(g + 1) * HEAD_DIM, :], ones_rows], axis=0)
            acc_ref[g] = alpha * acc_ref[g] + _dot(vext, p)
            m_ref[g] = m_new
        return n_eq + jnp.sum(eqf, axis=0, keepdims=True)

    lax.fori_loop(0, nch, attn_chunk, jnp.zeros((1, Q_BLOCK), F32))

    for g in range(N_KV_HEADS):
        acc = acc_ref[g]
        o = acc[:HEAD_DIM] / acc[HEAD_DIM:HEAD_DIM + 1]
        for hh in range(GROUP):
            col = (g * GROUP + hh) * HEAD_DIM
            o_ref[:, col:col + HEAD_DIM] = o[:, hh * Q_BLOCK:(hh + 1) * Q_BLOCK].T.astype(o_ref.dtype)


def _prompt_attn(qt, iqt, iwt, kb, vt, ikb, *, batch, seq):
    nqb = seq // Q_BLOCK
    nch = seq // KEY_CHUNK
    kvw = N_KV_HEADS * HEAD_DIM
    top_k = min(TOP_K_MAX, seq // 4)
    low = jnp.tril(jnp.ones((KEY_CHUNK, KEY_CHUNK), BF16), k=-1)
    per_qblock = lambda *shape: pl.BlockSpec((1,) + shape, lambda b, i: (b * nqb + i,) + (0,) * len(shape))
    per_batch = lambda width: pl.BlockSpec((seq, width), lambda b, i: (b, 0))
    return pl.pallas_call(
        functools.partial(_prompt_attn_kernel, top_k),
        grid=(batch, nqb),
        in_specs=[per_qblock(N_KV_HEADS, HEAD_DIM, GROUP * Q_BLOCK),
                  per_qblock(IDX_HEADS * IDX_DIM, Q_BLOCK), per_qblock(SUBLANES, Q_BLOCK),
                  per_batch(kvw), pl.BlockSpec((nch, kvw, KEY_CHUNK), lambda b, i: (b, 0, 0)),
                  per_batch(IDX_DIM), _const_spec((KEY_CHUNK, KEY_CHUNK))],
        out_specs=pl.BlockSpec((Q_BLOCK, N_HEADS * HEAD_DIM), lambda b, i: (b * nqb + i, 0)),
        out_shape=jax.ShapeDtypeStruct((batch * seq, N_HEADS * HEAD_DIM), BF16),
        scratch_shapes=[
            pltpu.VMEM((nch, KEY_CHUNK, Q_BLOCK), F32),
            pltpu.VMEM((32, nch * PLANE_ROWS, Q_BLOCK), I32),
            pltpu.VMEM((N_KV_HEADS, 1, GROUP * Q_BLOCK), F32),
            pltpu.VMEM((N_KV_HEADS, HEAD_DIM + MXU_ROWS, GROUP * Q_BLOCK), F32),
        ],
        compiler_params=pltpu.CompilerParams(
            dimension_semantics=("arbitrary", "arbitrary"), vmem_limit_bytes=VMEM_LIMIT_BYTES),
        name="prompt_attn",
    )(qt, iqt, iwt, kb, vt, ikb, low)


def _sample_attn_kernel(n_pages, page, top_k, rb, pt_ref, q_ref, iq_ref, iw_ref, kn_ref, vn_ref, ikn_ref,
                        ck_hbm, cv_hbm, cik_hbm, tri_ref, low_ref, o_ref,
                        kbuf, vbuf, ikbuf, sems, sc_ref):
    b = pl.program_id(0)
    nb = pl.num_programs(0)
    past = n_pages * page

    def page_copies(step, slot):
        copies = []
        for r in range(rb):
            for p in range(n_pages):
                phys = pt_ref[step * rb + r, p]
                rows = pl.ds(p * page * N_KV_HEADS, page * N_KV_HEADS)
                lanes = pl.ds(p * page, page)
                copies.append(pltpu.make_async_copy(ck_hbm.at[phys], kbuf.at[slot, r, rows], sems.at[0, slot]))
                copies.append(pltpu.make_async_copy(cv_hbm.at[phys], vbuf.at[slot, r, rows], sems.at[1, slot]))
                copies.append(
                    pltpu.make_async_copy(cik_hbm.at[phys], ikbuf.at[slot, r, :, lanes], sems.at[2, slot]))
        return copies

    slot = b % 2

    @pl.when(b == 0)
    def _():
        for c in page_copies(0, 0):
            c.start()

    @pl.when(b + 1 < nb)
    def _():
        for c in page_copies(b + 1, 1 - slot):
            c.start()

    for c in page_copies(b, slot):
        c.wait()

    sc_rows, sc_news = [], []
    for r in range(rb):
        iq = iq_ref[r].astype(BF16)
        iw = iw_ref[r]
        sidx = _dot(iq, ikbuf[slot, r].astype(BF16))
        sc_row = jnp.sum(jnp.maximum(sidx, 0.0) * iw, axis=0, keepdims=True)
        for p in range(n_pages):
            sc_ref[r, p:p + 1, :] = sc_row[:, p * page:(p + 1) * page]
        ikn = ikn_ref[r].astype(BF16).astype(F32)
        s_new = jnp.sum(iq.astype(F32) * ikn, axis=1, keepdims=True)
        sc_rows.append(sc_row)
        sc_news.append(jnp.sum(jnp.maximum(s_new, 0.0) * iw, axis=0, keepdims=True))

    def count_ge(r, cand):
        hits = jnp.where(sc_rows[r] >= cand, 1.0, 0.0)
        return jnp.sum(hits, axis=1, keepdims=True) + jnp.where(sc_news[r] >= cand, 1.0, 0.0)

    ts = _kth_largest_by_digits([functools.partial(count_ge, r) for r in range(rb)], top_k,
                                SAMPLE_SEARCH_BITS)

    def total(x):
        return jnp.sum(jnp.sum(x, axis=1, keepdims=True), axis=0, keepdims=True)

    for r in range(rb):
        t, sc, sc_new = ts[r], sc_ref[r], sc_news[r]
        n_gt = total(jnp.where(sc > t, 1.0, 0.0)) + jnp.where(sc_new > t, 1.0, 0.0)
        need = float(top_k) - n_gt
        eq = sc == t
        eqf = jnp.where(eq, 1.0, 0.0)
        in_row = _dot(eqf.astype(BF16), tri_ref[...])
        row_tot = jnp.broadcast_to(jnp.sum(eqf, axis=1, keepdims=True), sc.shape)
        rows_before = _dot(low_ref[...], row_tot.astype(BF16))
        sel = (sc > t) | (eq & (in_row + rows_before < need))
        sel_new = (sc_new > t) | ((sc_new == t) & (total(eqf) < need))
        bias = jnp.where(sel, 0.0, NEG_INF)
        bias_row = jnp.concatenate([bias[p:p + 1, :] for p in range(n_pages)], axis=1)
        bias_new = jnp.where(sel_new, 0.0, NEG_INF)

        q = q_ref[r].astype(BF16)
        qf = q.astype(F32)
        for g in range(N_KV_HEADS):
            head_rows = pl.ds(g, past, stride=N_KV_HEADS)
            kg = kbuf[slot, r, head_rows, :].astype(BF16)
            vg = vbuf[slot, r, head_rows, :].astype(BF16)
            kn = kn_ref[r, g:g + 1, :].astype(BF16).astype(F32)
            vn = vn_ref[r, g:g + 1, :].astype(BF16).astype(F32)
            s = _dot_nt(q, kg) + bias_row
            sn = jnp.sum(qf * kn, axis=1, keepdims=True) + bias_new
            m = jnp.maximum(jnp.max(s, axis=1, keepdims=True), sn)
            p = jnp.exp(s - m)
            pn = jnp.exp(sn - m)
            l = jnp.sum(p, axis=1, keepdims=True) + pn
            o = (_dot(p.astype(BF16), vg) + pn * vn) / l
            o_ref[r, g * GROUP:(g + 1) * GROUP, :] = o[g * GROUP:(g + 1) * GROUP]


def _sample_attn(page_table, q, iq, iw, k_new, v_new, ik_new, cache_k, cache_v, cache_ik):
    nreq, n_pages = page_table.shape
    n_phys, page = cache_k.shape[0], cache_k.shape[1]
    kvw = N_KV_HEADS * HEAD_DIM
    top_k = min(TOP_K_MAX, (n_pages * page + 1) // 4)
    assert top_k < n_pages * page + 1
    ck = cache_k.reshape(n_phys, page * N_KV_HEADS, HEAD_DIM)
    cv = cache_v.reshape(n_phys, page * N_KV_HEADS, HEAD_DIM)
    cik = jnp.swapaxes(cache_ik, 1, 2)
    tri = jnp.triu(jnp.ones((page, page), BF16), k=1)
    low = jnp.tril(jnp.ones((n_pages, n_pages), BF16), k=-1)
    rb = SAMPLE_REQS
    assert nreq % rb == 0
    pad_rows = lambda a: jnp.pad(a.astype(F32), ((0, 0), (0, MXU_ROWS - a.shape[1]), (0, 0)))
    per_req = lambda *shape: pl.BlockSpec((rb,) + shape, lambda b, pt: (b,) + (0,) * len(shape))
    const = lambda *shape: pl.BlockSpec(shape, lambda b, pt: (0,) * len(shape))
    any_spec = pl.BlockSpec(memory_space=pl.ANY)
    grid_spec = pltpu.PrefetchScalarGridSpec(
        num_scalar_prefetch=1,
        grid=(nreq // rb,),
        in_specs=[per_req(MXU_ROWS, HEAD_DIM), per_req(MXU_ROWS, IDX_DIM), per_req(MXU_ROWS, 1),
                  per_req(N_KV_HEADS, HEAD_DIM), per_req(N_KV_HEADS, HEAD_DIM), per_req(1, IDX_DIM),
                  any_spec, any_spec, any_spec, const(page, page), const(n_pages, n_pages)],
        out_specs=per_req(N_HEADS, HEAD_DIM),
        scratch_shapes=[
            pltpu.VMEM((2, rb, n_pages * page * N_KV_HEADS, HEAD_DIM), F32),
            pltpu.VMEM((2, rb, n_pages * page * N_KV_HEADS, HEAD_DIM), F32),
            pltpu.VMEM((2, rb, IDX_DIM, n_pages * page), F32),
            pltpu.SemaphoreType.DMA((3, 2)),
            pltpu.VMEM((rb, n_pages, page), F32),
        ],
    )
    out = pl.pallas_call(
        functools.partial(_sample_attn_kernel, n_pages, page, top_k, rb),
        grid_spec=grid_spec,
        out_shape=jax.ShapeDtypeStruct((nreq, N_HEADS, HEAD_DIM), F32),
        compiler_params=pltpu.CompilerParams(
            dimension_semantics=("arbitrary",), vmem_limit_bytes=VMEM_LIMIT_BYTES),
        name="sample_attn",
    )(page_table,
      pad_rows(q.reshape(nreq, N_HEADS, HEAD_DIM)),
      pad_rows(jnp.transpose(iq, (1, 0, 2))),
      pad_rows(iw.reshape(nreq, IDX_HEADS, 1)),
      k_new.reshape(nreq, N_KV_HEADS, HEAD_DIM), v_new.reshape(nreq, N_KV_HEADS, HEAD_DIM),
      ik_new.reshape(nreq, 1, IDX_DIM),
      ck, cv, cik, tri, low)
    return out.reshape(nreq, N_HEADS * HEAD_DIM)


def _merge_stage(x, attn, oa, sgb, wao_ref, wo_ref, gmem_ref, wmq_ref):
    out_b = _dot(attn, wao_ref[...])
    merged = oa.astype(F32) + sgb.astype(F32) * out_b
    x1 = x + _dot(merged.astype(BF16), wo_ref[...])
    hm = _rms(x1, gmem_ref[...]).astype(BF16)
    return x1, _dot(hm, wmq_ref[...]) * MEM_SCALE


def _ffn_stage(x1, mem_o, wmo_ref, gffn_ref, wg_ref, wu_ref, wd_ref, gfin_ref):
    x2 = x1 + _dot(mem_o.astype(BF16), wmo_ref[...])
    hf = _rms(x2, gffn_ref[...]).astype(BF16)
    acc = jnp.zeros_like(x2)
    for c in range(D_FF // FF_CHUNK):
        cols = slice(c * FF_CHUNK, (c + 1) * FF_CHUNK)
        gate = _dot(hf, wg_ref[:, cols])
        f = gate * _sigmoid(gate) * _dot(hf, wu_ref[:, cols])
        acc = acc + _dot(f.astype(BF16), wd_ref[cols, :])
    return _rms(x2 + acc, gfin_ref[...])


def _post_prompt_kernel(x_ref, attn_ref, oa_ref, sgb_ref, mk_ref, mv_ref, wao_ref, wo_ref, gmem_ref,
                        wmq_ref, wmo_ref, gffn_ref, wg_ref, wu_ref, wd_ref, gfin_ref, y_ref, mo_ref):
    x1, qm = _merge_stage(x_ref[...], attn_ref[...], oa_ref[...], sgb_ref[...],
                          wao_ref, wo_ref, gmem_ref, wmq_ref)
    for h in range(MEM_HEADS):
        cols = slice(h * MEM_HEAD_DIM, (h + 1) * MEM_HEAD_DIM)
        s = _dot_nt(qm[:, cols].astype(BF16), mk_ref[:, cols])
        p = jnp.exp(s - jnp.max(s, axis=1, keepdims=True))
        o = _dot(p.astype(BF16), mv_ref[:, cols]) / jnp.sum(p, axis=1, keepdims=True)
        mo_ref[:, cols] = o.astype(BF16)
    y_ref[...] = _ffn_stage(x1, mo_ref[...], wmo_ref, gffn_ref, wg_ref, wu_ref, wd_ref, gfin_ref)


def _post_prompt(x, attn, oa, sgb, mk, mv, w, *, batch, seq, tm):
    nt = seq // tm
    n_mem = mk.shape[0] // batch
    tok = lambda width: pl.BlockSpec((tm, width), lambda b, t: (b * nt + t, 0))
    mem = pl.BlockSpec((n_mem, D_MODEL), lambda b, t: (b, 0))
    sq = _const_spec((D_MODEL, D_MODEL))
    gain = _const_spec((1, D_MODEL))
    return pl.pallas_call(
        _post_prompt_kernel,
        grid=(batch, nt),
        in_specs=[tok(D_MODEL), tok(D_MODEL), tok(D_MODEL), tok(D_MODEL), mem, mem,
                  sq, sq, gain, sq, sq, gain,
                  _const_spec((D_MODEL, D_FF)), _const_spec((D_MODEL, D_FF)), _const_spec((D_FF, D_MODEL)),
                  gain],
        out_specs=tok(D_MODEL),
        out_shape=jax.ShapeDtypeStruct((batch * seq, D_MODEL), F32),
        scratch_shapes=[pltpu.VMEM((tm, D_MODEL), BF16)],
        compiler_params=pltpu.CompilerParams(
            dimension_semantics=("arbitrary", "arbitrary"), vmem_limit_bytes=VMEM_LIMIT_BYTES),
        name="post_prompt",
    )(x, attn, oa, sgb, mk, mv, w["wao"], w["wo"], w["gmem"], w["wmq"], w["wmo"], w["gffn"],
      w["wg"], w["wu"], w["wd"], w["gfin"])


def _merge_sample_kernel(x_ref, attn_ref, oa_ref, sgb_ref, wao_ref, wo_ref, gmem_ref, wmq_ref,
                         x1_ref, qm_ref):
    x1, qm = _merge_stage(x_ref[...], attn_ref[...].astype(BF16), oa_ref[...], sgb_ref[...],
                          wao_ref, wo_ref, gmem_ref, wmq_ref)
    x1_ref[...] = x1
    qm_ref[...] = qm


def _mem_sample_kernel(rb, q_ref, mk_ref, mv_ref, o_ref):
    for r in range(rb):
        s = jnp.sum(mk_ref[r] * q_ref[r][None], axis=2, keepdims=True)
        p = jnp.exp(s - jnp.max(s, axis=0, keepdims=True))
        o_ref[r] = jnp.sum(p * mv_ref[r], axis=0) / jnp.sum(p, axis=0)


def _ffn_sample_kernel(x1_ref, mo_ref, wmo_ref, gffn_ref, wg_ref, wu_ref, wd_ref, gfin_ref, y_ref):
    y_ref[...] = _ffn_stage(x1_ref[...], mo_ref[...], wmo_ref, gffn_ref, wg_ref, wu_ref, wd_ref, gfin_ref)


def _post_sample(x, attn, oa, sgb, mem_k, mem_v, w):
    n = x.shape[0]
    n_mem = mem_k.shape[1]
    full = lambda *shape: _const_spec(shape)
    act = full(n, D_MODEL)
    sq = full(D_MODEL, D_MODEL)
    gain = full(1, D_MODEL)
    params = pltpu.CompilerParams(dimension_semantics=("arbitrary",), vmem_limit_bytes=VMEM_LIMIT_BYTES)
    x1, qm = pl.pallas_call(
        _merge_sample_kernel, grid=(1,),
        in_specs=[act, act, act, act, sq, sq, gain, sq], out_specs=[act, act],
        out_shape=[jax.ShapeDtypeStruct((n, D_MODEL), F32)] * 2,
        compiler_params=params, name="merge_sample",
    )(x, attn, oa, sgb, w["wao"], w["wo"], w["gmem"], w["wmq"])

    rb = 4
    heads = pl.BlockSpec((rb, MEM_HEADS, MEM_HEAD_DIM), lambda i: (i, 0, 0))
    mem = pl.BlockSpec((rb, n_mem, MEM_HEADS, MEM_HEAD_DIM), lambda i: (i, 0, 0, 0))
    mem_o = pl.pallas_call(
        functools.partial(_mem_sample_kernel, rb), grid=(n // rb,),
        in_specs=[heads, mem, mem], out_specs=heads,
        out_shape=jax.ShapeDtypeStruct((n, MEM_HEADS, MEM_HEAD_DIM), F32),
        compiler_params=params, name="mem_sample",
    )(qm.reshape(n, MEM_HEADS, MEM_HEAD_DIM), mem_k, mem_v)

    return pl.pallas_call(
        _ffn_sample_kernel, grid=(1,),
        in_specs=[act, act, sq, gain, full(D_MODEL, D_FF), full(D_MODEL, D_FF), full(D_FF, D_MODEL), gain],
        out_specs=act, out_shape=jax.ShapeDtypeStruct((n, D_MODEL), F32),
        compiler_params=params, name="ffn_sample",
    )(x1, mem_o.reshape(n, D_MODEL), w["wmo"], w["gffn"], w["wg"], w["wu"], w["wd"], w["gfin"])


def _memory_kv_kernel(mem_ref, g_ref, wk_ref, wv_ref, kf_ref, vf_ref, kb_ref, vb_ref):
    m = _rms(mem_ref[...], g_ref[...]).astype(BF16)
    k = _dot(m, wk_ref[...])
    v = _dot(m, wv_ref[...])
    kf_ref[...] = k
    vf_ref[...] = v
    kb_ref[...] = k.astype(BF16)
    vb_ref[...] = v.astype(BF16)


def _memory_kv(mem, g, wk, wv):
    n = mem.shape[0]
    full = lambda *shape: _const_spec(shape)
    act = full(n, D_MODEL)
    sq = full(D_MODEL, D_MODEL)
    return pl.pallas_call(
        _memory_kv_kernel, grid=(1,),
        in_specs=[act, full(1, D_MODEL), sq, sq], out_specs=[act] * 4,
        out_shape=[jax.ShapeDtypeStruct((n, D_MODEL), F32)] * 2 + [jax.ShapeDtypeStruct((n, D_MODEL), BF16)] * 2,
        compiler_params=pltpu.CompilerParams(
            dimension_semantics=("arbitrary",), vmem_limit_bytes=VMEM_LIMIT_BYTES),
        name="memory_kv",
    )(mem, g, wk, wv)


def _prep_w_in(w_in):
    w_t = w_in.T
    main = jnp.pad(w_t[:IDX_RAW_END].astype(BF16), ((0, W_COLS - IDX_RAW_END), (0, 0)))
    return main, w_t[IDX_RAW_END:].astype(BF16)


def kernel(x_prompt, x_sample, mem_prompt, cache_k, cache_v, cache_idx_k, cache_mem_k, cache_mem_v, state_conv, page_table, g_mix, w_in, conv_w, w_conv_out, w_attn_out, w_o, g_mem, g_mem_kv, w_mq, w_mk, w_mv, w_mo, g_ffn, w_gate, w_up, w_down, g_final):
    depth = w_in.shape[0]
    assert depth == 1, "single-layer step"
    batch, seq, _ = x_prompt.shape
    nreq, dec_seq, _ = x_sample.shape
    assert dec_seq == 1
    n_mem = mem_prompt.shape[1]
    l = 0
    bf = lambda a: a.astype(BF16)
    row = lambda a: a.reshape(1, -1)

    w_main, w_gates = _prep_w_in(w_in[l])
    wco = bf(w_conv_out[l])
    w = dict(wao=bf(w_attn_out[l]), wo=bf(w_o[l]), gmem=row(g_mem[l]), wmq=bf(w_mq[l]), wmo=bf(w_mo[l]),
             gffn=row(g_ffn[l]), wg=bf(w_gate[l]), wu=bf(w_up[l]), wd=bf(w_down[l]), gfin=row(g_final))

    xp = x_prompt.reshape(batch * seq, D_MODEL)
    mkf, mvf, mkb, mvb = _memory_kv(mem_prompt.reshape(batch * n_mem, D_MODEL), row(g_mem_kv[l]),
                                    bf(w_mk[l]), bf(w_mv[l]))
    (kf, vf, ikf, oa, sgb, qt, kb, vt, iqt, ikb, iwt, conv_p) = _mixer_in(
        xp, row(g_mix[l]), w_main, w_gates, conv_w[l], wco, batch=batch, seq=seq, tm=KEY_CHUNK)
    attn = _prompt_attn(qt, iqt, iwt, kb, vt, ikb, batch=batch, seq=seq)
    yp = _post_prompt(xp, attn, oa, sgb, mkb, mvb, w, batch=batch, seq=seq, tm=256)

    xs = x_sample.reshape(nreq, D_MODEL)
    st = state_conv[l]
    (kf_s, vf_s, ikf_s, oa_s, sgb_s, q_s, iq_s, iw_s, u_s) = _mixer_in(
        xs, row(g_mix[l]), w_main, w_gates, conv_w[l], wco, batch=nreq, seq=1, tm=nreq, state=(st[:, 0], st[:, 1]))
    attn_s = _sample_attn(page_table, q_s, iq_s, iw_s, kf_s, vf_s, ikf_s,
                          cache_k[l], cache_v[l], cache_idx_k[l])
    ys = _post_sample(xs, attn_s, oa_s, sgb_s, cache_mem_k[l], cache_mem_v[l], w)

    return (
        yp.reshape(batch, seq, D_MODEL),
        ys.reshape(nreq, 1, D_MODEL),
        kf.reshape(1, batch, seq, N_KV_HEADS, HEAD_DIM),
        vf.reshape(1, batch, seq, N_KV_HEADS, HEAD_DIM),
        ikf.reshape(1, batch, seq, IDX_DIM),
        conv_p.reshape(1, batch, CONV_WIDTH - 1, D_CONV),
        mkf.reshape(1, batch, n_mem, MEM_HEADS, MEM_HEAD_DIM),
        mvf.reshape(1, batch, n_mem, MEM_HEADS, MEM_HEAD_DIM),
        kf_s.reshape(1, nreq, 1, N_KV_HEADS, HEAD_DIM),
        vf_s.reshape(1, nreq, 1, N_KV_HEADS, HEAD_DIM),
        ikf_s.reshape(1, nreq, 1, IDX_DIM),
        jnp.stack([st[:, 1], u_s], axis=1).reshape(1, nreq, CONV_WIDTH - 1, D_CONV),
    )
```

```python
import functools

import jax
import jax.numpy as jnp
from jax import lax
from jax.experimental import pallas as pl
from jax.experimental.pallas import tpu as pltpu

F32 = jnp.float32
BF16 = jnp.bfloat16
I32 = jnp.int32

D_MODEL = 1024
D_CONV = 1024
CONV_WIDTH = 3
N_HEADS = 8
HEAD_DIM = 128
N_KV_HEADS = 2
GROUP = N_HEADS // N_KV_HEADS
IDX_HEADS = 4
IDX_DIM = 64
TOP_K_MAX = 256
MEM_HEADS = 4
MEM_HEAD_DIM = D_MODEL // MEM_HEADS
D_FF = 2816
EPS = 1e-6
NEG_INF = -1e30
IDX_SCALE = (IDX_HEADS * IDX_DIM) ** -0.5
ATTN_SCALE = HEAD_DIM ** -0.5
MEM_SCALE = MEM_HEAD_DIM ** -0.5

LANES = 128
SUBLANES = 8
MXU_ROWS = 16
VMEM_LIMIT_BYTES = 56 * 1024 * 1024

COL_CIN = 0
COL_CB = COL_CIN + D_CONV
COL_CC = COL_CB + D_CONV
COL_Q = COL_CC + D_CONV
COL_K = COL_Q + N_HEADS * HEAD_DIM
COL_V = COL_K + N_KV_HEADS * HEAD_DIM
COL_IQ = COL_V + N_KV_HEADS * HEAD_DIM
COL_IK = COL_IQ + IDX_HEADS * IDX_DIM
COL_IW = COL_IK + IDX_DIM
IDX_RAW_END = COL_IW + IDX_HEADS
W_COLS = -(-IDX_RAW_END // LANES) * LANES

Q_BLOCK = 128
KEY_CHUNK = 512
COUNT_ROWS = 64
PLANE_ROWS = KEY_CHUNK // 32
SAMPLE_SEARCH_BITS = 4
SAMPLE_REQS = 4
FF_CHUNK = D_FF // 2
INT_MIN = -2 ** 31


def _rms(x, g):
    return x * lax.rsqrt(jnp.mean(x * x, axis=-1, keepdims=True) + EPS) * g


def _dot(a, b):
    return jnp.dot(a, b, preferred_element_type=F32)


def _dot_nt(a, b):
    return lax.dot_general(a, b, (((1,), (1,)), ((), ())), preferred_element_type=F32)


def _sigmoid(x):
    return 1.0 / (1.0 + jnp.exp(-x))


def _key_to_float(ukey):
    skey = ukey ^ INT_MIN
    bits = jnp.where(skey < 0, skey ^ 0x7FFFFFFF, skey)
    return lax.bitcast_convert_type(bits, F32)


def _float_to_key(x):
    bits = lax.bitcast_convert_type(x, I32)
    return bits ^ (lax.shift_right_arithmetic(bits, jnp.full_like(bits, 31)) & 0x7FFFFFFF) ^ INT_MIN


def _kth_largest(count_ge, top_k, shape):
    def body(b, ukey):
        cand = ukey | jnp.left_shift(jnp.int32(1), 31 - b)
        ok = count_ge(_key_to_float(cand)) >= float(top_k)
        return jnp.where(ok, cand, ukey)

    ukey = lax.fori_loop(0, 32, body, jnp.zeros(shape, I32))
    return _key_to_float(ukey)


def _kth_largest_by_digits(count_ge_fns, top_k, bits):
    digits = jnp.minimum(lax.broadcasted_iota(I32, (2 ** bits, 1), 0) + 1, 2 ** bits - 1)
    digits_f = digits.astype(F32)

    def body(rd, ukeys):
        shift = 32 - bits * (rd + 1)
        out = []
        for count_ge, ukey in zip(count_ge_fns, ukeys):
            cand = ukey | jnp.left_shift(digits, shift)
            ok = count_ge(_key_to_float(cand)) >= float(top_k)
            digit = jnp.max(jnp.where(ok, digits_f, 0.0), axis=0, keepdims=True).astype(I32)
            out.append(ukey | jnp.left_shift(digit, shift))
        return tuple(out)

    ukeys = lax.fori_loop(0, 32 // bits, body, tuple(jnp.zeros((1, 1), I32) for _ in count_ge_fns))
    return [_key_to_float(u) for u in ukeys]


def _bit_transpose32(words):
    a = list(words)
    j, m = 16, 0x0000FFFF
    while j:
        k = 0
        while k < 32:
            t = (a[k] ^ lax.shift_right_logical(a[k + j], jnp.full_like(a[k], j))) & m
            a[k] = a[k] ^ t
            a[k + j] = a[k + j] ^ lax.shift_left(t, jnp.full_like(t, j))
            k = (k + j + 1) & ~j
        j >>= 1
        m = (m ^ (m << j)) & 0xFFFFFFFF
    return a


def _radix_select(planes_ref, live, top_k):
    def body(i, carry):
        live, ukey, n_above = carry
        hit = live & planes_ref[i]
        cnt = jnp.sum(lax.population_count(hit).astype(F32), axis=0, keepdims=True)
        take = n_above + cnt >= float(top_k)
        live = jnp.where(take, hit, live ^ hit)
        ukey = jnp.where(take, ukey | jnp.left_shift(jnp.int32(1), 31 - i), ukey)
        n_above = jnp.where(take, n_above, n_above + cnt)
        return live, ukey, n_above

    lanes = live.shape[1]
    init = (live, jnp.zeros((1, lanes), I32), jnp.zeros((1, lanes), F32))
    return lax.fori_loop(0, 32, body, init)[1]


def _mixer_in_kernel(is_prompt, tm, *refs):
    if is_prompt:
        (x_ref, g_ref, w_ref, wgate_ref, cw_ref, wco_ref,
         kf_ref, vf_ref, ikf_ref, oa_ref, sgb_ref,
         qt_ref, kb_ref, vt_ref, iqt_ref, ikb_ref, iwt_ref, cs_ref, ubuf) = refs
    else:
        (x_ref, g_ref, w_ref, wgate_ref, cw_ref, wco_ref, s0_ref, s1_ref,
         kf_ref, vf_ref, ikf_ref, oa_ref, sgb_ref,
         q_ref, iq_ref, iw_ref, u_ref) = refs

    h = _rms(x_ref[...], g_ref[...]).astype(BF16)

    def proj(lo, hi):
        return _dot_nt(h, w_ref[lo:hi, :])

    u = proj(COL_CC, COL_CC + D_CONV) * proj(COL_CIN, COL_CIN + D_CONV)
    cw = cw_ref[...]
    if is_prompt:
        @pl.when(pl.program_id(1) == 0)
        def _():
            ubuf[0:SUBLANES, :] = jnp.zeros((SUBLANES, D_CONV), F32)

        ubuf[SUBLANES:SUBLANES + tm, :] = u
        conv = (ubuf[SUBLANES - 2:SUBLANES - 2 + tm, :] * cw[0:1]
                + ubuf[SUBLANES - 1:SUBLANES - 1 + tm, :] * cw[1:2] + u * cw[2:3])
        ubuf[0:SUBLANES, :] = ubuf[tm:tm + SUBLANES, :]
        cs_ref[0] = u[tm - (CONV_WIDTH - 1):, :]
    else:
        conv = s0_ref[...] * cw[0:1] + s1_ref[...] * cw[1:2] + u * cw[2:3]
        u_ref[...] = u

    a_in = (proj(COL_CB, COL_CB + D_CONV) * conv).astype(BF16)
    out_a = _dot(a_in, wco_ref[...])
    oa_ref[...] = (_sigmoid(_dot_nt(h, wgate_ref[:D_MODEL, :])) * out_a).astype(BF16)
    sgb_ref[...] = _sigmoid(_dot_nt(h, wgate_ref[D_MODEL:, :])).astype(BF16)

    q = proj(COL_Q, COL_K) * ATTN_SCALE
    kv = proj(COL_K, COL_IQ)
    k = kv[:, :N_KV_HEADS * HEAD_DIM]
    v = kv[:, N_KV_HEADS * HEAD_DIM:]
    kf_ref[...] = k
    vf_ref[...] = v
    idx = proj(COL_IQ, W_COLS)
    iq = idx[:, :IDX_HEADS * IDX_DIM]
    ikw = idx[:, COL_IK - COL_IQ:]
    ik = ikw[:, :IDX_DIM]
    ikf_ref[...] = ik
    if is_prompt:
        kb_ref[...] = k.astype(BF16)
        ikb_ref[...] = ik.astype(BF16)
        vt_ref[0] = v.T.astype(BF16)
        for j in range(tm // Q_BLOCK):
            rows = slice(j * Q_BLOCK, (j + 1) * Q_BLOCK)
            for head in range(N_HEADS):
                g, hh = divmod(head, GROUP)
                qt_ref[j, g, :, hh * Q_BLOCK:(hh + 1) * Q_BLOCK] = (
                    q[rows, head * HEAD_DIM:(head + 1) * HEAD_DIM].T.astype(BF16))
            iqt_ref[j] = iq[rows].T.astype(BF16)
            iwt_ref[j] = ikw[rows].T[IDX_DIM:IDX_DIM + SUBLANES] * IDX_SCALE
    else:
        q_ref[...] = q.astype(BF16)
        for hd in range(IDX_HEADS):
            iq_ref[hd] = iq[:, hd * IDX_DIM:(hd + 1) * IDX_DIM].astype(BF16)
        iw_ref[...] = ikw[:, IDX_DIM:IDX_DIM + IDX_HEADS] * IDX_SCALE


def _const_spec(shape):
    nd = len(shape)
    return pl.BlockSpec(shape, lambda *_: (0,) * nd, pipeline_mode=pl.Buffered(1))


def _mixer_in(x, g_mix, w_main, w_gates, conv_w, w_conv_out, *, batch, seq, tm, state=None):
    is_prompt = state is None
    t_all = batch * seq
    nt = seq // tm if is_prompt else 1
    grid = (batch, nt) if is_prompt else (1, 1)
    tok = lambda width: pl.BlockSpec((tm, width), lambda b, t: (b * nt + t, 0))
    in_specs = [tok(D_MODEL), _const_spec((1, D_MODEL)), _const_spec((W_COLS, D_MODEL)), _const_spec((2 * D_MODEL, D_MODEL)),
                _const_spec((CONV_WIDTH, D_CONV)), _const_spec((D_CONV, D_MODEL))]
    args = [x, g_mix, w_main, w_gates, conv_w, w_conv_out]
    kvw = N_KV_HEADS * HEAD_DIM
    out_shapes = [
        jax.ShapeDtypeStruct((t_all, kvw), F32),
        jax.ShapeDtypeStruct((t_all, kvw), F32),
        jax.ShapeDtypeStruct((t_all, IDX_DIM), F32),
        jax.ShapeDtypeStruct((t_all, D_MODEL), BF16),
        jax.ShapeDtypeStruct((t_all, D_MODEL), BF16),
    ]
    out_specs = [tok(s.shape[1]) for s in out_shapes]
    scratch = []
    if is_prompt:
        assert tm == KEY_CHUNK and tm % Q_BLOCK == 0
        qb = tm // Q_BLOCK
        nblk = t_all // Q_BLOCK
        per_qblock = lambda *shape: pl.BlockSpec((qb,) + shape, lambda b, t: (b * nt + t,) + (0,) * len(shape))
        out_shapes += [
            jax.ShapeDtypeStruct((nblk, N_KV_HEADS, HEAD_DIM, GROUP * Q_BLOCK), BF16),
            jax.ShapeDtypeStruct((t_all, kvw), BF16),
            jax.ShapeDtypeStruct((t_all // tm, kvw, tm), BF16),
            jax.ShapeDtypeStruct((nblk, IDX_HEADS * IDX_DIM, Q_BLOCK), BF16),
            jax.ShapeDtypeStruct((t_all, IDX_DIM), BF16),
            jax.ShapeDtypeStruct((nblk, SUBLANES, Q_BLOCK), F32),
            jax.ShapeDtypeStruct((batch, CONV_WIDTH - 1, D_CONV), F32),
        ]
        out_specs += [
            per_qblock(N_KV_HEADS, HEAD_DIM, GROUP * Q_BLOCK), tok(kvw),
            pl.BlockSpec((1, kvw, tm), lambda b, t: (b * nt + t, 0, 0)),
            per_qblock(IDX_HEADS * IDX_DIM, Q_BLOCK), tok(IDX_DIM), per_qblock(SUBLANES, Q_BLOCK),
            pl.BlockSpec((1, CONV_WIDTH - 1, D_CONV), lambda b, t: (b, 0, 0)),
        ]
        scratch.append(pltpu.VMEM((tm + SUBLANES, D_CONV), F32))
    else:
        in_specs += [tok(D_CONV), tok(D_CONV)]
        args += list(state)
        out_shapes += [
            jax.ShapeDtypeStruct((t_all, N_HEADS * HEAD_DIM), BF16),
            jax.ShapeDtypeStruct((IDX_HEADS, t_all, IDX_DIM), BF16),
            jax.ShapeDtypeStruct((t_all, IDX_HEADS), F32),
            jax.ShapeDtypeStruct((t_all, D_CONV), F32),
        ]
        out_specs += [tok(N_HEADS * HEAD_DIM),
                      pl.BlockSpec((IDX_HEADS, tm, IDX_DIM), lambda b, t: (0, b * nt + t, 0)),
                      tok(IDX_HEADS), tok(D_CONV)]
    return pl.pallas_call(
        functools.partial(_mixer_in_kernel, is_prompt, tm),
        grid=grid, in_specs=in_specs, out_specs=out_specs, out_shape=out_shapes,
        scratch_shapes=scratch,
        compiler_params=pltpu.CompilerParams(
            dimension_semantics=("arbitrary", "arbitrary"), vmem_limit_bytes=VMEM_LIMIT_BYTES),
        name="mixer_in_prompt" if is_prompt else "mixer_in_sample",
    )(*args)


def _prompt_attn_kernel(top_k, qt_ref, iqt_ref, iwt_ref, k_ref, vt_ref, ik_ref, low_ref, o_ref,
                        sc_ref, planes_ref, m_ref, acc_ref, p_ref):
    i = pl.program_id(1)
    nch = i // (KEY_CHUNK // Q_BLOCK) + 1
    qpos = i * Q_BLOCK + lax.broadcasted_iota(I32, (1, Q_BLOCK), 1)

    @pl.when(i == 0)
    def _():
        planes_ref[...] = jnp.zeros(planes_ref.shape, I32)

    iw = iwt_ref[0]
    iqt = iqt_ref[0]
    iq_pairs = [jnp.concatenate([iqt[h * IDX_DIM:(h + 1) * IDX_DIM] for h in (2 * pr, 2 * pr + 1)], axis=1)
                for pr in range(IDX_HEADS // 2)]

    def score_chunk(c):
        off = pl.multiple_of(c * KEY_CHUNK, KEY_CHUNK)
        ikc = ik_ref[pl.ds(off, KEY_CHUNK), :]
        acc = jnp.zeros((KEY_CHUNK, Q_BLOCK), F32)
        for pr in range(IDX_HEADS // 2):
            s2 = _dot(ikc, iq_pairs[pr])
            for e in range(2):
                h = 2 * pr + e
                acc = acc + jnp.maximum(s2[:, e * Q_BLOCK:(e + 1) * Q_BLOCK], 0.0) * iw[h:h + 1]
        kpos = off + lax.broadcasted_iota(I32, (KEY_CHUNK, Q_BLOCK), 0)
        sc_ref[c] = jnp.where(kpos <= qpos, acc, -jnp.inf)

    def plane_chunk(c):
        keys = _float_to_key(sc_ref[c])
        for wd in range(PLANE_ROWS // SUBLANES):
            words = [keys[(wd * 32 + j) * SUBLANES:(wd * 32 + j + 1) * SUBLANES] for j in range(32)]
            row = pl.multiple_of(c * PLANE_ROWS + wd * SUBLANES, SUBLANES)
            for b, plane in enumerate(_bit_transpose32(words)):
                planes_ref[b, pl.ds(row, SUBLANES), :] = plane

    score_chunk(0)

    def score_step(c, carry):
        plane_chunk(c - 1)
        score_chunk(c)
        return carry

    lax.fori_loop(1, nch, score_step, 0)
    plane_chunk(nch - 1)

    def count(cmp, t):
        def body(c, acc):
            hit = jnp.where(cmp(sc_ref[c], t), 1.0, 0.0)
            return acc + jnp.sum(hit.reshape(KEY_CHUNK // COUNT_ROWS, COUNT_ROWS, Q_BLOCK), axis=0)

        acc = lax.fori_loop(0, nch, body, jnp.zeros((COUNT_ROWS, Q_BLOCK), F32))
        return jnp.sum(acc, axis=0, keepdims=True)

    ge = lambda a, b: a >= b
    gt = lambda a, b: a > b
    few = qpos < top_k
    plane_row = lax.broadcasted_iota(I32, (planes_ref.shape[1], Q_BLOCK), 0)
    t = _key_to_float(_radix_select(planes_ref, jnp.where(plane_row < nch * PLANE_ROWS, -1, 0), top_k))
    n_gt = count(gt, t)
    n_ge = count(ge, t)
    is_kth = few | ((n_gt < float(top_k)) & (n_ge >= float(top_k)))

    def recount():
        t2 = _kth_largest(functools.partial(count, ge), top_k, (1, Q_BLOCK))
        return t2, count(gt, t2), count(ge, t2)

    t, n_gt, n_ge = lax.cond(jnp.min(jnp.where(is_kth, 1.0, 0.0)) > 0.0, lambda: (t, n_gt, n_ge), recount)
    t = jnp.where(few, -jnp.inf, t)
    need = jnp.where(few, 0.0, float(top_k) - n_gt)

    m_ref[...] = jnp.full(m_ref.shape, NEG_INF, F32)
    acc_ref[...] = jnp.zeros(acc_ref.shape, F32)
    ones_rows = jnp.ones((MXU_ROWS, KEY_CHUNK), BF16)

    def weigh_chunk(c, n_eq):
        off = pl.multiple_of(c * KEY_CHUNK, KEY_CHUNK)
        blk = sc_ref[c]
        eq = blk == t
        eqf = jnp.where(eq, 1.0, 0.0)
        before = n_eq + _dot(low_ref[...], eqf.astype(BF16))
        sel = (blk > t) | (eq & (before < need))
        bias = jnp.where(sel, 0.0, NEG_INF)
        bias = jnp.concatenate([bias] * GROUP, axis=1)
        alphas = []
        for g in range(N_KV_HEADS):
            kc = k_ref[pl.ds(off, KEY_CHUNK), g * HEAD_DIM:(g + 1) * HEAD_DIM]
            s = _dot(kc, qt_ref[0, g]) + bias
            m_old = m_ref[g]
            m_new = jnp.maximum(m_old, jnp.max(s, axis=0, keepdims=True))
            alphas.append(jnp.exp(m_old - m_new))
            p_ref[g] = jnp.exp(s - m_new).astype(BF16)
            m_ref[g] = m_new
        return n_eq + jnp.sum(eqf, axis=0, keepdims=True), tuple(alphas)

    def accumulate_chunk(c, alphas):
        for g in range(N_KV_HEADS):
            vext = jnp.concatenate([vt_ref[c, g * HEAD_DIM:(g + 1) * HEAD_DIM, :], ones_rows], axis=0)
            acc_ref[g] = alphas[g] * acc_ref[g] + _dot(vext, p_ref[g])

    def attn_step(c, carry):
        n_eq, alphas = carry
        accumulate_chunk(c - 1, alphas)
        return weigh_chunk(c, n_eq)

    first = weigh_chunk(0, jnp.zeros((1, Q_BLOCK), F32))
    _, alphas = lax.fori_loop(1, nch, attn_step, first)
    accumulate_chunk(nch - 1, alphas)

    for g in range(N_KV_HEADS):
        acc = acc_ref[g]
        o = acc[:HEAD_DIM] / acc[HEAD_DIM:HEAD_DIM + 1]
        for hh in range(GROUP):
            col = (g * GROUP + hh) * HEAD_DIM
            o_ref[:, col:col + HEAD_DIM] = o[:, hh * Q_BLOCK:(hh + 1) * Q_BLOCK].T.astype(o_ref.dtype)


def _prompt_attn(qt, iqt, iwt, kb, vt, ikb, *, batch, seq):
    nqb = seq // Q_BLOCK
    nch = seq // KEY_CHUNK
    kvw = N_KV_HEADS * HEAD_DIM
    top_k = min(TOP_K_MAX, seq // 4)
    low = jnp.tril(jnp.ones((KEY_CHUNK, KEY_CHUNK), BF16), k=-1)
    per_qblock = lambda *shape: pl.BlockSpec((1,) + shape, lambda b, i: (b * nqb + i,) + (0,) * len(shape))
    per_batch = lambda width: pl.BlockSpec((seq, width), lambda b, i: (b, 0))
    return pl.pallas_call(
        functools.partial(_prompt_attn_kernel, top_k),
        grid=(batch, nqb),
        in_specs=[per_qblock(N_KV_HEADS, HEAD_DIM, GROUP * Q_BLOCK),
                  per_qblock(IDX_HEADS * IDX_DIM, Q_BLOCK), per_qblock(SUBLANES, Q_BLOCK),
                  per_batch(kvw), pl.BlockSpec((nch, kvw, KEY_CHUNK), lambda b, i: (b, 0, 0)),
                  per_batch(IDX_DIM), _const_spec((KEY_CHUNK, KEY_CHUNK))],
        out_specs=pl.BlockSpec((Q_BLOCK, N_HEADS * HEAD_DIM), lambda b, i: (b * nqb + i, 0)),
        out_shape=jax.ShapeDtypeStruct((batch * seq, N_HEADS * HEAD_DIM), BF16),
        scratch_shapes=[
            pltpu.VMEM((nch, KEY_CHUNK, Q_BLOCK), F32),
            pltpu.VMEM((32, nch * PLANE_ROWS, Q_BLOCK), I32),
            pltpu.VMEM((N_KV_HEADS, 1, GROUP * Q_BLOCK), F32),
            pltpu.VMEM((N_KV_HEADS, HEAD_DIM + MXU_ROWS, GROUP * Q_BLOCK), F32),
            pltpu.VMEM((N_KV_HEADS, KEY_CHUNK, GROUP * Q_BLOCK), BF16),
        ],
        compiler_params=pltpu.CompilerParams(
            dimension_semantics=("arbitrary", "arbitrary"), vmem_limit_bytes=VMEM_LIMIT_BYTES),
        name="prompt_attn",
    )(qt, iqt, iwt, kb, vt, ikb, low)


def _sample_attn_kernel(n_pages, page, top_k, rb, pt_ref, q_ref, iq_ref, iw_ref, kn_ref, vn_ref, ikn_ref,
                        ck_hbm, cv_hbm, cik_hbm, tri_ref, low_ref, o_ref,
                        kbuf, vbuf, ikbuf, sems, sc_ref):
    b = pl.program_id(0)
    nb = pl.num_programs(0)
    past = n_pages * page

    def page_copies(step, slot):
        copies = []
        for r in range(rb):
            for p in range(n_pages):
                phys = pt_ref[step * rb + r, p]
                rows = pl.ds(p * page * N_KV_HEADS, page * N_KV_HEADS)
                lanes = pl.ds(p * page, page)
                copies.append(pltpu.make_async_copy(ck_hbm.at[phys], kbuf.at[slot, r, rows], sems.at[0, slot]))
                copies.append(pltpu.make_async_copy(cv_hbm.at[phys], vbuf.at[slot, r, rows], sems.at[1, slot]))
                copies.append(
                    pltpu.make_async_copy(cik_hbm.at[phys], ikbuf.at[slot, r, :, lanes], sems.at[2, slot]))
        return copies

    slot = b % 2

    @pl.when(b == 0)
    def _():
        for c in page_copies(0, 0):
            c.start()

    @pl.when(b + 1 < nb)
    def _():
        for c in page_copies(b + 1, 1 - slot):
            c.start()

    for c in page_copies(b, slot):
        c.wait()

    sc_rows, sc_news = [], []
    for r in range(rb):
        iq = iq_ref[r].astype(BF16)
        iw = iw_ref[r]
        sidx = _dot(iq, ikbuf[slot, r].astype(BF16))
        sc_row = jnp.sum(jnp.maximum(sidx, 0.0) * iw, axis=0, keepdims=True)
        for p in range(n_pages):
            sc_ref[r, p:p + 1, :] = sc_row[:, p * page:(p + 1) * page]
        ikn = ikn_ref[r].astype(BF16).astype(F32)
        s_new = jnp.sum(iq.astype(F32) * ikn, axis=1, keepdims=True)
        sc_rows.append(sc_row)
        sc_news.append(jnp.sum(jnp.maximum(s_new, 0.0) * iw, axis=0, keepdims=True))

    def count_ge(r, cand):
        hits = jnp.where(sc_rows[r] >= cand, 1.0, 0.0)
        return jnp.sum(hits, axis=1, keepdims=True) + jnp.where(sc_news[r] >= cand, 1.0, 0.0)

    ts = _kth_largest_by_digits([functools.partial(count_ge, r) for r in range(rb)], top_k,
                                SAMPLE_SEARCH_BITS)

    def total(x):
        return jnp.sum(jnp.sum(x, axis=1, keepdims=True), axis=0, keepdims=True)

    for r in range(rb):
        t, sc, sc_new = ts[r], sc_ref[r], sc_news[r]
        n_gt = total(jnp.where(sc > t, 1.0, 0.0)) + jnp.where(sc_new > t, 1.0, 0.0)
        need = float(top_k) - n_gt
        eq = sc == t
        eqf = jnp.where(eq, 1.0, 0.0)
        in_row = _dot(eqf.astype(BF16), tri_ref[...])
        row_tot = jnp.broadcast_to(jnp.sum(eqf, axis=1, keepdims=True), sc.shape)
        rows_before = _dot(low_ref[...], row_tot.astype(BF16))
        sel = (sc > t) | (eq & (in_row + rows_before < need))
        sel_new = (sc_new > t) | ((sc_new == t) & (total(eqf) < need))
        bias = jnp.where(sel, 0.0, NEG_INF)
        bias_row = jnp.concatenate([bias[p:p + 1, :] for p in range(n_pages)], axis=1)
        bias_new = jnp.where(sel_new, 0.0, NEG_INF)

        q = q_ref[r].astype(BF16)
        qf = q.astype(F32)
        for g in range(N_KV_HEADS):
            head_rows = pl.ds(g, past, stride=N_KV_HEADS)
            kg = kbuf[slot, r, head_rows, :].astype(BF16)
            vg = vbuf[slot, r, head_rows, :].astype(BF16)
            kn = kn_ref[r, g:g + 1, :].astype(BF16).astype(F32)
            vn = vn_ref[r, g:g + 1, :].astype(BF16).astype(F32)
            s = _dot_nt(q, kg) + bias_row
            sn = jnp.sum(qf * kn, axis=1, keepdims=True) + bias_new
            m = jnp.maximum(jnp.max(s, axis=1, keepdims=True), sn)
            p = jnp.exp(s - m)
            pn = jnp.exp(sn - m)
            l = jnp.sum(p, axis=1, keepdims=True) + pn
            o = (_dot(p.astype(BF16), vg) + pn * vn) / l
            o_ref[r, g * GROUP:(g + 1) * GROUP, :] = o[g * GROUP:(g + 1) * GROUP]


def _sample_attn(page_table, q, iq, iw, k_new, v_new, ik_new, cache_k, cache_v, cache_ik):
    nreq, n_pages = page_table.shape
    n_phys, page = cache_k.shape[0], cache_k.shape[1]
    kvw = N_KV_HEADS * HEAD_DIM
    top_k = min(TOP_K_MAX, (n_pages * page + 1) // 4)
    assert top_k < n_pages * page + 1
    ck = cache_k.reshape(n_phys, page * N_KV_HEADS, HEAD_DIM)
    cv = cache_v.reshape(n_phys, page * N_KV_HEADS, HEAD_DIM)
    cik = jnp.swapaxes(cache_ik, 1, 2)
    tri = jnp.triu(jnp.ones((page, page), BF16), k=1)
    low = jnp.tril(jnp.ones((n_pages, n_pages), BF16), k=-1)
    rb = SAMPLE_REQS
    assert nreq % rb == 0
    pad_rows = lambda a: jnp.pad(a.astype(F32), ((0, 0), (0, MXU_ROWS - a.shape[1]), (0, 0)))
    per_req = lambda *shape: pl.BlockSpec((rb,) + shape, lambda b, pt: (b,) + (0,) * len(shape))
    const = lambda *shape: pl.BlockSpec(shape, lambda b, pt: (0,) * len(shape))
    any_spec = pl.BlockSpec(memory_space=pl.ANY)
    grid_spec = pltpu.PrefetchScalarGridSpec(
        num_scalar_prefetch=1,
        grid=(nreq // rb,),
        in_specs=[per_req(MXU_ROWS, HEAD_DIM), per_req(MXU_ROWS, IDX_DIM), per_req(MXU_ROWS, 1),
                  per_req(N_KV_HEADS, HEAD_DIM), per_req(N_KV_HEADS, HEAD_DIM), per_req(1, IDX_DIM),
                  any_spec, any_spec, any_spec, const(page, page), const(n_pages, n_pages)],
        out_specs=per_req(N_HEADS, HEAD_DIM),
        scratch_shapes=[
            pltpu.VMEM((2, rb, n_pages * page * N_KV_HEADS, HEAD_DIM), F32),
            pltpu.VMEM((2, rb, n_pages * page * N_KV_HEADS, HEAD_DIM), F32),
            pltpu.VMEM((2, rb, IDX_DIM, n_pages * page), F32),
            pltpu.SemaphoreType.DMA((3, 2)),
            pltpu.VMEM((rb, n_pages, page), F32),
        ],
    )
    out = pl.pallas_call(
        functools.partial(_sample_attn_kernel, n_pages, page, top_k, rb),
        grid_spec=grid_spec,
        out_shape=jax.ShapeDtypeStruct((nreq, N_HEADS, HEAD_DIM), F32),
        compiler_params=pltpu.CompilerParams(
            dimension_semantics=("arbitrary",), vmem_limit_bytes=VMEM_LIMIT_BYTES),
        name="sample_attn",
    )(page_table,
      pad_rows(q.reshape(nreq, N_HEADS, HEAD_DIM)),
      pad_rows(jnp.transpose(iq, (1, 0, 2))),
      pad_rows(iw.reshape(nreq, IDX_HEADS, 1)),
      k_new.reshape(nreq, N_KV_HEADS, HEAD_DIM), v_new.reshape(nreq, N_KV_HEADS, HEAD_DIM),
      ik_new.reshape(nreq, 1, IDX_DIM),
      ck, cv, cik, tri, low)
    return out.reshape(nreq, N_HEADS * HEAD_DIM)


def _merge_stage(x, attn, oa, sgb, wao_ref, wo_ref, gmem_ref, wmq_ref):
    out_b = _dot(attn, wao_ref[...])
    merged = oa.astype(F32) + sgb.astype(F32) * out_b
    x1 = x + _dot(merged.astype(BF16), wo_ref[...])
    hm = _rms(x1, gmem_ref[...]).astype(BF16)
    return x1, _dot(hm, wmq_ref[...]) * MEM_SCALE


def _ffn_stage(x1, mem_o, wmo_ref, gffn_ref, wg_ref, wu_ref, wd_ref, gfin_ref):
    x2 = x1 + _dot(mem_o.astype(BF16), wmo_ref[...])
    hf = _rms(x2, gffn_ref[...]).astype(BF16)
    acc = jnp.zeros_like(x2)
    for c in range(D_FF // FF_CHUNK):
        cols = slice(c * FF_CHUNK, (c + 1) * FF_CHUNK)
        gate = _dot(hf, wg_ref[:, cols])
        f = gate * _sigmoid(gate) * _dot(hf, wu_ref[:, cols])
        acc = acc + _dot(f.astype(BF16), wd_ref[cols, :])
    return _rms(x2 + acc, gfin_ref[...])


def _post_prompt_kernel(x_ref, attn_ref, oa_ref, sgb_ref, mk_ref, mv_ref, wao_ref, wo_ref, gmem_ref,
                        wmq_ref, wmo_ref, gffn_ref, wg_ref, wu_ref, wd_ref, gfin_ref, y_ref, mo_ref):
    x1, qm = _merge_stage(x_ref[...], attn_ref[...], oa_ref[...], sgb_ref[...],
                          wao_ref, wo_ref, gmem_ref, wmq_ref)
    for h in range(MEM_HEADS):
        cols = slice(h * MEM_HEAD_DIM, (h + 1) * MEM_HEAD_DIM)
        s = _dot_nt(qm[:, cols].astype(BF16), mk_ref[:, cols])
        p = jnp.exp(s - jnp.max(s, axis=1, keepdims=True))
        o = _dot(p.astype(BF16), mv_ref[:, cols]) / jnp.sum(p, axis=1, keepdims=True)
        mo_ref[:, cols] = o.astype(BF16)
    y_ref[...] = _ffn_stage(x1, mo_ref[...], wmo_ref, gffn_ref, wg_ref, wu_ref, wd_ref, gfin_ref)


def _post_prompt(x, attn, oa, sgb, mk, mv, w, *, batch, seq, tm):
    nt = seq // tm
    n_mem = mk.shape[0] // batch
    tok = lambda width: pl.BlockSpec((tm, width), lambda b, t: (b * nt + t, 0))
    mem = pl.BlockSpec((n_mem, D_MODEL), lambda b, t: (b, 0))
    sq = _const_spec((D_MODEL, D_MODEL))
    gain = _const_spec((1, D_MODEL))
    return pl.pallas_call(
        _post_prompt_kernel,
        grid=(batch, nt),
        in_specs=[tok(D_MODEL), tok(D_MODEL), tok(D_MODEL), tok(D_MODEL), mem, mem,
                  sq, sq, gain, sq, sq, gain,
                  _const_spec((D_MODEL, D_FF)), _const_spec((D_MODEL, D_FF)), _const_spec((D_FF, D_MODEL)),
                  gain],
        out_specs=tok(D_MODEL),
        out_shape=jax.ShapeDtypeStruct((batch * seq, D_MODEL), F32),
        scratch_shapes=[pltpu.VMEM((tm, D_MODEL), BF16)],
        compiler_params=pltpu.CompilerParams(
            dimension_semantics=("arbitrary", "arbitrary"), vmem_limit_bytes=VMEM_LIMIT_BYTES),
        name="post_prompt",
    )(x, attn, oa, sgb, mk, mv, w["wao"], w["wo"], w["gmem"], w["wmq"], w["wmo"], w["gffn"],
      w["wg"], w["wu"], w["wd"], w["gfin"])


def _merge_sample_kernel(x_ref, attn_ref, oa_ref, sgb_ref, wao_ref, wo_ref, gmem_ref, wmq_ref,
                         x1_ref, qm_ref):
    x1, qm = _merge_stage(x_ref[...], attn_ref[...].astype(BF16), oa_ref[...], sgb_ref[...],
                          wao_ref, wo_ref, gmem_ref, wmq_ref)
    x1_ref[...] = x1
    qm_ref[...] = qm


def _mem_sample_kernel(rb, q_ref, mk_ref, mv_ref, o_ref):
    for r in range(rb):
        s = jnp.sum(mk_ref[r] * q_ref[r][None], axis=2, keepdims=True)
        p = jnp.exp(s - jnp.max(s, axis=0, keepdims=True))
        o_ref[r] = jnp.sum(p * mv_ref[r], axis=0) / jnp.sum(p, axis=0)


def _ffn_sample_kernel(x1_ref, mo_ref, wmo_ref, gffn_ref, wg_ref, wu_ref, wd_ref, gfin_ref, y_ref):
    y_ref[...] = _ffn_stage(x1_ref[...], mo_ref[...], wmo_ref, gffn_ref, wg_ref, wu_ref, wd_ref, gfin_ref)


def _post_sample(x, attn, oa, sgb, mem_k, mem_v, w):
    n = x.shape[0]
    n_mem = mem_k.shape[1]
    full = lambda *shape: _const_spec(shape)
    act = full(n, D_MODEL)
    sq = full(D_MODEL, D_MODEL)
    gain = full(1, D_MODEL)
    params = pltpu.CompilerParams(dimension_semantics=("arbitrary",), vmem_limit_bytes=VMEM_LIMIT_BYTES)
    x1, qm = pl.pallas_call(
        _merge_sample_kernel, grid=(1,),
        in_specs=[act, act, act, act, sq, sq, gain, sq], out_specs=[act, act],
        out_shape=[jax.ShapeDtypeStruct((n, D_MODEL), F32)] * 2,
        compiler_params=params, name="merge_sample",
    )(x, attn, oa, sgb, w["wao"], w["wo"], w["gmem"], w["wmq"])

    rb = 4
    heads = pl.BlockSpec((rb, MEM_HEADS, MEM_HEAD_DIM), lambda i: (i, 0, 0))
    mem = pl.BlockSpec((rb, n_mem, MEM_HEADS, MEM_HEAD_DIM), lambda i: (i, 0, 0, 0))
    mem_o = pl.pallas_call(
        functools.partial(_mem_sample_kernel, rb), grid=(n // rb,),
        in_specs=[heads, mem, mem], out_specs=heads,
        out_shape=jax.ShapeDtypeStruct((n, MEM_HEADS, MEM_HEAD_DIM), F32),
        compiler_params=params, name="mem_sample",
    )(qm.reshape(n, MEM_HEADS, MEM_HEAD_DIM), mem_k, mem_v)

    return pl.pallas_call(
        _ffn_sample_kernel, grid=(1,),
        in_specs=[act, act, sq, gain, full(D_MODEL, D_FF), full(D_MODEL, D_FF), full(D_FF, D_MODEL), gain],
        out_specs=act, out_shape=jax.ShapeDtypeStruct((n, D_MODEL), F32),
        compiler_params=params, name="ffn_sample",
    )(x1, mem_o.reshape(n, D_MODEL), w["wmo"], w["gffn"], w["wg"], w["wu"], w["wd"], w["gfin"])


def _memory_kv_kernel(mem_ref, g_ref, wk_ref, wv_ref, kf_ref, vf_ref, kb_ref, vb_ref):
    m = _rms(mem_ref[...], g_ref[...]).astype(BF16)
    k = _dot(m, wk_ref[...])
    v = _dot(m, wv_ref[...])
    kf_ref[...] = k
    vf_ref[...] = v
    kb_ref[...] = k.astype(BF16)
    vb_ref[...] = v.astype(BF16)


def _memory_kv(mem, g, wk, wv):
    n = mem.shape[0]
    full = lambda *shape: _const_spec(shape)
    act = full(n, D_MODEL)
    sq = full(D_MODEL, D_MODEL)
    return pl.pallas_call(
        _memory_kv_kernel, grid=(1,),
        in_specs=[act, full(1, D_MODEL), sq, sq], out_specs=[act] * 4,
        out_shape=[jax.ShapeDtypeStruct((n, D_MODEL), F32)] * 2 + [jax.ShapeDtypeStruct((n, D_MODEL), BF16)] * 2,
        compiler_params=pltpu.CompilerParams(
            dimension_semantics=("arbitrary",), vmem_limit_bytes=VMEM_LIMIT_BYTES),
        name="memory_kv",
    )(mem, g, wk, wv)


def _prep_w_in(w_in):
    w_t = w_in.T
    main = jnp.pad(w_t[:IDX_RAW_END].astype(BF16), ((0, W_COLS - IDX_RAW_END), (0, 0)))
    return main, w_t[IDX_RAW_END:].astype(BF16)


def kernel(x_prompt, x_sample, mem_prompt, cache_k, cache_v, cache_idx_k, cache_mem_k, cache_mem_v, state_conv, page_table, g_mix, w_in, conv_w, w_conv_out, w_attn_out, w_o, g_mem, g_mem_kv, w_mq, w_mk, w_mv, w_mo, g_ffn, w_gate, w_up, w_down, g_final):
    depth = w_in.shape[0]
    assert depth == 1, "single-layer step"
    batch, seq, _ = x_prompt.shape
    nreq, dec_seq, _ = x_sample.shape
    assert dec_seq == 1
    n_mem = mem_prompt.shape[1]
    l = 0
    bf = lambda a: a.astype(BF16)
    row = lambda a: a.reshape(1, -1)

    w_main, w_gates = _prep_w_in(w_in[l])
    wco = bf(w_conv_out[l])
    w = dict(wao=bf(w_attn_out[l]), wo=bf(w_o[l]), gmem=row(g_mem[l]), wmq=bf(w_mq[l]), wmo=bf(w_mo[l]),
             gffn=row(g_ffn[l]), wg=bf(w_gate[l]), wu=bf(w_up[l]), wd=bf(w_down[l]), gfin=row(g_final))

    xp = x_prompt.reshape(batch * seq, D_MODEL)
    mkf, mvf, mkb, mvb = _memory_kv(mem_prompt.reshape(batch * n_mem, D_MODEL), row(g_mem_kv[l]),
                                    bf(w_mk[l]), bf(w_mv[l]))
    (kf, vf, ikf, oa, sgb, qt, kb, vt, iqt, ikb, iwt, conv_p) = _mixer_in(
        xp, row(g_mix[l]), w_main, w_gates, conv_w[l], wco, batch=batch, seq=seq, tm=KEY_CHUNK)
    attn = _prompt_attn(qt, iqt, iwt, kb, vt, ikb, batch=batch, seq=seq)
    yp = _post_prompt(xp, attn, oa, sgb, mkb, mvb, w, batch=batch, seq=seq, tm=256)

    xs = x_sample.reshape(nreq, D_MODEL)
    st = state_conv[l]
    (kf_s, vf_s, ikf_s, oa_s, sgb_s, q_s, iq_s, iw_s, u_s) = _mixer_in(
        xs, row(g_mix[l]), w_main, w_gates, conv_w[l], wco, batch=nreq, seq=1, tm=nreq, state=(st[:, 0], st[:, 1]))
    attn_s = _sample_attn(page_table, q_s, iq_s, iw_s, kf_s, vf_s, ikf_s,
                          cache_k[l], cache_v[l], cache_idx_k[l])
    ys = _post_sample(xs, attn_s, oa_s, sgb_s, cache_mem_k[l], cache_mem_v[l], w)

    return (
        yp.reshape(batch, seq, D_MODEL),
        ys.reshape(nreq, 1, D_MODEL),
        kf.reshape(1, batch, seq, N_KV_HEADS, HEAD_DIM),
        vf.reshape(1, batch, seq, N_KV_HEADS, HEAD_DIM),
        ikf.reshape(1, batch, seq, IDX_DIM),
        conv_p.reshape(1, batch, CONV_WIDTH - 1, D_CONV),
        mkf.reshape(1, batch, n_mem, MEM_HEADS, MEM_HEAD_DIM),
        mvf.reshape(1, batch, n_mem, MEM_HEADS, MEM_HEAD_DIM),
        kf_s.reshape(1, nreq, 1, N_KV_HEADS, HEAD_DIM),
        vf_s.reshape(1, nreq, 1, N_KV_HEADS, HEAD_DIM),
        ikf_s.reshape(1, nreq, 1, IDX_DIM),
        jnp.stack([st[:, 1], u_s], axis=1).reshape(1, nreq, CONV_WIDTH - 1, D_CONV),
    )
```

```python
import functools

import jax
import jax.numpy as jnp
from jax import lax
from jax.experimental import pallas as pl
from jax.experimental.pallas import tpu as pltpu

F32 = jnp.float32
BF16 = jnp.bfloat16
I32 = jnp.int32

D_MODEL = 1024
D_CONV = 1024
CONV_WIDTH = 3
N_HEADS = 8
HEAD_DIM = 128
N_KV_HEADS = 2
GROUP = N_HEADS // N_KV_HEADS
IDX_HEADS = 4
IDX_DIM = 64
TOP_K_MAX = 256
MEM_HEADS = 4
MEM_HEAD_DIM = D_MODEL // MEM_HEADS
D_FF = 2816
EPS = 1e-6
NEG_INF = -1e30
IDX_SCALE = (IDX_HEADS * IDX_DIM) ** -0.5
ATTN_SCALE = HEAD_DIM ** -0.5
MEM_SCALE = MEM_HEAD_DIM ** -0.5
LOG2_E = 1.4426950408889634

LANES = 128
SUBLANES = 8
MXU_ROWS = 16
VMEM_LIMIT_BYTES = 56 * 1024 * 1024

COL_CIN = 0
COL_CB = COL_CIN + D_CONV
COL_CC = COL_CB + D_CONV
COL_Q = COL_CC + D_CONV
COL_K = COL_Q + N_HEADS * HEAD_DIM
COL_V = COL_K + N_KV_HEADS * HEAD_DIM
COL_IQ = COL_V + N_KV_HEADS * HEAD_DIM
COL_IK = COL_IQ + IDX_HEADS * IDX_DIM
COL_IW = COL_IK + IDX_DIM
IDX_RAW_END = COL_IW + IDX_HEADS
W_COLS = -(-IDX_RAW_END // LANES) * LANES

Q_BLOCK = 256
KEY_CHUNK = 512
COUNT_ROWS = 64
PLANE_ROWS = KEY_CHUNK // 32
SAMPLE_SEARCH_BITS = 4
SAMPLE_REQS = 4
FF_CHUNK = D_FF // 2
POST_TILE = 512
INT_MIN = -2 ** 31


def _rms(x, g):
    return x * lax.rsqrt(jnp.mean(x * x, axis=-1, keepdims=True) + EPS) * g


def _dot(a, b):
    return jnp.dot(a, b, preferred_element_type=F32)


def _dot_nt(a, b):
    return lax.dot_general(a, b, (((1,), (1,)), ((), ())), preferred_element_type=F32)


def _sigmoid(x):
    return 1.0 / (1.0 + jnp.exp(-x))


def _key_to_float(ukey):
    skey = ukey ^ INT_MIN
    bits = jnp.where(skey < 0, skey ^ 0x7FFFFFFF, skey)
    return lax.bitcast_convert_type(bits, F32)


def _float_to_key(x):
    bits = lax.bitcast_convert_type(x, I32)
    return bits ^ (lax.shift_right_arithmetic(bits, jnp.full_like(bits, 31)) & 0x7FFFFFFF) ^ INT_MIN


def _kth_largest(count_ge, top_k, shape):
    def body(b, ukey):
        cand = ukey | jnp.left_shift(jnp.int32(1), 31 - b)
        ok = count_ge(_key_to_float(cand)) >= float(top_k)
        return jnp.where(ok, cand, ukey)

    ukey = lax.fori_loop(0, 32, body, jnp.zeros(shape, I32))
    return _key_to_float(ukey)


def _kth_largest_by_digits(count_ge_fns, top_k, bits):
    digits = jnp.minimum(lax.broadcasted_iota(I32, (2 ** bits, 1), 0) + 1, 2 ** bits - 1)
    digits_f = digits.astype(F32)

    def body(rd, ukeys):
        shift = 32 - bits * (rd + 1)
        out = []
        for count_ge, ukey in zip(count_ge_fns, ukeys):
            cand = ukey | jnp.left_shift(digits, shift)
            ok = count_ge(_key_to_float(cand)) >= float(top_k)
            digit = jnp.max(jnp.where(ok, digits_f, 0.0), axis=0, keepdims=True).astype(I32)
            out.append(ukey | jnp.left_shift(digit, shift))
        return tuple(out)

    ukeys = lax.fori_loop(0, 32 // bits, body, tuple(jnp.zeros((1, 1), I32) for _ in count_ge_fns))
    return [_key_to_float(u) for u in ukeys]


def _bit_transpose32(words):
    a = list(words)
    j, m = 16, 0x0000FFFF
    while j:
        k = 0
        while k < 32:
            t = (a[k] ^ lax.shift_right_logical(a[k + j], jnp.full_like(a[k], j))) & m
            a[k] = a[k] ^ t
            a[k + j] = a[k + j] ^ lax.shift_left(t, jnp.full_like(t, j))
            k = (k + j + 1) & ~j
        j >>= 1
        m = (m ^ (m << j)) & 0xFFFFFFFF
    return a


def _radix_select(planes_ref, live, top_k):
    def body(i, carry):
        live, ukey, n_above = carry
        hit = live & planes_ref[i]
        cnt = jnp.sum(lax.population_count(hit).astype(F32), axis=0, keepdims=True)
        take = n_above + cnt >= float(top_k)
        live = jnp.where(take, hit, live ^ hit)
        ukey = jnp.where(take, ukey | jnp.left_shift(jnp.int32(1), 31 - i), ukey)
        n_above = jnp.where(take, n_above, n_above + cnt)
        return live, ukey, n_above

    lanes = live.shape[1]
    init = (live, jnp.zeros((1, lanes), I32), jnp.zeros((1, lanes), F32))
    return lax.fori_loop(0, 32, body, init)[1]


def _mixer_in_kernel(is_prompt, tm, *refs):
    if is_prompt:
        (x_ref, g_ref, w_ref, wgate_ref, cw_ref, wco_ref,
         kf_ref, vf_ref, ikf_ref, oa_ref, sgb_ref,
         qt_ref, kb_ref, vt_ref, iqt_ref, ikb_ref, iwt_ref, cs_ref, ubuf) = refs
    else:
        (x_ref, g_ref, w_ref, wgate_ref, cw_ref, wco_ref, s0_ref, s1_ref,
         kf_ref, vf_ref, ikf_ref, oa_ref, sgb_ref,
         q_ref, iq_ref, iw_ref, u_ref) = refs

    h = _rms(x_ref[...], g_ref[...]).astype(BF16)

    def proj(lo, hi):
        return _dot_nt(h, w_ref[lo:hi, :])

    u = proj(COL_CC, COL_CC + D_CONV) * proj(COL_CIN, COL_CIN + D_CONV)
    cw = cw_ref[...]
    if is_prompt:
        @pl.when(pl.program_id(1) == 0)
        def _():
            ubuf[0:SUBLANES, :] = jnp.zeros((SUBLANES, D_CONV), F32)

        ubuf[SUBLANES:SUBLANES + tm, :] = u
        conv = (ubuf[SUBLANES - 2:SUBLANES - 2 + tm, :] * cw[0:1]
                + ubuf[SUBLANES - 1:SUBLANES - 1 + tm, :] * cw[1:2] + u * cw[2:3])
        ubuf[0:SUBLANES, :] = ubuf[tm:tm + SUBLANES, :]
        cs_ref[0] = u[tm - (CONV_WIDTH - 1):, :]
    else:
        conv = s0_ref[...] * cw[0:1] + s1_ref[...] * cw[1:2] + u * cw[2:3]
        u_ref[...] = u

    a_in = (proj(COL_CB, COL_CB + D_CONV) * conv).astype(BF16)
    out_a = _dot(a_in, wco_ref[...])
    oa_ref[...] = (_sigmoid(_dot_nt(h, wgate_ref[:D_MODEL, :])) * out_a).astype(BF16)
    sgb_ref[...] = _sigmoid(_dot_nt(h, wgate_ref[D_MODEL:, :])).astype(BF16)

    q = proj(COL_Q, COL_K) * (ATTN_SCALE * LOG2_E if is_prompt else ATTN_SCALE)
    kv = proj(COL_K, COL_IQ)
    k = kv[:, :N_KV_HEADS * HEAD_DIM]
    v = kv[:, N_KV_HEADS * HEAD_DIM:]
    for g in range(N_KV_HEADS):
        head_rows = pl.ds(g, tm, stride=N_KV_HEADS)
        kf_ref[head_rows, :] = k[:, g * HEAD_DIM:(g + 1) * HEAD_DIM]
        vf_ref[head_rows, :] = v[:, g * HEAD_DIM:(g + 1) * HEAD_DIM]
    idx = proj(COL_IQ, W_COLS)
    iq = idx[:, :IDX_HEADS * IDX_DIM]
    ikw = idx[:, COL_IK - COL_IQ:]
    ik = ikw[:, :IDX_DIM]
    ikf_ref[...] = ik
    if is_prompt:
        kb_ref[...] = k.astype(BF16)
        ikb_ref[...] = ik.astype(BF16)
        vt_ref[0] = v.T.astype(BF16)
        for j in range(tm // Q_BLOCK):
            rows = slice(j * Q_BLOCK, (j + 1) * Q_BLOCK)
            for head in range(N_HEADS):
                g, hh = divmod(head, GROUP)
                qt_ref[j, g, :, hh * Q_BLOCK:(hh + 1) * Q_BLOCK] = (
                    q[rows, head * HEAD_DIM:(head + 1) * HEAD_DIM].T.astype(BF16))
            iqt_ref[j] = iq[rows].T.astype(BF16)
            iwt_ref[j] = ikw[rows].T[IDX_DIM:IDX_DIM + SUBLANES] * IDX_SCALE
    else:
        q_ref[...] = q.astype(BF16)
        for hd in range(IDX_HEADS):
            iq_ref[hd] = iq[:, hd * IDX_DIM:(hd + 1) * IDX_DIM].astype(BF16)
        iw_ref[...] = ikw[:, IDX_DIM:IDX_DIM + IDX_HEADS] * IDX_SCALE


def _const_spec(shape):
    nd = len(shape)
    return pl.BlockSpec(shape, lambda *_: (0,) * nd, pipeline_mode=pl.Buffered(1))


def _mixer_in(x, g_mix, w_main, w_gates, conv_w, w_conv_out, *, batch, seq, tm, state=None):
    is_prompt = state is None
    t_all = batch * seq
    nt = seq // tm if is_prompt else 1
    grid = (batch, nt) if is_prompt else (1, 1)
    tok = lambda width: pl.BlockSpec((tm, width), lambda b, t: (b * nt + t, 0))
    in_specs = [tok(D_MODEL), _const_spec((1, D_MODEL)), _const_spec((W_COLS, D_MODEL)), _const_spec((2 * D_MODEL, D_MODEL)),
                _const_spec((CONV_WIDTH, D_CONV)), _const_spec((D_CONV, D_MODEL))]
    args = [x, g_mix, w_main, w_gates, conv_w, w_conv_out]
    kvw = N_KV_HEADS * HEAD_DIM
    out_shapes = [
        jax.ShapeDtypeStruct((t_all * N_KV_HEADS, HEAD_DIM), F32),
        jax.ShapeDtypeStruct((t_all * N_KV_HEADS, HEAD_DIM), F32),
        jax.ShapeDtypeStruct((t_all, IDX_DIM), F32),
        jax.ShapeDtypeStruct((t_all, D_MODEL), BF16),
        jax.ShapeDtypeStruct((t_all, D_MODEL), BF16),
    ]
    kv_rows = pl.BlockSpec((tm * N_KV_HEADS, HEAD_DIM), lambda b, t: (b * nt + t, 0))
    out_specs = [kv_rows, kv_rows] + [tok(s.shape[1]) for s in out_shapes[2:]]
    scratch = []
    if is_prompt:
        assert tm == KEY_CHUNK and tm % Q_BLOCK == 0
        qb = tm // Q_BLOCK
        nblk = t_all // Q_BLOCK
        per_qblock = lambda *shape: pl.BlockSpec((qb,) + shape, lambda b, t: (b * nt + t,) + (0,) * len(shape))
        out_shapes += [
            jax.ShapeDtypeStruct((nblk, N_KV_HEADS, HEAD_DIM, GROUP * Q_BLOCK), BF16),
            jax.ShapeDtypeStruct((t_all, kvw), BF16),
            jax.ShapeDtypeStruct((t_all // tm, kvw, tm), BF16),
            jax.ShapeDtypeStruct((nblk, IDX_HEADS * IDX_DIM, Q_BLOCK), BF16),
            jax.ShapeDtypeStruct((t_all, IDX_DIM), BF16),
            jax.ShapeDtypeStruct((nblk, SUBLANES, Q_BLOCK), F32),
            jax.ShapeDtypeStruct((batch, CONV_WIDTH - 1, D_CONV), F32),
        ]
        out_specs += [
            per_qblock(N_KV_HEADS, HEAD_DIM, GROUP * Q_BLOCK), tok(kvw),
            pl.BlockSpec((1, kvw, tm), lambda b, t: (b * nt + t, 0, 0)),
            per_qblock(IDX_HEADS * IDX_DIM, Q_BLOCK), tok(IDX_DIM), per_qblock(SUBLANES, Q_BLOCK),
            pl.BlockSpec((1, CONV_WIDTH - 1, D_CONV), lambda b, t: (b, 0, 0)),
        ]
        scratch.append(pltpu.VMEM((tm + SUBLANES, D_CONV), F32))
    else:
        in_specs += [tok(D_CONV), tok(D_CONV)]
        args += list(state)
        out_shapes += [
            jax.ShapeDtypeStruct((t_all, N_HEADS * HEAD_DIM), BF16),
            jax.ShapeDtypeStruct((IDX_HEADS, t_all, IDX_DIM), BF16),
            jax.ShapeDtypeStruct((t_all, IDX_HEADS), F32),
            jax.ShapeDtypeStruct((t_all, D_CONV), F32),
        ]
        out_specs += [tok(N_HEADS * HEAD_DIM),
                      pl.BlockSpec((IDX_HEADS, tm, IDX_DIM), lambda b, t: (0, b * nt + t, 0)),
                      tok(IDX_HEADS), tok(D_CONV)]
    return pl.pallas_call(
        functools.partial(_mixer_in_kernel, is_prompt, tm),
        grid=grid, in_specs=in_specs, out_specs=out_specs, out_shape=out_shapes,
        scratch_shapes=scratch,
        compiler_params=pltpu.CompilerParams(
            dimension_semantics=("arbitrary", "arbitrary"), vmem_limit_bytes=VMEM_LIMIT_BYTES),
        name="mixer_in_prompt" if is_prompt else "mixer_in_sample",
    )(*args)


def _prompt_attn_kernel(top_k, qt_ref, iqt_ref, iwt_ref, k_ref, vt_ref, ik_ref, low_ref, o_ref,
                        sc_ref, planes_ref, m_ref, acc_ref, p_ref):
    i = pl.program_id(1)
    nch = i // (KEY_CHUNK // Q_BLOCK) + 1
    qpos = i * Q_BLOCK + lax.broadcasted_iota(I32, (1, Q_BLOCK), 1)

    @pl.when(i == 0)
    def _():
        planes_ref[...] = jnp.zeros(planes_ref.shape, I32)

    iw = iwt_ref[0]
    iqt = iqt_ref[0]
    iq_pairs = [jnp.concatenate([iqt[h * IDX_DIM:(h + 1) * IDX_DIM] for h in (2 * pr, 2 * pr + 1)], axis=1)
                for pr in range(IDX_HEADS // 2)]

    def score_chunk(c):
        off = pl.multiple_of(c * KEY_CHUNK, KEY_CHUNK)
        ikc = ik_ref[pl.ds(off, KEY_CHUNK), :]
        acc = jnp.zeros((KEY_CHUNK, Q_BLOCK), F32)
        for pr in range(IDX_HEADS // 2):
            s2 = _dot(ikc, iq_pairs[pr])
            for e in range(2):
                h = 2 * pr + e
                acc = acc + jnp.maximum(s2[:, e * Q_BLOCK:(e + 1) * Q_BLOCK], 0.0) * iw[h:h + 1]
        kpos = off + lax.broadcasted_iota(I32, (KEY_CHUNK, Q_BLOCK), 0)
        sc_ref[c] = jnp.where(kpos <= qpos, acc, -jnp.inf)

    def plane_chunk(c):
        keys = _float_to_key(sc_ref[c])
        for wd in range(PLANE_ROWS // SUBLANES):
            words = [keys[(wd * 32 + j) * SUBLANES:(wd * 32 + j + 1) * SUBLANES] for j in range(32)]
            row = pl.multiple_of(c * PLANE_ROWS + wd * SUBLANES, SUBLANES)
            for b, plane in enumerate(_bit_transpose32(words)):
                planes_ref[b, pl.ds(row, SUBLANES), :] = plane

    score_chunk(0)

    def score_step(c, carry):
        plane_chunk(c - 1)
        score_chunk(c)
        return carry

    lax.fori_loop(1, nch, score_step, 0)
    plane_chunk(nch - 1)

    def count(cmps, t):
        def body(c, accs):
            blk = sc_ref[c]
            hits = [jnp.where(cmp(blk, t), 1.0, 0.0) for cmp in cmps]
            return tuple(acc + jnp.sum(hit.reshape(KEY_CHUNK // COUNT_ROWS, COUNT_ROWS, Q_BLOCK), axis=0)
                         for acc, hit in zip(accs, hits))

        accs = lax.fori_loop(0, nch, body, tuple(jnp.zeros((COUNT_ROWS, Q_BLOCK), F32) for _ in cmps))
        return tuple(jnp.sum(acc, axis=0, keepdims=True) for acc in accs)

    ge = lambda a, b: a >= b
    gt = lambda a, b: a > b
    few = qpos < top_k
    plane_row = lax.broadcasted_iota(I32, (planes_ref.shape[1], Q_BLOCK), 0)
    t = _key_to_float(_radix_select(planes_ref, jnp.where(plane_row < nch * PLANE_ROWS, -1, 0), top_k))
    n_gt, n_ge = count((gt,), t)[0], count((ge,), t)[0]
    is_kth = few | ((n_gt < float(top_k)) & (n_ge >= float(top_k)))

    def recount():
        t2 = _kth_largest(lambda cand: count((ge,), cand)[0], top_k, (1, Q_BLOCK))
        return t2, count((gt,), t2)[0]

    t, n_gt = lax.cond(jnp.min(jnp.where(is_kth, 1.0, 0.0)) > 0.0, lambda: (t, n_gt), recount)
    t = jnp.where(few, -jnp.inf, t)
    need = jnp.where(few, 0.0, float(top_k) - n_gt)

    m_ref[...] = jnp.full(m_ref.shape, NEG_INF, F32)
    acc_ref[...] = jnp.zeros(acc_ref.shape, F32)
    ones_rows = jnp.ones((MXU_ROWS, KEY_CHUNK), BF16)

    def weigh_chunk(c, n_eq):
        off = pl.multiple_of(c * KEY_CHUNK, KEY_CHUNK)
        blk = sc_ref[c]
        eq = blk == t
        eqf = jnp.where(eq, 1.0, 0.0)
        before = n_eq + _dot(low_ref[...], eqf.astype(BF16))
        sel = (blk > t) | (eq & (before < need))
        bias = jnp.where(sel, 0.0, NEG_INF)
        bias = jnp.concatenate([bias] * GROUP, axis=1)
        alphas = []
        for g in range(N_KV_HEADS):
            kc = k_ref[pl.ds(off, KEY_CHUNK), g * HEAD_DIM:(g + 1) * HEAD_DIM]
            s = _dot(kc, qt_ref[0, g]) + bias
            m_old = m_ref[g]
            m_new = jnp.maximum(m_old, jnp.max(s, axis=0, keepdims=True))
            alphas.append(jnp.exp2(m_old - m_new))
            p_ref[g] = jnp.exp2(s - m_new).astype(BF16)
            m_ref[g] = m_new
        return n_eq + jnp.sum(eqf, axis=0, keepdims=True), tuple(alphas)

    def accumulate_chunk(c, alphas):
        for g in range(N_KV_HEADS):
            vext = jnp.concatenate([vt_ref[c, g * HEAD_DIM:(g + 1) * HEAD_DIM, :], ones_rows], axis=0)
            acc_ref[g] = alphas[g] * acc_ref[g] + _dot(vext, p_ref[g])

    def attn_step(c, carry):
        n_eq, alphas = carry
        accumulate_chunk(c - 1, alphas)
        return weigh_chunk(c, n_eq)

    first = weigh_chunk(0, jnp.zeros((1, Q_BLOCK), F32))
    _, alphas = lax.fori_loop(1, nch, attn_step, first)
    accumulate_chunk(nch - 1, alphas)

    for g in range(N_KV_HEADS):
        acc = acc_ref[g]
        o = acc[:HEAD_DIM] / acc[HEAD_DIM:HEAD_DIM + 1]
        for hh in range(GROUP):
            col = (g * GROUP + hh) * HEAD_DIM
            o_ref[:, col:col + HEAD_DIM] = o[:, hh * Q_BLOCK:(hh + 1) * Q_BLOCK].T.astype(o_ref.dtype)


def _prompt_attn(qt, iqt, iwt, kb, vt, ikb, *, batch, seq):
    nqb = seq // Q_BLOCK
    nch = seq // KEY_CHUNK
    kvw = N_KV_HEADS * HEAD_DIM
    top_k = min(TOP_K_MAX, seq // 4)
    low = jnp.tril(jnp.ones((KEY_CHUNK, KEY_CHUNK), BF16), k=-1)
    per_qblock = lambda *shape: pl.BlockSpec((1,) + shape, lambda b, i: (b * nqb + i,) + (0,) * len(shape))
    per_batch = lambda width: pl.BlockSpec((seq, width), lambda b, i: (b, 0))
    return pl.pallas_call(
        functools.partial(_prompt_attn_kernel, top_k),
        grid=(batch, nqb),
        in_specs=[per_qblock(N_KV_HEADS, HEAD_DIM, GROUP * Q_BLOCK),
                  per_qblock(IDX_HEADS * IDX_DIM, Q_BLOCK), per_qblock(SUBLANES, Q_BLOCK),
                  per_batch(kvw), pl.BlockSpec((nch, kvw, KEY_CHUNK), lambda b, i: (b, 0, 0)),
                  per_batch(IDX_DIM), _const_spec((KEY_CHUNK, KEY_CHUNK))],
        out_specs=pl.BlockSpec((Q_BLOCK, N_HEADS * HEAD_DIM), lambda b, i: (b * nqb + i, 0)),
        out_shape=jax.ShapeDtypeStruct((batch * seq, N_HEADS * HEAD_DIM), BF16),
        scratch_shapes=[
            pltpu.VMEM((nch, KEY_CHUNK, Q_BLOCK), F32),
            pltpu.VMEM((32, nch * PLANE_ROWS, Q_BLOCK), I32),
            pltpu.VMEM((N_KV_HEADS, 1, GROUP * Q_BLOCK), F32),
            pltpu.VMEM((N_KV_HEADS, HEAD_DIM + MXU_ROWS, GROUP * Q_BLOCK), F32),
            pltpu.VMEM((N_KV_HEADS, KEY_CHUNK, GROUP * Q_BLOCK), BF16),
        ],
        compiler_params=pltpu.CompilerParams(
            dimension_semantics=("arbitrary", "arbitrary"), vmem_limit_bytes=VMEM_LIMIT_BYTES),
        name="prompt_attn",
    )(qt, iqt, iwt, kb, vt, ikb, low)


def _sample_attn_kernel(n_pages, page, top_k, rb, pt_ref, q_ref, iq_ref, iw_ref, kn_ref, vn_ref, ikn_ref,
                        ck_hbm, cv_hbm, cik_hbm, tri_ref, low_ref, o_ref,
                        kbuf, vbuf, ikbuf, sems, sc_ref):
    b = pl.program_id(0)
    nb = pl.num_programs(0)
    past = n_pages * page

    def page_copies(step, slot):
        copies = []
        for r in range(rb):
            for p in range(n_pages):
                phys = pt_ref[step * rb + r, p]
                rows = pl.ds(p * page * N_KV_HEADS, page * N_KV_HEADS)
                lanes = pl.ds(p * page, page)
                copies.append(pltpu.make_async_copy(ck_hbm.at[phys], kbuf.at[slot, r, rows], sems.at[0, slot]))
                copies.append(pltpu.make_async_copy(cv_hbm.at[phys], vbuf.at[slot, r, rows], sems.at[1, slot]))
                copies.append(
                    pltpu.make_async_copy(cik_hbm.at[phys], ikbuf.at[slot, r, :, lanes], sems.at[2, slot]))
        return copies

    slot = b % 2

    @pl.when(b == 0)
    def _():
        for c in page_copies(0, 0):
            c.start()

    @pl.when(b + 1 < nb)
    def _():
        for c in page_copies(b + 1, 1 - slot):
            c.start()

    for c in page_copies(b, slot):
        c.wait()

    sc_rows, sc_news = [], []
    for r in range(rb):
        iq = iq_ref[r].astype(BF16)
        iw = iw_ref[r]
        sidx = _dot(iq, ikbuf[slot, r].astype(BF16))
        sc_row = jnp.sum(jnp.maximum(sidx, 0.0) * iw, axis=0, keepdims=True)
        for p in range(n_pages):
            sc_ref[r, p:p + 1, :] = sc_row[:, p * page:(p + 1) * page]
        ikn = ikn_ref[r].astype(BF16).astype(F32)
        s_new = jnp.sum(iq.astype(F32) * ikn, axis=1, keepdims=True)
        sc_rows.append(sc_row)
        sc_news.append(jnp.sum(jnp.maximum(s_new, 0.0) * iw, axis=0, keepdims=True))

    def count_ge(r, cand):
        hits = jnp.where(sc_rows[r] >= cand, 1.0, 0.0)
        return jnp.sum(hits, axis=1, keepdims=True) + jnp.where(sc_news[r] >= cand, 1.0, 0.0)

    ts = _kth_largest_by_digits([functools.partial(count_ge, r) for r in range(rb)], top_k,
                                SAMPLE_SEARCH_BITS)

    def total(x):
        return jnp.sum(jnp.sum(x, axis=1, keepdims=True), axis=0, keepdims=True)

    for r in range(rb):
        t, sc, sc_new = ts[r], sc_ref[r], sc_news[r]
        n_gt = total(jnp.where(sc > t, 1.0, 0.0)) + jnp.where(sc_new > t, 1.0, 0.0)
        need = float(top_k) - n_gt
        eq = sc == t
        eqf = jnp.where(eq, 1.0, 0.0)
        in_row = _dot(eqf.astype(BF16), tri_ref[...])
        row_tot = jnp.broadcast_to(jnp.sum(eqf, axis=1, keepdims=True), sc.shape)
        rows_before = _dot(low_ref[...], row_tot.astype(BF16))
        sel = (sc > t) | (eq & (in_row + rows_before < need))
        sel_new = (sc_new > t) | ((sc_new == t) & (total(eqf) < need))
        bias = jnp.where(sel, 0.0, NEG_INF)
        bias_row = jnp.concatenate([bias[p:p + 1, :] for p in range(n_pages)], axis=1)
        bias_new = jnp.where(sel_new, 0.0, NEG_INF)

        q = q_ref[r].astype(BF16)
        qf = q.astype(F32)
        for g in range(N_KV_HEADS):
            head_rows = pl.ds(g, past, stride=N_KV_HEADS)
            kg = kbuf[slot, r, head_rows, :].astype(BF16)
            vg = vbuf[slot, r, head_rows, :].astype(BF16)
            kn = kn_ref[r, g:g + 1, :].astype(BF16).astype(F32)
            vn = vn_ref[r, g:g + 1, :].astype(BF16).astype(F32)
            s = _dot_nt(q, kg) + bias_row
            sn = jnp.sum(qf * kn, axis=1, keepdims=True) + bias_new
            m = jnp.maximum(jnp.max(s, axis=1, keepdims=True), sn)
            p = jnp.exp(s - m)
            pn = jnp.exp(sn - m)
            l = jnp.sum(p, axis=1, keepdims=True) + pn
            o = (_dot(p.astype(BF16), vg) + pn * vn) / l
            o_ref[r, g * GROUP:(g + 1) * GROUP, :] = o[g * GROUP:(g + 1) * GROUP]


def _sample_attn(page_table, q, iq, iw, k_new, v_new, ik_new, cache_k, cache_v, cache_ik):
    nreq, n_pages = page_table.shape
    n_phys, page = cache_k.shape[0], cache_k.shape[1]
    kvw = N_KV_HEADS * HEAD_DIM
    top_k = min(TOP_K_MAX, (n_pages * page + 1) // 4)
    assert top_k < n_pages * page + 1
    ck = cache_k.reshape(n_phys, page * N_KV_HEADS, HEAD_DIM)
    cv = cache_v.reshape(n_phys, page * N_KV_HEADS, HEAD_DIM)
    cik = jnp.swapaxes(cache_ik, 1, 2)
    tri = jnp.triu(jnp.ones((page, page), BF16), k=1)
    low = jnp.tril(jnp.ones((n_pages, n_pages), BF16), k=-1)
    rb = SAMPLE_REQS
    assert nreq % rb == 0
    pad_rows = lambda a: jnp.pad(a.astype(F32), ((0, 0), (0, MXU_ROWS - a.shape[1]), (0, 0)))
    per_req = lambda *shape: pl.BlockSpec((rb,) + shape, lambda b, pt: (b,) + (0,) * len(shape))
    const = lambda *shape: pl.BlockSpec(shape, lambda b, pt: (0,) * len(shape))
    any_spec = pl.BlockSpec(memory_space=pl.ANY)
    grid_spec = pltpu.PrefetchScalarGridSpec(
        num_scalar_prefetch=1,
        grid=(nreq // rb,),
        in_specs=[per_req(MXU_ROWS, HEAD_DIM), per_req(MXU_ROWS, IDX_DIM), per_req(MXU_ROWS, 1),
                  per_req(N_KV_HEADS, HEAD_DIM), per_req(N_KV_HEADS, HEAD_DIM), per_req(1, IDX_DIM),
                  any_spec, any_spec, any_spec, const(page, page), const(n_pages, n_pages)],
        out_specs=per_req(N_HEADS, HEAD_DIM),
        scratch_shapes=[
            pltpu.VMEM((2, rb, n_pages * page * N_KV_HEADS, HEAD_DIM), F32),
            pltpu.VMEM((2, rb, n_pages * page * N_KV_HEADS, HEAD_DIM), F32),
            pltpu.VMEM((2, rb, IDX_DIM, n_pages * page), F32),
            pltpu.SemaphoreType.DMA((3, 2)),
            pltpu.VMEM((rb, n_pages, page), F32),
        ],
    )
    out = pl.pallas_call(
        functools.partial(_sample_attn_kernel, n_pages, page, top_k, rb),
        grid_spec=grid_spec,
        out_shape=jax.ShapeDtypeStruct((nreq, N_HEADS, HEAD_DIM), F32),
        compiler_params=pltpu.CompilerParams(
            dimension_semantics=("arbitrary",), vmem_limit_bytes=VMEM_LIMIT_BYTES),
        name="sample_attn",
    )(page_table,
      pad_rows(q.reshape(nreq, N_HEADS, HEAD_DIM)),
      pad_rows(jnp.transpose(iq, (1, 0, 2))),
      pad_rows(iw.reshape(nreq, IDX_HEADS, 1)),
      k_new.reshape(nreq, N_KV_HEADS, HEAD_DIM), v_new.reshape(nreq, N_KV_HEADS, HEAD_DIM),
      ik_new.reshape(nreq, 1, IDX_DIM),
      ck, cv, cik, tri, low)
    return out.reshape(nreq, N_HEADS * HEAD_DIM)


def _merge_stage(x, attn, oa, sgb, wao_ref, wo_ref, gmem_ref, wmq_ref):
    out_b = _dot(attn, wao_ref[...])
    merged = oa.astype(F32) + sgb.astype(F32) * out_b
    x1 = x + _dot(merged.astype(BF16), wo_ref[...])
    hm = _rms(x1, gmem_ref[...]).astype(BF16)
    return x1, _dot(hm, wmq_ref[...]) * MEM_SCALE


def _ffn_stage(x1, mem_o, wmo_ref, gffn_ref, wg_ref, wu_ref, wd_ref, gfin_ref):
    x2 = x1 + _dot(mem_o.astype(BF16), wmo_ref[...])
    hf = _rms(x2, gffn_ref[...]).astype(BF16)
    acc = jnp.zeros_like(x2)
    for c in range(D_FF // FF_CHUNK):
        cols = slice(c * FF_CHUNK, (c + 1) * FF_CHUNK)
        gate = _dot(hf, wg_ref[:, cols])
        f = gate * _sigmoid(gate) * _dot(hf, wu_ref[:, cols])
        acc = acc + _dot(f.astype(BF16), wd_ref[cols, :])
    return _rms(x2 + acc, gfin_ref[...])


def _post_prompt_kernel(x_ref, attn_ref, oa_ref, sgb_ref, mk_ref, mv_ref, wao_ref, wo_ref, gmem_ref,
                        wmq_ref, wmo_ref, gffn_ref, wg_ref, wu_ref, wd_ref, gfin_ref, y_ref, mo_ref):
    x1, qm = _merge_stage(x_ref[...], attn_ref[...], oa_ref[...], sgb_ref[...],
                          wao_ref, wo_ref, gmem_ref, wmq_ref)
    for h in range(MEM_HEADS):
        cols = slice(h * MEM_HEAD_DIM, (h + 1) * MEM_HEAD_DIM)
        s = _dot_nt(qm[:, cols].astype(BF16), mk_ref[:, cols])
        p = jnp.exp(s - jnp.max(s, axis=1, keepdims=True))
        o = _dot(p.astype(BF16), mv_ref[:, cols]) / jnp.sum(p, axis=1, keepdims=True)
        mo_ref[:, cols] = o.astype(BF16)
    y_ref[...] = _ffn_stage(x1, mo_ref[...], wmo_ref, gffn_ref, wg_ref, wu_ref, wd_ref, gfin_ref)


def _post_prompt(x, attn, oa, sgb, mk, mv, w, *, batch, seq, tm):
    nt = seq // tm
    n_mem = mk.shape[0] // batch
    tok = lambda width: pl.BlockSpec((tm, width), lambda b, t: (b * nt + t, 0))
    mem = pl.BlockSpec((n_mem, D_MODEL), lambda b, t: (b, 0))
    sq = _const_spec((D_MODEL, D_MODEL))
    gain = _const_spec((1, D_MODEL))
    return pl.pallas_call(
        _post_prompt_kernel,
        grid=(batch, nt),
        in_specs=[tok(D_MODEL), tok(D_MODEL), tok(D_MODEL), tok(D_MODEL), mem, mem,
                  sq, sq, gain, sq, sq, gain,
                  _const_spec((D_MODEL, D_FF)), _const_spec((D_MODEL, D_FF)), _const_spec((D_FF, D_MODEL)),
                  gain],
        out_specs=tok(D_MODEL),
        out_shape=jax.ShapeDtypeStruct((batch * seq, D_MODEL), F32),
        scratch_shapes=[pltpu.VMEM((tm, D_MODEL), BF16)],
        compiler_params=pltpu.CompilerParams(
            dimension_semantics=("arbitrary", "arbitrary"), vmem_limit_bytes=VMEM_LIMIT_BYTES),
        name="post_prompt",
    )(x, attn, oa, sgb, mk, mv, w["wao"], w["wo"], w["gmem"], w["wmq"], w["wmo"], w["gffn"],
      w["wg"], w["wu"], w["wd"], w["gfin"])


def _merge_sample_kernel(x_ref, attn_ref, oa_ref, sgb_ref, wao_ref, wo_ref, gmem_ref, wmq_ref,
                         x1_ref, qm_ref):
    x1, qm = _merge_stage(x_ref[...], attn_ref[...].astype(BF16), oa_ref[...], sgb_ref[...],
                          wao_ref, wo_ref, gmem_ref, wmq_ref)
    x1_ref[...] = x1
    qm_ref[...] = qm


def _mem_sample_kernel(rb, q_ref, mk_ref, mv_ref, o_ref):
    for r in range(rb):
        s = jnp.sum(mk_ref[r] * q_ref[r][None], axis=2, keepdims=True)
        p = jnp.exp(s - jnp.max(s, axis=0, keepdims=True))
        o_ref[r] = jnp.sum(p * mv_ref[r], axis=0) / jnp.sum(p, axis=0)


def _ffn_sample_kernel(x1_ref, mo_ref, wmo_ref, gffn_ref, wg_ref, wu_ref, wd_ref, gfin_ref, y_ref):
    y_ref[...] = _ffn_stage(x1_ref[...], mo_ref[...], wmo_ref, gffn_ref, wg_ref, wu_ref, wd_ref, gfin_ref)


def _post_sample(x, attn, oa, sgb, mem_k, mem_v, w):
    n = x.shape[0]
    n_mem = mem_k.shape[1]
    full = lambda *shape: _const_spec(shape)
    act = full(n, D_MODEL)
    sq = full(D_MODEL, D_MODEL)
    gain = full(1, D_MODEL)
    params = pltpu.CompilerParams(dimension_semantics=("arbitrary",), vmem_limit_bytes=VMEM_LIMIT_BYTES)
    x1, qm = pl.pallas_call(
        _merge_sample_kernel, grid=(1,),
        in_specs=[act, act, act, act, sq, sq, gain, sq], out_specs=[act, act],
        out_shape=[jax.ShapeDtypeStruct((n, D_MODEL), F32)] * 2,
        compiler_params=params, name="merge_sample",
    )(x, attn, oa, sgb, w["wao"], w["wo"], w["gmem"], w["wmq"])

    rb = 4
    heads = pl.BlockSpec((rb, MEM_HEADS, MEM_HEAD_DIM), lambda i: (i, 0, 0))
    mem = pl.BlockSpec((rb, n_mem, MEM_HEADS, MEM_HEAD_DIM), lambda i: (i, 0, 0, 0))
    mem_o = pl.pallas_call(
        functools.partial(_mem_sample_kernel, rb), grid=(n // rb,),
        in_specs=[heads, mem, mem], out_specs=heads,
        out_shape=jax.ShapeDtypeStruct((n, MEM_HEADS, MEM_HEAD_DIM), F32),
        compiler_params=params, name="mem_sample",
    )(qm.reshape(n, MEM_HEADS, MEM_HEAD_DIM), mem_k, mem_v)

    return pl.pallas_call(
        _ffn_sample_kernel, grid=(1,),
        in_specs=[act, act, sq, gain, full(D_MODEL, D_FF), full(D_MODEL, D_FF), full(D_FF, D_MODEL), gain],
        out_specs=act, out_shape=jax.ShapeDtypeStruct((n, D_MODEL), F32),
        compiler_params=params, name="ffn_sample",
    )(x1, mem_o.reshape(n, D_MODEL), w["wmo"], w["gffn"], w["wg"], w["wu"], w["wd"], w["gfin"])


def _memory_kv_kernel(mem_ref, g_ref, wk_ref, wv_ref, kf_ref, vf_ref, kb_ref, vb_ref):
    m = _rms(mem_ref[...], g_ref[...]).astype(BF16)
    k = _dot(m, wk_ref[...])
    v = _dot(m, wv_ref[...])
    kf_ref[...] = k
    vf_ref[...] = v
    kb_ref[...] = k.astype(BF16)
    vb_ref[...] = v.astype(BF16)


def _memory_kv(mem, g, wk, wv):
    n = mem.shape[0]
    full = lambda *shape: _const_spec(shape)
    act = full(n, D_MODEL)
    sq = full(D_MODEL, D_MODEL)
    return pl.pallas_call(
        _memory_kv_kernel, grid=(1,),
        in_specs=[act, full(1, D_MODEL), sq, sq], out_specs=[act] * 4,
        out_shape=[jax.ShapeDtypeStruct((n, D_MODEL), F32)] * 2 + [jax.ShapeDtypeStruct((n, D_MODEL), BF16)] * 2,
        compiler_params=pltpu.CompilerParams(
            dimension_semantics=("arbitrary",), vmem_limit_bytes=VMEM_LIMIT_BYTES),
        name="memory_kv",
    )(mem, g, wk, wv)


def _prep_w_in(w_in):
    w_t = w_in.T
    main = jnp.pad(w_t[:IDX_RAW_END].astype(BF16), ((0, W_COLS - IDX_RAW_END), (0, 0)))
    return main, w_t[IDX_RAW_END:].astype(BF16)


def kernel(x_prompt, x_sample, mem_prompt, cache_k, cache_v, cache_idx_k, cache_mem_k, cache_mem_v, state_conv, page_table, g_mix, w_in, conv_w, w_conv_out, w_attn_out, w_o, g_mem, g_mem_kv, w_mq, w_mk, w_mv, w_mo, g_ffn, w_gate, w_up, w_down, g_final):
    depth = w_in.shape[0]
    assert depth == 1, "single-layer step"
    batch, seq, _ = x_prompt.shape
    nreq, dec_seq, _ = x_sample.shape
    assert dec_seq == 1
    n_mem = mem_prompt.shape[1]
    l = 0
    bf = lambda a: a.astype(BF16)
    row = lambda a: a.reshape(1, -1)

    w_main, w_gates = _prep_w_in(w_in[l])
    wco = bf(w_conv_out[l])
    w = dict(wao=bf(w_attn_out[l]), wo=bf(w_o[l]), gmem=row(g_mem[l]), wmq=bf(w_mq[l]), wmo=bf(w_mo[l]),
             gffn=row(g_ffn[l]), wg=bf(w_gate[l]), wu=bf(w_up[l]), wd=bf(w_down[l]), gfin=row(g_final))

    xp = x_prompt.reshape(batch * seq, D_MODEL)
    mkf, mvf, mkb, mvb = _memory_kv(mem_prompt.reshape(batch * n_mem, D_MODEL), row(g_mem_kv[l]),
                                    bf(w_mk[l]), bf(w_mv[l]))
    (kf, vf, ikf, oa, sgb, qt, kb, vt, iqt, ikb, iwt, conv_p) = _mixer_in(
        xp, row(g_mix[l]), w_main, w_gates, conv_w[l], wco, batch=batch, seq=seq, tm=KEY_CHUNK)
    attn = _prompt_attn(qt, iqt, iwt, kb, vt, ikb, batch=batch, seq=seq)
    yp = _post_prompt(xp, attn, oa, sgb, mkb, mvb, w, batch=batch, seq=seq, tm=POST_TILE)

    xs = x_sample.reshape(nreq, D_MODEL)
    st = state_conv[l]
    (kf_s, vf_s, ikf_s, oa_s, sgb_s, q_s, iq_s, iw_s, u_s) = _mixer_in(
        xs, row(g_mix[l]), w_main, w_gates, conv_w[l], wco, batch=nreq, seq=1, tm=nreq, state=(st[:, 0], st[:, 1]))
    attn_s = _sample_attn(page_table, q_s, iq_s, iw_s, kf_s, vf_s, ikf_s,
                          cache_k[l], cache_v[l], cache_idx_k[l])
    ys = _post_sample(xs, attn_s, oa_s, sgb_s, cache_mem_k[l], cache_mem_v[l], w)

    return (
        yp.reshape(batch, seq, D_MODEL),
        ys.reshape(nreq, 1, D_MODEL),
        kf.reshape(1, batch, seq, N_KV_HEADS, HEAD_DIM),
        vf.reshape(1, batch, seq, N_KV_HEADS, HEAD_DIM),
        ikf.reshape(1, batch, seq, IDX_DIM),
        conv_p.reshape(1, batch, CONV_WIDTH - 1, D_CONV),
        mkf.reshape(1, batch, n_mem, MEM_HEADS, MEM_HEAD_DIM),
        mvf.reshape(1, batch, n_mem, MEM_HEADS, MEM_HEAD_DIM),
        kf_s.reshape(1, nreq, 1, N_KV_HEADS, HEAD_DIM),
        vf_s.reshape(1, nreq, 1, N_KV_HEADS, HEAD_DIM),
        ikf_s.reshape(1, nreq, 1, IDX_DIM),
        jnp.stack([st[:, 1], u_s], axis=1).reshape(1, nreq, CONV_WIDTH - 1, D_CONV),
    )
```

```python
import functools

import jax
import jax.numpy as jnp
from jax import lax
from jax.experimental import pallas as pl
from jax.experimental.pallas import tpu as pltpu

F32 = jnp.float32
BF16 = jnp.bfloat16
I32 = jnp.int32

D_MODEL = 1024
D_CONV = 1024
CONV_WIDTH = 3
N_HEADS = 8
HEAD_DIM = 128
N_KV_HEADS = 2
GROUP = N_HEADS // N_KV_HEADS
IDX_HEADS = 4
IDX_DIM = 64
TOP_K_MAX = 256
MEM_HEADS = 4
MEM_HEAD_DIM = D_MODEL // MEM_HEADS
D_FF = 2816
EPS = 1e-6
NEG_INF = -1e30
IDX_SCALE = (IDX_HEADS * IDX_DIM) ** -0.5
ATTN_SCALE = HEAD_DIM ** -0.5
MEM_SCALE = MEM_HEAD_DIM ** -0.5
LOG2_E = 1.4426950408889634

LANES = 128
SUBLANES = 8
MXU_ROWS = 16
VMEM_LIMIT_BYTES = 56 * 1024 * 1024

COL_CIN = 0
COL_CB = COL_CIN + D_CONV
COL_CC = COL_CB + D_CONV
COL_Q = COL_CC + D_CONV
COL_K = COL_Q + N_HEADS * HEAD_DIM
COL_V = COL_K + N_KV_HEADS * HEAD_DIM
COL_IQ = COL_V + N_KV_HEADS * HEAD_DIM
COL_IK = COL_IQ + IDX_HEADS * IDX_DIM
COL_IW = COL_IK + IDX_DIM
IDX_RAW_END = COL_IW + IDX_HEADS
W_COLS = -(-IDX_RAW_END // LANES) * LANES

Q_BLOCK = 256
KEY_CHUNK = 512
COUNT_ROWS = 64
PLANE_ROWS = KEY_CHUNK // 32
SAMPLE_SEARCH_BITS = 4
SAMPLE_REQS = 4
FF_CHUNK = D_FF // 2
POST_TILE = 512
STABILISER_SLACK = 64.0
INT_MIN = -2 ** 31


def _rms(x, g):
    return x * lax.rsqrt(jnp.mean(x * x, axis=-1, keepdims=True) + EPS) * g


def _dot(a, b):
    return jnp.dot(a, b, preferred_element_type=F32)


def _dot_nt(a, b):
    return lax.dot_general(a, b, (((1,), (1,)), ((), ())), preferred_element_type=F32)


def _sigmoid(x):
    return 1.0 / (1.0 + jnp.exp(-x))


def _key_to_float(ukey):
    skey = ukey ^ INT_MIN
    bits = jnp.where(skey < 0, skey ^ 0x7FFFFFFF, skey)
    return lax.bitcast_convert_type(bits, F32)


def _float_to_key(x):
    bits = lax.bitcast_convert_type(x, I32)
    return bits ^ (lax.shift_right_arithmetic(bits, jnp.full_like(bits, 31)) & 0x7FFFFFFF) ^ INT_MIN


def _kth_largest(count_ge, top_k, shape):
    def body(b, ukey):
        cand = ukey | jnp.left_shift(jnp.int32(1), 31 - b)
        ok = count_ge(_key_to_float(cand)) >= float(top_k)
        return jnp.where(ok, cand, ukey)

    ukey = lax.fori_loop(0, 32, body, jnp.zeros(shape, I32))
    return _key_to_float(ukey)


def _kth_largest_by_digits(count_ge_fns, top_k, bits):
    digits = jnp.minimum(lax.broadcasted_iota(I32, (2 ** bits, 1), 0) + 1, 2 ** bits - 1)
    digits_f = digits.astype(F32)

    def body(rd, ukeys):
        shift = 32 - bits * (rd + 1)
        out = []
        for count_ge, ukey in zip(count_ge_fns, ukeys):
            cand = ukey | jnp.left_shift(digits, shift)
            ok = count_ge(_key_to_float(cand)) >= float(top_k)
            digit = jnp.max(jnp.where(ok, digits_f, 0.0), axis=0, keepdims=True).astype(I32)
            out.append(ukey | jnp.left_shift(digit, shift))
        return tuple(out)

    ukeys = lax.fori_loop(0, 32 // bits, body, tuple(jnp.zeros((1, 1), I32) for _ in count_ge_fns))
    return [_key_to_float(u) for u in ukeys]


def _bit_transpose32(words):
    a = list(words)
    j, m = 16, 0x0000FFFF
    while j:
        k = 0
        while k < 32:
            t = (a[k] ^ lax.shift_right_logical(a[k + j], jnp.full_like(a[k], j))) & m
            a[k] = a[k] ^ t
            a[k + j] = a[k + j] ^ lax.shift_left(t, jnp.full_like(t, j))
            k = (k + j + 1) & ~j
        j >>= 1
        m = (m ^ (m << j)) & 0xFFFFFFFF
    return a


def _radix_select(planes_ref, live, top_k):
    def body(i, carry):
        live, ukey, n_above = carry
        hit = live & planes_ref[i]
        cnt = jnp.sum(lax.population_count(hit).astype(F32), axis=0, keepdims=True)
        take = n_above + cnt >= float(top_k)
        live = jnp.where(take, hit, live ^ hit)
        ukey = jnp.where(take, ukey | jnp.left_shift(jnp.int32(1), 31 - i), ukey)
        n_above = jnp.where(take, n_above, n_above + cnt)
        return live, ukey, n_above

    lanes = live.shape[1]
    init = (live, jnp.zeros((1, lanes), I32), jnp.zeros((1, lanes), F32))
    return lax.fori_loop(0, 32, body, init)[1]


def _mixer_in_kernel(is_prompt, tm, *refs):
    if is_prompt:
        (x_ref, g_ref, w_ref, wgate_ref, cw_ref, wco_ref,
         kf_ref, vf_ref, ikf_ref, oa_ref, sgb_ref,
         qt_ref, kb_ref, vt_ref, iqt_ref, ikb_ref, iwt_ref, cs_ref, ubuf) = refs
    else:
        (x_ref, g_ref, w_ref, wgate_ref, cw_ref, wco_ref, s0_ref, s1_ref,
         kf_ref, vf_ref, ikf_ref, oa_ref, sgb_ref,
         q_ref, iq_ref, iw_ref, u_ref) = refs

    h = _rms(x_ref[...], g_ref[...]).astype(BF16)

    def proj(lo, hi):
        return _dot_nt(h, w_ref[lo:hi, :])

    u = proj(COL_CC, COL_CC + D_CONV) * proj(COL_CIN, COL_CIN + D_CONV)
    cw = cw_ref[...]
    if is_prompt:
        @pl.when(pl.program_id(1) == 0)
        def _():
            ubuf[0:SUBLANES, :] = jnp.zeros((SUBLANES, D_CONV), F32)

        ubuf[SUBLANES:SUBLANES + tm, :] = u
        conv = (ubuf[SUBLANES - 2:SUBLANES - 2 + tm, :] * cw[0:1]
                + ubuf[SUBLANES - 1:SUBLANES - 1 + tm, :] * cw[1:2] + u * cw[2:3])
        ubuf[0:SUBLANES, :] = ubuf[tm:tm + SUBLANES, :]
        cs_ref[0] = u[tm - (CONV_WIDTH - 1):, :]
    else:
        conv = s0_ref[...] * cw[0:1] + s1_ref[...] * cw[1:2] + u * cw[2:3]
        u_ref[...] = u

    a_in = (proj(COL_CB, COL_CB + D_CONV) * conv).astype(BF16)
    out_a = _dot(a_in, wco_ref[...])
    oa_ref[...] = (_sigmoid(_dot_nt(h, wgate_ref[:D_MODEL, :])) * out_a).astype(BF16)
    sgb_ref[...] = _sigmoid(_dot_nt(h, wgate_ref[D_MODEL:, :])).astype(BF16)

    q = proj(COL_Q, COL_K) * (ATTN_SCALE * LOG2_E if is_prompt else ATTN_SCALE)
    kv = proj(COL_K, COL_IQ)
    k = kv[:, :N_KV_HEADS * HEAD_DIM]
    v = kv[:, N_KV_HEADS * HEAD_DIM:]
    for g in range(N_KV_HEADS):
        head_rows = pl.ds(g, tm, stride=N_KV_HEADS)
        kf_ref[head_rows, :] = k[:, g * HEAD_DIM:(g + 1) * HEAD_DIM]
        vf_ref[head_rows, :] = v[:, g * HEAD_DIM:(g + 1) * HEAD_DIM]
    idx = proj(COL_IQ, W_COLS)
    iq = idx[:, :IDX_HEADS * IDX_DIM]
    ikw = idx[:, COL_IK - COL_IQ:]
    ik = ikw[:, :IDX_DIM]
    ikf_ref[...] = ik
    if is_prompt:
        kb_ref[...] = k.astype(BF16)
        ikb_ref[...] = ik.astype(BF16)
        vt_ref[0] = v.T.astype(BF16)
        for j in range(tm // Q_BLOCK):
            rows = slice(j * Q_BLOCK, (j + 1) * Q_BLOCK)
            for head in range(N_HEADS):
                g, hh = divmod(head, GROUP)
                qt_ref[j, g, :, hh * Q_BLOCK:(hh + 1) * Q_BLOCK] = (
                    q[rows, head * HEAD_DIM:(head + 1) * HEAD_DIM].T.astype(BF16))
            iqt_ref[j] = iq[rows].T.astype(BF16)
            iwt_ref[j] = ikw[rows].T[IDX_DIM:IDX_DIM + SUBLANES] * IDX_SCALE
    else:
        q_ref[...] = q.astype(BF16)
        for hd in range(IDX_HEADS):
            iq_ref[hd] = iq[:, hd * IDX_DIM:(hd + 1) * IDX_DIM].astype(BF16)
        iw_ref[...] = ikw[:, IDX_DIM:IDX_DIM + IDX_HEADS] * IDX_SCALE


def _const_spec(shape):
    nd = len(shape)
    return pl.BlockSpec(shape, lambda *_: (0,) * nd, pipeline_mode=pl.Buffered(1))


def _mixer_in(x, g_mix, w_main, w_gates, conv_w, w_conv_out, *, batch, seq, tm, state=None):
    is_prompt = state is None
    t_all = batch * seq
    nt = seq // tm if is_prompt else 1
    grid = (batch, nt) if is_prompt else (1, 1)
    tok = lambda width: pl.BlockSpec((tm, width), lambda b, t: (b * nt + t, 0))
    in_specs = [tok(D_MODEL), _const_spec((1, D_MODEL)), _const_spec((W_COLS, D_MODEL)), _const_spec((2 * D_MODEL, D_MODEL)),
                _const_spec((CONV_WIDTH, D_CONV)), _const_spec((D_CONV, D_MODEL))]
    args = [x, g_mix, w_main, w_gates, conv_w, w_conv_out]
    kvw = N_KV_HEADS * HEAD_DIM
    out_shapes = [
        jax.ShapeDtypeStruct((t_all * N_KV_HEADS, HEAD_DIM), F32),
        jax.ShapeDtypeStruct((t_all * N_KV_HEADS, HEAD_DIM), F32),
        jax.ShapeDtypeStruct((t_all, IDX_DIM), F32),
        jax.ShapeDtypeStruct((t_all, D_MODEL), BF16),
        jax.ShapeDtypeStruct((t_all, D_MODEL), BF16),
    ]
    kv_rows = pl.BlockSpec((tm * N_KV_HEADS, HEAD_DIM), lambda b, t: (b * nt + t, 0))
    out_specs = [kv_rows, kv_rows] + [tok(s.shape[1]) for s in out_shapes[2:]]
    scratch = []
    if is_prompt:
        assert tm == KEY_CHUNK and tm % Q_BLOCK == 0
        qb = tm // Q_BLOCK
        nblk = t_all // Q_BLOCK
        per_qblock = lambda *shape: pl.BlockSpec((qb,) + shape, lambda b, t: (b * nt + t,) + (0,) * len(shape))
        out_shapes += [
            jax.ShapeDtypeStruct((nblk, N_KV_HEADS, HEAD_DIM, GROUP * Q_BLOCK), BF16),
            jax.ShapeDtypeStruct((t_all, kvw), BF16),
            jax.ShapeDtypeStruct((t_all // tm, kvw, tm), BF16),
            jax.ShapeDtypeStruct((nblk, IDX_HEADS * IDX_DIM, Q_BLOCK), BF16),
            jax.ShapeDtypeStruct((t_all, IDX_DIM), BF16),
            jax.ShapeDtypeStruct((nblk, SUBLANES, Q_BLOCK), F32),
            jax.ShapeDtypeStruct((batch, CONV_WIDTH - 1, D_CONV), F32),
        ]
        out_specs += [
            per_qblock(N_KV_HEADS, HEAD_DIM, GROUP * Q_BLOCK), tok(kvw),
            pl.BlockSpec((1, kvw, tm), lambda b, t: (b * nt + t, 0, 0)),
            per_qblock(IDX_HEADS * IDX_DIM, Q_BLOCK), tok(IDX_DIM), per_qblock(SUBLANES, Q_BLOCK),
            pl.BlockSpec((1, CONV_WIDTH - 1, D_CONV), lambda b, t: (b, 0, 0)),
        ]
        scratch.append(pltpu.VMEM((tm + SUBLANES, D_CONV), F32))
    else:
        in_specs += [tok(D_CONV), tok(D_CONV)]
        args += list(state)
        out_shapes += [
            jax.ShapeDtypeStruct((t_all, N_HEADS * HEAD_DIM), BF16),
            jax.ShapeDtypeStruct((IDX_HEADS, t_all, IDX_DIM), BF16),
            jax.ShapeDtypeStruct((t_all, IDX_HEADS), F32),
            jax.ShapeDtypeStruct((t_all, D_CONV), F32),
        ]
        out_specs += [tok(N_HEADS * HEAD_DIM),
                      pl.BlockSpec((IDX_HEADS, tm, IDX_DIM), lambda b, t: (0, b * nt + t, 0)),
                      tok(IDX_HEADS), tok(D_CONV)]
    return pl.pallas_call(
        functools.partial(_mixer_in_kernel, is_prompt, tm),
        grid=grid, in_specs=in_specs, out_specs=out_specs, out_shape=out_shapes,
        scratch_shapes=scratch,
        compiler_params=pltpu.CompilerParams(
            dimension_semantics=("arbitrary", "arbitrary"), vmem_limit_bytes=VMEM_LIMIT_BYTES),
        name="mixer_in_prompt" if is_prompt else "mixer_in_sample",
    )(*args)


def _prompt_attn_kernel(top_k, qt_ref, iqt_ref, iwt_ref, k_ref, vt_ref, ik_ref, low_ref, o_ref,
                        sc_ref, planes_ref, m_ref, acc_ref, p_ref):
    i = pl.program_id(1)
    nch = i // (KEY_CHUNK // Q_BLOCK) + 1
    qpos = i * Q_BLOCK + lax.broadcasted_iota(I32, (1, Q_BLOCK), 1)

    @pl.when(i == 0)
    def _():
        planes_ref[...] = jnp.zeros(planes_ref.shape, I32)

    iw = iwt_ref[0]
    iqt = iqt_ref[0]
    iq_pairs = [jnp.concatenate([iqt[h * IDX_DIM:(h + 1) * IDX_DIM] for h in (2 * pr, 2 * pr + 1)], axis=1)
                for pr in range(IDX_HEADS // 2)]

    def score_chunk(c):
        off = pl.multiple_of(c * KEY_CHUNK, KEY_CHUNK)
        ikc = ik_ref[pl.ds(off, KEY_CHUNK), :]
        acc = jnp.zeros((KEY_CHUNK, Q_BLOCK), F32)
        for pr in range(IDX_HEADS // 2):
            s2 = _dot(ikc, iq_pairs[pr])
            for e in range(2):
                h = 2 * pr + e
                acc = acc + jnp.maximum(s2[:, e * Q_BLOCK:(e + 1) * Q_BLOCK], 0.0) * iw[h:h + 1]
        kpos = off + lax.broadcasted_iota(I32, (KEY_CHUNK, Q_BLOCK), 0)
        sc_ref[c] = jnp.where(kpos <= qpos, acc, -jnp.inf)

    def plane_chunk(c):
        keys = _float_to_key(sc_ref[c])
        for wd in range(PLANE_ROWS // SUBLANES):
            words = [keys[(wd * 32 + j) * SUBLANES:(wd * 32 + j + 1) * SUBLANES] for j in range(32)]
            row = pl.multiple_of(c * PLANE_ROWS + wd * SUBLANES, SUBLANES)
            for b, plane in enumerate(_bit_transpose32(words)):
                planes_ref[b, pl.ds(row, SUBLANES), :] = plane

    score_chunk(0)

    def score_step(c, carry):
        plane_chunk(c - 1)
        score_chunk(c)
        return carry

    lax.fori_loop(1, nch, score_step, 0)
    plane_chunk(nch - 1)

    def count(cmps, t):
        def body(c, accs):
            blk = sc_ref[c]
            hits = [jnp.where(cmp(blk, t), 1.0, 0.0) for cmp in cmps]
            return tuple(acc + jnp.sum(hit.reshape(KEY_CHUNK // COUNT_ROWS, COUNT_ROWS, Q_BLOCK), axis=0)
                         for acc, hit in zip(accs, hits))

        accs = lax.fori_loop(0, nch, body, tuple(jnp.zeros((COUNT_ROWS, Q_BLOCK), F32) for _ in cmps))
        return tuple(jnp.sum(acc, axis=0, keepdims=True) for acc in accs)

    ge = lambda a, b: a >= b
    gt = lambda a, b: a > b
    few = qpos < top_k
    plane_row = lax.broadcasted_iota(I32, (planes_ref.shape[1], Q_BLOCK), 0)
    t = _key_to_float(_radix_select(planes_ref, jnp.where(plane_row < nch * PLANE_ROWS, -1, 0), top_k))
    n_gt, n_ge = count((gt,), t)[0], count((ge,), t)[0]
    is_kth = few | ((n_gt < float(top_k)) & (n_ge >= float(top_k)))

    def recount():
        t2 = _kth_largest(lambda cand: count((ge,), cand)[0], top_k, (1, Q_BLOCK))
        return t2, count((gt,), t2)[0]

    t, n_gt = lax.cond(jnp.min(jnp.where(is_kth, 1.0, 0.0)) > 0.0, lambda: (t, n_gt), recount)
    t = jnp.where(few, -jnp.inf, t)
    need = jnp.where(few, 0.0, float(top_k) - n_gt)

    m_ref[...] = jnp.full(m_ref.shape, NEG_INF, F32)
    acc_ref[...] = jnp.zeros(acc_ref.shape, F32)
    ones_rows = jnp.ones((MXU_ROWS, KEY_CHUNK), BF16)

    heads = [(g, slice(hh * Q_BLOCK, (hh + 1) * Q_BLOCK)) for g in range(N_KV_HEADS) for hh in range(GROUP)]

    def chunk_bias(c, n_eq):
        blk = sc_ref[c]
        eq = blk == t
        eqf = jnp.where(eq, 1.0, 0.0)
        before = n_eq + _dot(low_ref[...], eqf.astype(BF16))
        sel = (blk > t) | (eq & (before < need))
        return jnp.where(sel, 0.0, NEG_INF), n_eq + jnp.sum(eqf, axis=0, keepdims=True)

    def head_logits(c, g, lanes, bias):
        off = pl.multiple_of(c * KEY_CHUNK, KEY_CHUNK)
        kc = k_ref[pl.ds(off, KEY_CHUNK), g * HEAD_DIM:(g + 1) * HEAD_DIM]
        return _dot(kc, qt_ref[0, g, :, lanes]) + bias

    def reweigh_chunk(c, bias):
        alphas = []
        for g, lanes in heads:
            sh = head_logits(c, g, lanes, bias)
            m_old = m_ref[g, :, lanes]
            m_new = jnp.maximum(m_old, jnp.max(sh, axis=0, keepdims=True))
            alphas.append(jnp.exp2(m_old - m_new))
            p_ref[g, :, lanes] = jnp.exp2(sh - m_new).astype(BF16)
            m_ref[g, :, lanes] = m_new
        return tuple(alphas)

    def weigh_chunk(c, n_eq):
        bias, n_eq = chunk_bias(c, n_eq)
        excess = jnp.full((1, Q_BLOCK), -jnp.inf, F32)
        for g, lanes in heads:
            sh = head_logits(c, g, lanes, bias)
            m_cur = m_ref[g, :, lanes]
            p_ref[g, :, lanes] = jnp.exp2(sh - m_cur).astype(BF16)
            excess = jnp.maximum(excess, jnp.max(sh, axis=0, keepdims=True) - m_cur)
        keep = tuple(jnp.ones((1, Q_BLOCK), F32) for _ in heads)
        alphas = lax.cond(jnp.max(excess) > STABILISER_SLACK, lambda: reweigh_chunk(c, bias), lambda: keep)
        return n_eq, alphas

    def accumulate_chunk(c, alphas):
        for g in range(N_KV_HEADS):
            vext = jnp.concatenate([vt_ref[c, g * HEAD_DIM:(g + 1) * HEAD_DIM, :], ones_rows], axis=0)
            alpha = jnp.concatenate(alphas[g * GROUP:(g + 1) * GROUP], axis=1)
            acc_ref[g] = alpha * acc_ref[g] + _dot(vext, p_ref[g])

    def attn_step(c, carry):
        n_eq, alphas = carry
        accumulate_chunk(c - 1, alphas)
        return weigh_chunk(c, n_eq)

    bias0, n_eq0 = chunk_bias(0, jnp.zeros((1, Q_BLOCK), F32))
    _, alphas = lax.fori_loop(1, nch, attn_step, (n_eq0, reweigh_chunk(0, bias0)))
    accumulate_chunk(nch - 1, alphas)

    for g in range(N_KV_HEADS):
        acc = acc_ref[g]
        o = acc[:HEAD_DIM] / acc[HEAD_DIM:HEAD_DIM + 1]
        for hh in range(GROUP):
            col = (g * GROUP + hh) * HEAD_DIM
            o_ref[:, col:col + HEAD_DIM] = o[:, hh * Q_BLOCK:(hh + 1) * Q_BLOCK].T.astype(o_ref.dtype)


def _prompt_attn(qt, iqt, iwt, kb, vt, ikb, *, batch, seq):
    nqb = seq // Q_BLOCK
    nch = seq // KEY_CHUNK
    kvw = N_KV_HEADS * HEAD_DIM
    top_k = min(TOP_K_MAX, seq // 4)
    low = jnp.tril(jnp.ones((KEY_CHUNK, KEY_CHUNK), BF16), k=-1)
    per_qblock = lambda *shape: pl.BlockSpec((1,) + shape, lambda b, i: (b * nqb + i,) + (0,) * len(shape))
    per_batch = lambda width: pl.BlockSpec((seq, width), lambda b, i: (b, 0))
    return pl.pallas_call(
        functools.partial(_prompt_attn_kernel, top_k),
        grid=(batch, nqb),
        in_specs=[per_qblock(N_KV_HEADS, HEAD_DIM, GROUP * Q_BLOCK),
                  per_qblock(IDX_HEADS * IDX_DIM, Q_BLOCK), per_qblock(SUBLANES, Q_BLOCK),
                  per_batch(kvw), pl.BlockSpec((nch, kvw, KEY_CHUNK), lambda b, i: (b, 0, 0)),
                  per_batch(IDX_DIM), _const_spec((KEY_CHUNK, KEY_CHUNK))],
        out_specs=pl.BlockSpec((Q_BLOCK, N_HEADS * HEAD_DIM), lambda b, i: (b * nqb + i, 0)),
        out_shape=jax.ShapeDtypeStruct((batch * seq, N_HEADS * HEAD_DIM), BF16),
        scratch_shapes=[
            pltpu.VMEM((nch, KEY_CHUNK, Q_BLOCK), F32),
            pltpu.VMEM((32, nch * PLANE_ROWS, Q_BLOCK), I32),
            pltpu.VMEM((N_KV_HEADS, 1, GROUP * Q_BLOCK), F32),
            pltpu.VMEM((N_KV_HEADS, HEAD_DIM + MXU_ROWS, GROUP * Q_BLOCK), F32),
            pltpu.VMEM((N_KV_HEADS, KEY_CHUNK, GROUP * Q_BLOCK), BF16),
        ],
        compiler_params=pltpu.CompilerParams(
            dimension_semantics=("arbitrary", "arbitrary"), vmem_limit_bytes=VMEM_LIMIT_BYTES),
        name="prompt_attn",
    )(qt, iqt, iwt, kb, vt, ikb, low)


def _sample_attn_kernel(n_pages, page, top_k, rb, pt_ref, q_ref, iq_ref, iw_ref, kn_ref, vn_ref, ikn_ref,
                        ck_hbm, cv_hbm, cik_hbm, tri_ref, low_ref, o_ref,
                        kbuf, vbuf, ikbuf, sems, sc_ref):
    b = pl.program_id(0)
    nb = pl.num_programs(0)
    past = n_pages * page

    def page_copies(step, slot):
        copies = []
        for r in range(rb):
            for p in range(n_pages):
                phys = pt_ref[step * rb + r, p]
                rows = pl.ds(p * page * N_KV_HEADS, page * N_KV_HEADS)
                lanes = pl.ds(p * page, page)
                copies.append(pltpu.make_async_copy(ck_hbm.at[phys], kbuf.at[slot, r, rows], sems.at[0, slot]))
                copies.append(pltpu.make_async_copy(cv_hbm.at[phys], vbuf.at[slot, r, rows], sems.at[1, slot]))
                copies.append(
                    pltpu.make_async_copy(cik_hbm.at[phys], ikbuf.at[slot, r, :, lanes], sems.at[2, slot]))
        return copies

    slot = b % 2

    @pl.when(b == 0)
    def _():
        for c in page_copies(0, 0):
            c.start()

    @pl.when(b + 1 < nb)
    def _():
        for c in page_copies(b + 1, 1 - slot):
            c.start()

    for c in page_copies(b, slot):
        c.wait()

    sc_rows, sc_news = [], []
    for r in range(rb):
        iq = iq_ref[r].astype(BF16)
        iw = iw_ref[r]
        sidx = _dot(iq, ikbuf[slot, r].astype(BF16))
        sc_row = jnp.sum(jnp.maximum(sidx, 0.0) * iw, axis=0, keepdims=True)
        for p in range(n_pages):
            sc_ref[r, p:p + 1, :] = sc_row[:, p * page:(p + 1) * page]
        ikn = ikn_ref[r].astype(BF16).astype(F32)
        s_new = jnp.sum(iq.astype(F32) * ikn, axis=1, keepdims=True)
        sc_rows.append(sc_row)
        sc_news.append(jnp.sum(jnp.maximum(s_new, 0.0) * iw, axis=0, keepdims=True))

    def count_ge(r, cand):
        hits = jnp.where(sc_rows[r] >= cand, 1.0, 0.0)
        return jnp.sum(hits, axis=1, keepdims=True) + jnp.where(sc_news[r] >= cand, 1.0, 0.0)

    ts = _kth_largest_by_digits([functools.partial(count_ge, r) for r in range(rb)], top_k,
                                SAMPLE_SEARCH_BITS)

    def total(x):
        return jnp.sum(jnp.sum(x, axis=1, keepdims=True), axis=0, keepdims=True)

    for r in range(rb):
        t, sc, sc_new = ts[r], sc_ref[r], sc_news[r]
        n_gt = total(jnp.where(sc > t, 1.0, 0.0)) + jnp.where(sc_new > t, 1.0, 0.0)
        need = float(top_k) - n_gt
        eq = sc == t
        eqf = jnp.where(eq, 1.0, 0.0)
        in_row = _dot(eqf.astype(BF16), tri_ref[...])
        row_tot = jnp.broadcast_to(jnp.sum(eqf, axis=1, keepdims=True), sc.shape)
        rows_before = _dot(low_ref[...], row_tot.astype(BF16))
        sel = (sc > t) | (eq & (in_row + rows_before < need))
        sel_new = (sc_new > t) | ((sc_new == t) & (total(eqf) < need))
        bias = jnp.where(sel, 0.0, NEG_INF)
        bias_row = jnp.concatenate([bias[p:p + 1, :] for p in range(n_pages)], axis=1)
        bias_new = jnp.where(sel_new, 0.0, NEG_INF)

        q = q_ref[r].astype(BF16)
        qf = q.astype(F32)
        for g in range(N_KV_HEADS):
            head_rows = pl.ds(g, past, stride=N_KV_HEADS)
            kg = kbuf[slot, r, head_rows, :].astype(BF16)
            vg = vbuf[slot, r, head_rows, :].astype(BF16)
            kn = kn_ref[r, g:g + 1, :].astype(BF16).astype(F32)
            vn = vn_ref[r, g:g + 1, :].astype(BF16).astype(F32)
            s = _dot_nt(q, kg) + bias_row
            sn = jnp.sum(qf * kn, axis=1, keepdims=True) + bias_new
            m = jnp.maximum(jnp.max(s, axis=1, keepdims=True), sn)
            p = jnp.exp(s - m)
            pn = jnp.exp(sn - m)
            l = jnp.sum(p, axis=1, keepdims=True) + pn
            o = (_dot(p.astype(BF16), vg) + pn * vn) / l
            o_ref[r, g * GROUP:(g + 1) * GROUP, :] = o[g * GROUP:(g + 1) * GROUP]


def _sample_attn(page_table, q, iq, iw, k_new, v_new, ik_new, cache_k, cache_v, cache_ik):
    nreq, n_pages = page_table.shape
    n_phys, page = cache_k.shape[0], cache_k.shape[1]
    kvw = N_KV_HEADS * HEAD_DIM
    top_k = min(TOP_K_MAX, (n_pages * page + 1) // 4)
    assert top_k < n_pages * page + 1
    ck = cache_k.reshape(n_phys, page * N_KV_HEADS, HEAD_DIM)
    cv = cache_v.reshape(n_phys, page * N_KV_HEADS, HEAD_DIM)
    cik = jnp.swapaxes(cache_ik, 1, 2)
    tri = jnp.triu(jnp.ones((page, page), BF16), k=1)
    low = jnp.tril(jnp.ones((n_pages, n_pages), BF16), k=-1)
    rb = SAMPLE_REQS
    assert nreq % rb == 0
    pad_rows = lambda a: jnp.pad(a.astype(F32), ((0, 0), (0, MXU_ROWS - a.shape[1]), (0, 0)))
    per_req = lambda *shape: pl.BlockSpec((rb,) + shape, lambda b, pt: (b,) + (0,) * len(shape))
    const = lambda *shape: pl.BlockSpec(shape, lambda b, pt: (0,) * len(shape))
    any_spec = pl.BlockSpec(memory_space=pl.ANY)
    grid_spec = pltpu.PrefetchScalarGridSpec(
        num_scalar_prefetch=1,
        grid=(nreq // rb,),
        in_specs=[per_req(MXU_ROWS, HEAD_DIM), per_req(MXU_ROWS, IDX_DIM), per_req(MXU_ROWS, 1),
                  per_req(N_KV_HEADS, HEAD_DIM), per_req(N_KV_HEADS, HEAD_DIM), per_req(1, IDX_DIM),
                  any_spec, any_spec, any_spec, const(page, page), const(n_pages, n_pages)],
        out_specs=per_req(N_HEADS, HEAD_DIM),
        scratch_shapes=[
            pltpu.VMEM((2, rb, n_pages * page * N_KV_HEADS, HEAD_DIM), F32),
            pltpu.VMEM((2, rb, n_pages * page * N_KV_HEADS, HEAD_DIM), F32),
            pltpu.VMEM((2, rb, IDX_DIM, n_pages * page), F32),
            pltpu.SemaphoreType.DMA((3, 2)),
            pltpu.VMEM((rb, n_pages, page), F32),
        ],
    )
    out = pl.pallas_call(
        functools.partial(_sample_attn_kernel, n_pages, page, top_k, rb),
        grid_spec=grid_spec,
        out_shape=jax.ShapeDtypeStruct((nreq, N_HEADS, HEAD_DIM), F32),
        compiler_params=pltpu.CompilerParams(
            dimension_semantics=("arbitrary",), vmem_limit_bytes=VMEM_LIMIT_BYTES),
        name="sample_attn",
    )(page_table,
      pad_rows(q.reshape(nreq, N_HEADS, HEAD_DIM)),
      pad_rows(jnp.transpose(iq, (1, 0, 2))),
      pad_rows(iw.reshape(nreq, IDX_HEADS, 1)),
      k_new.reshape(nreq, N_KV_HEADS, HEAD_DIM), v_new.reshape(nreq, N_KV_HEADS, HEAD_DIM),
      ik_new.reshape(nreq, 1, IDX_DIM),
      ck, cv, cik, tri, low)
    return out.reshape(nreq, N_HEADS * HEAD_DIM)


def _merge_stage(x, attn, oa, sgb, wao_ref, wo_ref, gmem_ref, wmq_ref):
    out_b = _dot(attn, wao_ref[...])
    merged = oa.astype(F32) + sgb.astype(F32) * out_b
    x1 = x + _dot(merged.astype(BF16), wo_ref[...])
    hm = _rms(x1, gmem_ref[...]).astype(BF16)
    return x1, _dot(hm, wmq_ref[...]) * MEM_SCALE


def _ffn_stage(x1, mem_o, wmo_ref, gffn_ref, wg_ref, wu_ref, wd_ref, gfin_ref):
    x2 = x1 + _dot(mem_o.astype(BF16), wmo_ref[...])
    hf = _rms(x2, gffn_ref[...]).astype(BF16)
    acc = jnp.zeros_like(x2)
    for c in range(D_FF // FF_CHUNK):
        cols = slice(c * FF_CHUNK, (c + 1) * FF_CHUNK)
        gate = _dot(hf, wg_ref[:, cols])
        f = gate * _sigmoid(gate) * _dot(hf, wu_ref[:, cols])
        acc = acc + _dot(f.astype(BF16), wd_ref[cols, :])
    return _rms(x2 + acc, gfin_ref[...])


def _post_prompt_kernel(x_ref, attn_ref, oa_ref, sgb_ref, mk_ref, mv_ref, wao_ref, wo_ref, gmem_ref,
                        wmq_ref, wmo_ref, gffn_ref, wg_ref, wu_ref, wd_ref, gfin_ref, y_ref, mo_ref):
    x1, qm = _merge_stage(x_ref[...], attn_ref[...], oa_ref[...], sgb_ref[...],
                          wao_ref, wo_ref, gmem_ref, wmq_ref)
    for h in range(MEM_HEADS):
        cols = slice(h * MEM_HEAD_DIM, (h + 1) * MEM_HEAD_DIM)
        s = _dot_nt(qm[:, cols].astype(BF16), mk_ref[:, cols])
        p = jnp.exp(s - jnp.max(s, axis=1, keepdims=True))
        o = _dot(p.astype(BF16), mv_ref[:, cols]) / jnp.sum(p, axis=1, keepdims=True)
        mo_ref[:, cols] = o.astype(BF16)
    y_ref[...] = _ffn_stage(x1, mo_ref[...], wmo_ref, gffn_ref, wg_ref, wu_ref, wd_ref, gfin_ref)


def _post_prompt(x, attn, oa, sgb, mk, mv, w, *, batch, seq, tm):
    nt = seq // tm
    n_mem = mk.shape[0] // batch
    tok = lambda width: pl.BlockSpec((tm, width), lambda b, t: (b * nt + t, 0))
    mem = pl.BlockSpec((n_mem, D_MODEL), lambda b, t: (b, 0))
    sq = _const_spec((D_MODEL, D_MODEL))
    gain = _const_spec((1, D_MODEL))
    return pl.pallas_call(
        _post_prompt_kernel,
        grid=(batch, nt),
        in_specs=[tok(D_MODEL), tok(D_MODEL), tok(D_MODEL), tok(D_MODEL), mem, mem,
                  sq, sq, gain, sq, sq, gain,
                  _const_spec((D_MODEL, D_FF)), _const_spec((D_MODEL, D_FF)), _const_spec((D_FF, D_MODEL)),
                  gain],
        out_specs=tok(D_MODEL),
        out_shape=jax.ShapeDtypeStruct((batch * seq, D_MODEL), F32),
        scratch_shapes=[pltpu.VMEM((tm, D_MODEL), BF16)],
        compiler_params=pltpu.CompilerParams(
            dimension_semantics=("arbitrary", "arbitrary"), vmem_limit_bytes=VMEM_LIMIT_BYTES),
        name="post_prompt",
    )(x, attn, oa, sgb, mk, mv, w["wao"], w["wo"], w["gmem"], w["wmq"], w["wmo"], w["gffn"],
      w["wg"], w["wu"], w["wd"], w["gfin"])


def _merge_sample_kernel(x_ref, attn_ref, oa_ref, sgb_ref, wao_ref, wo_ref, gmem_ref, wmq_ref,
                         x1_ref, qm_ref):
    x1, qm = _merge_stage(x_ref[...], attn_ref[...].astype(BF16), oa_ref[...], sgb_ref[...],
                          wao_ref, wo_ref, gmem_ref, wmq_ref)
    x1_ref[...] = x1
    qm_ref[...] = qm


def _mem_sample_kernel(rb, q_ref, mk_ref, mv_ref, o_ref):
    for r in range(rb):
        s = jnp.sum(mk_ref[r] * q_ref[r][None], axis=2, keepdims=True)
        p = jnp.exp(s - jnp.max(s, axis=0, keepdims=True))
        o_ref[r] = jnp.sum(p * mv_ref[r], axis=0) / jnp.sum(p, axis=0)


def _ffn_sample_kernel(x1_ref, mo_ref, wmo_ref, gffn_ref, wg_ref, wu_ref, wd_ref, gfin_ref, y_ref):
    y_ref[...] = _ffn_stage(x1_ref[...], mo_ref[...], wmo_ref, gffn_ref, wg_ref, wu_ref, wd_ref, gfin_ref)


def _post_sample(x, attn, oa, sgb, mem_k, mem_v, w):
    n = x.shape[0]
    n_mem = mem_k.shape[1]
    full = lambda *shape: _const_spec(shape)
    act = full(n, D_MODEL)
    sq = full(D_MODEL, D_MODEL)
    gain = full(1, D_MODEL)
    params = pltpu.CompilerParams(dimension_semantics=("arbitrary",), vmem_limit_bytes=VMEM_LIMIT_BYTES)
    x1, qm = pl.pallas_call(
        _merge_sample_kernel, grid=(1,),
        in_specs=[act, act, act, act, sq, sq, gain, sq], out_specs=[act, act],
        out_shape=[jax.ShapeDtypeStruct((n, D_MODEL), F32)] * 2,
        compiler_params=params, name="merge_sample",
    )(x, attn, oa, sgb, w["wao"], w["wo"], w["gmem"], w["wmq"])

    rb = 4
    heads = pl.BlockSpec((rb, MEM_HEADS, MEM_HEAD_DIM), lambda i: (i, 0, 0))
    mem = pl.BlockSpec((rb, n_mem, MEM_HEADS, MEM_HEAD_DIM), lambda i: (i, 0, 0, 0))
    mem_o = pl.pallas_call(
        functools.partial(_mem_sample_kernel, rb), grid=(n // rb,),
        in_specs=[heads, mem, mem], out_specs=heads,
        out_shape=jax.ShapeDtypeStruct((n, MEM_HEADS, MEM_HEAD_DIM), F32),
        compiler_params=params, name="mem_sample",
    )(qm.reshape(n, MEM_HEADS, MEM_HEAD_DIM), mem_k, mem_v)

    return pl.pallas_call(
        _ffn_sample_kernel, grid=(1,),
        in_specs=[act, act, sq, gain, full(D_MODEL, D_FF), full(D_MODEL, D_FF), full(D_FF, D_MODEL), gain],
        out_specs=act, out_shape=jax.ShapeDtypeStruct((n, D_MODEL), F32),
        compiler_params=params, name="ffn_sample",
    )(x1, mem_o.reshape(n, D_MODEL), w["wmo"], w["gffn"], w["wg"], w["wu"], w["wd"], w["gfin"])


def _memory_kv_kernel(mem_ref, g_ref, wk_ref, wv_ref, kf_ref, vf_ref, kb_ref, vb_ref):
    m = _rms(mem_ref[...], g_ref[...]).astype(BF16)
    k = _dot(m, wk_ref[...])
    v = _dot(m, wv_ref[...])
    kf_ref[...] = k
    vf_ref[...] = v
    kb_ref[...] = k.astype(BF16)
    vb_ref[...] = v.astype(BF16)


def _memory_kv(mem, g, wk, wv):
    n = mem.shape[0]
    full = lambda *shape: _const_spec(shape)
    act = full(n, D_MODEL)
    sq = full(D_MODEL, D_MODEL)
    return pl.pallas_call(
        _memory_kv_kernel, grid=(1,),
        in_specs=[act, full(1, D_MODEL), sq, sq], out_specs=[act] * 4,
        out_shape=[jax.ShapeDtypeStruct((n, D_MODEL), F32)] * 2 + [jax.ShapeDtypeStruct((n, D_MODEL), BF16)] * 2,
        compiler_params=pltpu.CompilerParams(
            dimension_semantics=("arbitrary",), vmem_limit_bytes=VMEM_LIMIT_BYTES),
        name="memory_kv",
    )(mem, g, wk, wv)


def _prep_w_in(w_in):
    w_t = w_in.T
    main = jnp.pad(w_t[:IDX_RAW_END].astype(BF16), ((0, W_COLS - IDX_RAW_END), (0, 0)))
    return main, w_t[IDX_RAW_END:].astype(BF16)


def kernel(x_prompt, x_sample, mem_prompt, cache_k, cache_v, cache_idx_k, cache_mem_k, cache_mem_v, state_conv, page_table, g_mix, w_in, conv_w, w_conv_out, w_attn_out, w_o, g_mem, g_mem_kv, w_mq, w_mk, w_mv, w_mo, g_ffn, w_gate, w_up, w_down, g_final):
    depth = w_in.shape[0]
    assert depth == 1, "single-layer step"
    batch, seq, _ = x_prompt.shape
    nreq, dec_seq, _ = x_sample.shape
    assert dec_seq == 1
    n_mem = mem_prompt.shape[1]
    l = 0
    bf = lambda a: a.astype(BF16)
    row = lambda a: a.reshape(1, -1)

    w_main, w_gates = _prep_w_in(w_in[l])
    wco = bf(w_conv_out[l])
    w = dict(wao=bf(w_attn_out[l]), wo=bf(w_o[l]), gmem=row(g_mem[l]), wmq=bf(w_mq[l]), wmo=bf(w_mo[l]),
             gffn=row(g_ffn[l]), wg=bf(w_gate[l]), wu=bf(w_up[l]), wd=bf(w_down[l]), gfin=row(g_final))

    xp = x_prompt.reshape(batch * seq, D_MODEL)
    mkf, mvf, mkb, mvb = _memory_kv(mem_prompt.reshape(batch * n_mem, D_MODEL), row(g_mem_kv[l]),
                                    bf(w_mk[l]), bf(w_mv[l]))
    (kf, vf, ikf, oa, sgb, qt, kb, vt, iqt, ikb, iwt, conv_p) = _mixer_in(
        xp, row(g_mix[l]), w_main, w_gates, conv_w[l], wco, batch=batch, seq=seq, tm=KEY_CHUNK)
    attn = _prompt_attn(qt, iqt, iwt, kb, vt, ikb, batch=batch, seq=seq)
    yp = _post_prompt(xp, attn, oa, sgb, mkb, mvb, w, batch=batch, seq=seq, tm=POST_TILE)

    xs = x_sample.reshape(nreq, D_MODEL)
    st = state_conv[l]
    (kf_s, vf_s, ikf_s, oa_s, sgb_s, q_s, iq_s, iw_s, u_s) = _mixer_in(
        xs, row(g_mix[l]), w_main, w_gates, conv_w[l], wco, batch=nreq, seq=1, tm=nreq, state=(st[:, 0], st[:, 1]))
    attn_s = _sample_attn(page_table, q_s, iq_s, iw_s, kf_s, vf_s, ikf_s,
                          cache_k[l], cache_v[l], cache_idx_k[l])
    ys = _post_sample(xs, attn_s, oa_s, sgb_s, cache_mem_k[l], cache_mem_v[l], w)

    return (
        yp.reshape(batch, seq, D_MODEL),
        ys.reshape(nreq, 1, D_MODEL),
        kf.reshape(1, batch, seq, N_KV_HEADS, HEAD_DIM),
        vf.reshape(1, batch, seq, N_KV_HEADS, HEAD_DIM),
        ikf.reshape(1, batch, seq, IDX_DIM),
        conv_p.reshape(1, batch, CONV_WIDTH - 1, D_CONV),
        mkf.reshape(1, batch, n_mem, MEM_HEADS, MEM_HEAD_DIM),
        mvf.reshape(1, batch, n_mem, MEM_HEADS, MEM_HEAD_DIM),
        kf_s.reshape(1, nreq, 1, N_KV_HEADS, HEAD_DIM),
        vf_s.reshape(1, nreq, 1, N_KV_HEADS, HEAD_DIM),
        ikf_s.reshape(1, nreq, 1, IDX_DIM),
        jnp.stack([st[:, 1], u_s], axis=1).reshape(1, nreq, CONV_WIDTH - 1, D_CONV),
    )
```

```python
import functools

import jax
import jax.numpy as jnp
from jax import lax
from jax.experimental import pallas as pl
from jax.experimental.pallas import tpu as pltpu

F32 = jnp.float32
BF16 = jnp.bfloat16
I32 = jnp.int32

D_MODEL = 1024
D_CONV = 1024
CONV_WIDTH = 3
N_HEADS = 8
HEAD_DIM = 128
N_KV_HEADS = 2
GROUP = N_HEADS // N_KV_HEADS
IDX_HEADS = 4
IDX_DIM = 64
TOP_K_MAX = 256
MEM_HEADS = 4
MEM_HEAD_DIM = D_MODEL // MEM_HEADS
D_FF = 2816
EPS = 1e-6
NEG_INF = -1e30
IDX_SCALE = (IDX_HEADS * IDX_DIM) ** -0.5
ATTN_SCALE = HEAD_DIM ** -0.5
MEM_SCALE = MEM_HEAD_DIM ** -0.5
LOG2_E = 1.4426950408889634

LANES = 128
SUBLANES = 8
MXU_ROWS = 16
VMEM_LIMIT_BYTES = 56 * 1024 * 1024

COL_CIN = 0
COL_CB = COL_CIN + D_CONV
COL_CC = COL_CB + D_CONV
COL_Q = COL_CC + D_CONV
COL_K = COL_Q + N_HEADS * HEAD_DIM
COL_V = COL_K + N_KV_HEADS * HEAD_DIM
COL_IQ = COL_V + N_KV_HEADS * HEAD_DIM
COL_IK = COL_IQ + IDX_HEADS * IDX_DIM
COL_IW = COL_IK + IDX_DIM
IDX_RAW_END = COL_IW + IDX_HEADS
W_COLS = -(-IDX_RAW_END // LANES) * LANES

Q_BLOCK = 256
KEY_CHUNK = 512
COUNT_ROWS = 64
PLANE_ROWS = KEY_CHUNK // 32
TIE_BLOCK = 128
SAMPLE_SEARCH_BITS = 4
SAMPLE_REQS = 4
FF_CHUNK = D_FF // 2
POST_TILE = 512
STABILISER_SLACK = 64.0
INT_MIN = -2 ** 31


def _rms(x, g):
    return x * lax.rsqrt(jnp.mean(x * x, axis=-1, keepdims=True) + EPS) * g


def _dot(a, b):
    return jnp.dot(a, b, preferred_element_type=F32)


def _dot_nt(a, b):
    return lax.dot_general(a, b, (((1,), (1,)), ((), ())), preferred_element_type=F32)


def _sigmoid(x):
    return 1.0 / (1.0 + jnp.exp(-x))


def _key_to_float(ukey):
    skey = ukey ^ INT_MIN
    bits = jnp.where(skey < 0, skey ^ 0x7FFFFFFF, skey)
    return lax.bitcast_convert_type(bits, F32)


def _float_to_key(x):
    bits = lax.bitcast_convert_type(x, I32)
    return bits ^ (lax.shift_right_arithmetic(bits, jnp.full_like(bits, 31)) & 0x7FFFFFFF) ^ INT_MIN


def _kth_largest(count_ge, top_k, shape):
    def body(b, ukey):
        cand = ukey | jnp.left_shift(jnp.int32(1), 31 - b)
        ok = count_ge(_key_to_float(cand)) >= float(top_k)
        return jnp.where(ok, cand, ukey)

    ukey = lax.fori_loop(0, 32, body, jnp.zeros(shape, I32))
    return _key_to_float(ukey)


def _kth_largest_by_digits(count_ge_fns, top_k, bits):
    digits = jnp.minimum(lax.broadcasted_iota(I32, (2 ** bits, 1), 0) + 1, 2 ** bits - 1)
    digits_f = digits.astype(F32)

    def body(rd, ukeys):
        shift = 32 - bits * (rd + 1)
        out = []
        for count_ge, ukey in zip(count_ge_fns, ukeys):
            cand = ukey | jnp.left_shift(digits, shift)
            ok = count_ge(_key_to_float(cand)) >= float(top_k)
            digit = jnp.max(jnp.where(ok, digits_f, 0.0), axis=0, keepdims=True).astype(I32)
            out.append(ukey | jnp.left_shift(digit, shift))
        return tuple(out)

    ukeys = lax.fori_loop(0, 32 // bits, body, tuple(jnp.zeros((1, 1), I32) for _ in count_ge_fns))
    return [_key_to_float(u) for u in ukeys]


def _bit_transpose32(words):
    a = list(words)
    j, m = 16, 0x0000FFFF
    while j:
        k = 0
        while k < 32:
            t = (a[k] ^ lax.shift_right_logical(a[k + j], jnp.full_like(a[k], j))) & m
            a[k] = a[k] ^ t
            a[k + j] = a[k + j] ^ lax.shift_left(t, jnp.full_like(t, j))
            k = (k + j + 1) & ~j
        j >>= 1
        m = (m ^ (m << j)) & 0xFFFFFFFF
    return a


def _radix_select(planes_ref, live, top_k):
    rows = live.shape[0]

    def body(i, carry):
        live, ukey, n_above = carry
        hit = live & planes_ref[i, :rows]
        cnt = jnp.sum(lax.population_count(hit).astype(F32), axis=0, keepdims=True)
        take = n_above + cnt >= float(top_k)
        live = jnp.where(take, hit, live ^ hit)
        ukey = jnp.where(take, ukey | jnp.left_shift(jnp.int32(1), 31 - i), ukey)
        n_above = jnp.where(take, n_above, n_above + cnt)
        return live, ukey, n_above

    lanes = live.shape[1]
    init = (live, jnp.zeros((1, lanes), I32), jnp.zeros((1, lanes), F32))
    return lax.fori_loop(0, 32, body, init)[1]


def _mixer_in_kernel(is_prompt, tm, *refs):
    if is_prompt:
        (x_ref, g_ref, w_ref, wgate_ref, cw_ref, wco_ref,
         kf_ref, vf_ref, ikf_ref, oa_ref, sgb_ref,
         qt_ref, kb_ref, vt_ref, iqt_ref, ikb_ref, iwt_ref, cs_ref, ubuf) = refs
    else:
        (x_ref, g_ref, w_ref, wgate_ref, cw_ref, wco_ref, s0_ref, s1_ref,
         kf_ref, vf_ref, ikf_ref, oa_ref, sgb_ref,
         q_ref, iq_ref, iw_ref, u_ref) = refs

    h = _rms(x_ref[...], g_ref[...]).astype(BF16)

    def proj(lo, hi):
        return _dot_nt(h, w_ref[lo:hi, :])

    u = proj(COL_CC, COL_CC + D_CONV) * proj(COL_CIN, COL_CIN + D_CONV)
    cw = cw_ref[...]
    if is_prompt:
        @pl.when(pl.program_id(1) == 0)
        def _():
            ubuf[0:SUBLANES, :] = jnp.zeros((SUBLANES, D_CONV), F32)

        ubuf[SUBLANES:SUBLANES + tm, :] = u
        conv = (ubuf[SUBLANES - 2:SUBLANES - 2 + tm, :] * cw[0:1]
                + ubuf[SUBLANES - 1:SUBLANES - 1 + tm, :] * cw[1:2] + u * cw[2:3])
        ubuf[0:SUBLANES, :] = ubuf[tm:tm + SUBLANES, :]
        cs_ref[0] = u[tm - (CONV_WIDTH - 1):, :]
    else:
        conv = s0_ref[...] * cw[0:1] + s1_ref[...] * cw[1:2] + u * cw[2:3]
        u_ref[...] = u

    a_in = (proj(COL_CB, COL_CB + D_CONV) * conv).astype(BF16)
    out_a = _dot(a_in, wco_ref[...])
    oa_ref[...] = (_sigmoid(_dot_nt(h, wgate_ref[:D_MODEL, :])) * out_a).astype(BF16)
    sgb_ref[...] = _sigmoid(_dot_nt(h, wgate_ref[D_MODEL:, :])).astype(BF16)

    q = proj(COL_Q, COL_K) * (ATTN_SCALE * LOG2_E if is_prompt else ATTN_SCALE)
    kv = proj(COL_K, COL_IQ)
    k = kv[:, :N_KV_HEADS * HEAD_DIM]
    v = kv[:, N_KV_HEADS * HEAD_DIM:]
    for g in range(N_KV_HEADS):
        head_rows = pl.ds(g, tm, stride=N_KV_HEADS)
        kf_ref[head_rows, :] = k[:, g * HEAD_DIM:(g + 1) * HEAD_DIM]
        vf_ref[head_rows, :] = v[:, g * HEAD_DIM:(g + 1) * HEAD_DIM]
    idx = proj(COL_IQ, W_COLS)
    iq = idx[:, :IDX_HEADS * IDX_DIM]
    ikw = idx[:, COL_IK - COL_IQ:]
    ik = ikw[:, :IDX_DIM]
    ikf_ref[...] = ik
    if is_prompt:
        kb_ref[...] = k.astype(BF16)
        ikb_ref[...] = ik.astype(BF16)
        vt_ref[0] = v.T.astype(BF16)
        for j in range(tm // Q_BLOCK):
            rows = slice(j * Q_BLOCK, (j + 1) * Q_BLOCK)
            for head in range(N_HEADS):
                g, hh = divmod(head, GROUP)
                qt_ref[j, g, :, hh * Q_BLOCK:(hh + 1) * Q_BLOCK] = (
                    q[rows, head * HEAD_DIM:(head + 1) * HEAD_DIM].T.astype(BF16))
            iqt_ref[j] = iq[rows].T.astype(BF16)
            iwt_ref[j] = ikw[rows].T[IDX_DIM:IDX_DIM + SUBLANES] * IDX_SCALE
    else:
        q_ref[...] = q.astype(BF16)
        for hd in range(IDX_HEADS):
            iq_ref[hd] = iq[:, hd * IDX_DIM:(hd + 1) * IDX_DIM].astype(BF16)
        iw_ref[...] = ikw[:, IDX_DIM:IDX_DIM + IDX_HEADS] * IDX_SCALE


def _const_spec(shape):
    nd = len(shape)
    return pl.BlockSpec(shape, lambda *_: (0,) * nd, pipeline_mode=pl.Buffered(1))


def _mixer_in(x, g_mix, w_main, w_gates, conv_w, w_conv_out, *, batch, seq, tm, state=None):
    is_prompt = state is None
    t_all = batch * seq
    nt = seq // tm if is_prompt else 1
    grid = (batch, nt) if is_prompt else (1, 1)
    tok = lambda width: pl.BlockSpec((tm, width), lambda b, t: (b * nt + t, 0))
    in_specs = [tok(D_MODEL), _const_spec((1, D_MODEL)), _const_spec((W_COLS, D_MODEL)), _const_spec((2 * D_MODEL, D_MODEL)),
                _const_spec((CONV_WIDTH, D_CONV)), _const_spec((D_CONV, D_MODEL))]
    args = [x, g_mix, w_main, w_gates, conv_w, w_conv_out]
    kvw = N_KV_HEADS * HEAD_DIM
    out_shapes = [
        jax.ShapeDtypeStruct((t_all * N_KV_HEADS, HEAD_DIM), F32),
        jax.ShapeDtypeStruct((t_all * N_KV_HEADS, HEAD_DIM), F32),
        jax.ShapeDtypeStruct((t_all, IDX_DIM), F32),
        jax.ShapeDtypeStruct((t_all, D_MODEL), BF16),
        jax.ShapeDtypeStruct((t_all, D_MODEL), BF16),
    ]
    kv_rows = pl.BlockSpec((tm * N_KV_HEADS, HEAD_DIM), lambda b, t: (b * nt + t, 0))
    out_specs = [kv_rows, kv_rows] + [tok(s.shape[1]) for s in out_shapes[2:]]
    scratch = []
    if is_prompt:
        assert tm == KEY_CHUNK and tm % Q_BLOCK == 0
        qb = tm // Q_BLOCK
        nblk = t_all // Q_BLOCK
        per_qblock = lambda *shape: pl.BlockSpec((qb,) + shape, lambda b, t: (b * nt + t,) + (0,) * len(shape))
        out_shapes += [
            jax.ShapeDtypeStruct((nblk, N_KV_HEADS, HEAD_DIM, GROUP * Q_BLOCK), BF16),
            jax.ShapeDtypeStruct((t_all, kvw), BF16),
            jax.ShapeDtypeStruct((t_all // tm, kvw, tm), BF16),
            jax.ShapeDtypeStruct((nblk, IDX_HEADS * IDX_DIM, Q_BLOCK), BF16),
            jax.ShapeDtypeStruct((t_all, IDX_DIM), BF16),
            jax.ShapeDtypeStruct((nblk, SUBLANES, Q_BLOCK), F32),
            jax.ShapeDtypeStruct((batch, CONV_WIDTH - 1, D_CONV), F32),
        ]
        out_specs += [
            per_qblock(N_KV_HEADS, HEAD_DIM, GROUP * Q_BLOCK), tok(kvw),
            pl.BlockSpec((1, kvw, tm), lambda b, t: (b * nt + t, 0, 0)),
            per_qblock(IDX_HEADS * IDX_DIM, Q_BLOCK), tok(IDX_DIM), per_qblock(SUBLANES, Q_BLOCK),
            pl.BlockSpec((1, CONV_WIDTH - 1, D_CONV), lambda b, t: (b, 0, 0)),
        ]
        scratch.append(pltpu.VMEM((tm + SUBLANES, D_CONV), F32))
    else:
        in_specs += [tok(D_CONV), tok(D_CONV)]
        args += list(state)
        out_shapes += [
            jax.ShapeDtypeStruct((t_all, N_HEADS * HEAD_DIM), BF16),
            jax.ShapeDtypeStruct((IDX_HEADS, t_all, IDX_DIM), BF16),
            jax.ShapeDtypeStruct((t_all, IDX_HEADS), F32),
            jax.ShapeDtypeStruct((t_all, D_CONV), F32),
        ]
        out_specs += [tok(N_HEADS * HEAD_DIM),
                      pl.BlockSpec((IDX_HEADS, tm, IDX_DIM), lambda b, t: (0, b * nt + t, 0)),
                      tok(IDX_HEADS), tok(D_CONV)]
    return pl.pallas_call(
        functools.partial(_mixer_in_kernel, is_prompt, tm),
        grid=grid, in_specs=in_specs, out_specs=out_specs, out_shape=out_shapes,
        scratch_shapes=scratch,
        compiler_params=pltpu.CompilerParams(
            dimension_semantics=("arbitrary", "arbitrary"), vmem_limit_bytes=VMEM_LIMIT_BYTES),
        name="mixer_in_prompt" if is_prompt else "mixer_in_sample",
    )(*args)


def _prompt_attn_kernel(top_k, qt_ref, iqt_ref, iwt_ref, k_ref, vt_ref, ik_ref, low_ref, o_ref,
                        sc_ref, planes_ref, m_ref, acc_ref, p_ref):
    i = pl.program_id(1)
    nch = i // (KEY_CHUNK // Q_BLOCK) + 1
    qpos = i * Q_BLOCK + lax.broadcasted_iota(I32, (1, Q_BLOCK), 1)

    @pl.when(i == 0)
    def _():
        planes_ref[...] = jnp.zeros(planes_ref.shape, I32)

    iw = iwt_ref[0]
    iqt = iqt_ref[0]
    iq_pairs = [jnp.concatenate([iqt[h * IDX_DIM:(h + 1) * IDX_DIM] for h in (2 * pr, 2 * pr + 1)], axis=1)
                for pr in range(IDX_HEADS // 2)]

    def score_chunk(c):
        off = pl.multiple_of(c * KEY_CHUNK, KEY_CHUNK)
        ikc = ik_ref[pl.ds(off, KEY_CHUNK), :]
        acc = jnp.zeros((KEY_CHUNK, Q_BLOCK), F32)
        for pr in range(IDX_HEADS // 2):
            s2 = _dot(ikc, iq_pairs[pr])
            for e in range(2):
                h = 2 * pr + e
                acc = acc + jnp.maximum(s2[:, e * Q_BLOCK:(e + 1) * Q_BLOCK], 0.0) * iw[h:h + 1]
        kpos = off + lax.broadcasted_iota(I32, (KEY_CHUNK, Q_BLOCK), 0)
        sc_ref[c] = jnp.where(kpos <= qpos, acc, -jnp.inf)

    def plane_chunk(c):
        keys = _float_to_key(sc_ref[c])
        for wd in range(PLANE_ROWS // SUBLANES):
            words = [keys[(wd * 32 + j) * SUBLANES:(wd * 32 + j + 1) * SUBLANES] for j in range(32)]
            row = pl.multiple_of(c * PLANE_ROWS + wd * SUBLANES, SUBLANES)
            for b, plane in enumerate(_bit_transpose32(words)):
                planes_ref[b, pl.ds(row, SUBLANES), :] = plane

    score_chunk(0)

    def score_step(c, carry):
        plane_chunk(c - 1)
        score_chunk(c)
        return carry

    lax.fori_loop(1, nch, score_step, 0)
    plane_chunk(nch - 1)

    def count(cmps, t):
        def body(c, accs):
            blk = sc_ref[c]
            hits = [jnp.where(cmp(blk, t), 1.0, 0.0) for cmp in cmps]
            return tuple(acc + jnp.sum(hit.reshape(KEY_CHUNK // COUNT_ROWS, COUNT_ROWS, Q_BLOCK), axis=0)
                         for acc, hit in zip(accs, hits))

        accs = lax.fori_loop(0, nch, body, tuple(jnp.zeros((COUNT_ROWS, Q_BLOCK), F32) for _ in cmps))
        return tuple(jnp.sum(acc, axis=0, keepdims=True) for acc in accs)

    ge = lambda a, b: a >= b
    gt = lambda a, b: a > b
    few = qpos < top_k
    def select(rows):
        plane_row = lax.broadcasted_iota(I32, (rows, Q_BLOCK), 0)
        return _radix_select(planes_ref, jnp.where(plane_row < nch * PLANE_ROWS, -1, 0), top_k)

    half_rows = planes_ref.shape[1] // 2
    t = _key_to_float(lax.cond(nch * PLANE_ROWS <= half_rows,
                               lambda: select(half_rows), lambda: select(2 * half_rows)))
    n_gt, n_ge = count((gt,), t)[0], count((ge,), t)[0]
    is_kth = few | ((n_gt < float(top_k)) & (n_ge >= float(top_k)))

    def recount():
        t2 = _kth_largest(lambda cand: count((ge,), cand)[0], top_k, (1, Q_BLOCK))
        return t2, count((gt,), t2)[0]

    t, n_gt = lax.cond(jnp.min(jnp.where(is_kth, 1.0, 0.0)) > 0.0, lambda: (t, n_gt), recount)
    t = jnp.where(few, -jnp.inf, t)
    need = jnp.where(few, 0.0, float(top_k) - n_gt)

    m_ref[...] = jnp.full(m_ref.shape, NEG_INF, F32)
    acc_ref[...] = jnp.zeros(acc_ref.shape, F32)
    ones_rows = jnp.ones((MXU_ROWS, KEY_CHUNK), BF16)

    heads = [(g, slice(hh * Q_BLOCK, (hh + 1) * Q_BLOCK)) for g in range(N_KV_HEADS) for hh in range(GROUP)]

    def chunk_bias(c, n_eq):
        blk = sc_ref[c]
        eq = blk == t
        eqf = jnp.where(eq, 1.0, 0.0)
        eqb = eqf.astype(BF16)
        before = []
        for blk_i in range(KEY_CHUNK // TIE_BLOCK):
            rows = slice(blk_i * TIE_BLOCK, (blk_i + 1) * TIE_BLOCK)
            before.append(n_eq + _dot(low_ref[...], eqb[rows]))
            n_eq = n_eq + jnp.sum(eqf[rows], axis=0, keepdims=True)
        sel = (blk > t) | (eq & (jnp.concatenate(before, axis=0) < need))
        return jnp.where(sel, 0.0, NEG_INF), n_eq

    def head_logits(c, g, lanes, bias):
        off = pl.multiple_of(c * KEY_CHUNK, KEY_CHUNK)
        kc = k_ref[pl.ds(off, KEY_CHUNK), g * HEAD_DIM:(g + 1) * HEAD_DIM]
        return _dot(kc, qt_ref[0, g, :, lanes]) + bias

    def reweigh_chunk(c, bias):
        alphas = []
        for g, lanes in heads:
            sh = head_logits(c, g, lanes, bias)
            m_old = m_ref[g, :, lanes]
            m_new = jnp.maximum(m_old, jnp.max(sh, axis=0, keepdims=True))
            alphas.append(jnp.exp2(m_old - m_new))
            p_ref[g, :, lanes] = jnp.exp2(sh - m_new).astype(BF16)
            m_ref[g, :, lanes] = m_new
        return tuple(alphas)

    def weigh_chunk(c, n_eq):
        bias, n_eq = chunk_bias(c, n_eq)
        excess = jnp.full((1, Q_BLOCK), -jnp.inf, F32)
        for g, lanes in heads:
            sh = head_logits(c, g, lanes, bias)
            m_cur = m_ref[g, :, lanes]
            p_ref[g, :, lanes] = jnp.exp2(sh - m_cur).astype(BF16)
            excess = jnp.maximum(excess, jnp.max(sh, axis=0, keepdims=True) - m_cur)
        keep = tuple(jnp.ones((1, Q_BLOCK), F32) for _ in heads)
        alphas = lax.cond(jnp.max(excess) > STABILISER_SLACK, lambda: reweigh_chunk(c, bias), lambda: keep)
        return n_eq, alphas

    def accumulate_chunk(c, alphas):
        for g in range(N_KV_HEADS):
            vext = jnp.concatenate([vt_ref[c, g * HEAD_DIM:(g + 1) * HEAD_DIM, :], ones_rows], axis=0)
            alpha = jnp.concatenate(alphas[g * GROUP:(g + 1) * GROUP], axis=1)
            acc_ref[g] = alpha * acc_ref[g] + _dot(vext, p_ref[g])

    def attn_step(c, carry):
        n_eq, alphas = carry
        accumulate_chunk(c - 1, alphas)
        return weigh_chunk(c, n_eq)

    bias0, n_eq0 = chunk_bias(0, jnp.zeros((1, Q_BLOCK), F32))
    _, alphas = lax.fori_loop(1, nch, attn_step, (n_eq0, reweigh_chunk(0, bias0)))
    accumulate_chunk(nch - 1, alphas)

    for g in range(N_KV_HEADS):
        acc = acc_ref[g]
        o = acc[:HEAD_DIM] / acc[HEAD_DIM:HEAD_DIM + 1]
        for hh in range(GROUP):
            col = (g * GROUP + hh) * HEAD_DIM
            o_ref[:, col:col + HEAD_DIM] = o[:, hh * Q_BLOCK:(hh + 1) * Q_BLOCK].T.astype(o_ref.dtype)


def _prompt_attn(qt, iqt, iwt, kb, vt, ikb, *, batch, seq):
    nqb = seq // Q_BLOCK
    nch = seq // KEY_CHUNK
    kvw = N_KV_HEADS * HEAD_DIM
    top_k = min(TOP_K_MAX, seq // 4)
    low = jnp.tril(jnp.ones((TIE_BLOCK, TIE_BLOCK), BF16), k=-1)
    per_qblock = lambda *shape: pl.BlockSpec((1,) + shape, lambda b, i: (b * nqb + i,) + (0,) * len(shape))
    per_batch = lambda width: pl.BlockSpec((seq, width), lambda b, i: (b, 0))
    return pl.pallas_call(
        functools.partial(_prompt_attn_kernel, top_k),
        grid=(batch, nqb),
        in_specs=[per_qblock(N_KV_HEADS, HEAD_DIM, GROUP * Q_BLOCK),
                  per_qblock(IDX_HEADS * IDX_DIM, Q_BLOCK), per_qblock(SUBLANES, Q_BLOCK),
                  per_batch(kvw), pl.BlockSpec((nch, kvw, KEY_CHUNK), lambda b, i: (b, 0, 0)),
                  per_batch(IDX_DIM), _const_spec((TIE_BLOCK, TIE_BLOCK))],
        out_specs=pl.BlockSpec((Q_BLOCK, N_HEADS * HEAD_DIM), lambda b, i: (b * nqb + i, 0)),
        out_shape=jax.ShapeDtypeStruct((batch * seq, N_HEADS * HEAD_DIM), BF16),
        scratch_shapes=[
            pltpu.VMEM((nch, KEY_CHUNK, Q_BLOCK), F32),
            pltpu.VMEM((32, nch * PLANE_ROWS, Q_BLOCK), I32),
            pltpu.VMEM((N_KV_HEADS, 1, GROUP * Q_BLOCK), F32),
            pltpu.VMEM((N_KV_HEADS, HEAD_DIM + MXU_ROWS, GROUP * Q_BLOCK), F32),
            pltpu.VMEM((N_KV_HEADS, KEY_CHUNK, GROUP * Q_BLOCK), BF16),
        ],
        compiler_params=pltpu.CompilerParams(
            dimension_semantics=("arbitrary", "arbitrary"), vmem_limit_bytes=VMEM_LIMIT_BYTES),
        name="prompt_attn",
    )(qt, iqt, iwt, kb, vt, ikb, low)


def _sample_attn_kernel(n_pages, page, top_k, rb, pt_ref, q_ref, iq_ref, iw_ref, kn_ref, vn_ref, ikn_ref,
                        ck_hbm, cv_hbm, cik_hbm, tri_ref, low_ref, o_ref,
                        kbuf, vbuf, ikbuf, sems, sc_ref):
    b = pl.program_id(0)
    nb = pl.num_programs(0)
    past = n_pages * page

    def page_copies(step, slot):
        copies = []
        for r in range(rb):
            for p in range(n_pages):
                phys = pt_ref[step * rb + r, p]
                rows = pl.ds(p * page * N_KV_HEADS, page * N_KV_HEADS)
                lanes = pl.ds(p * page, page)
                copies.append(pltpu.make_async_copy(ck_hbm.at[phys], kbuf.at[slot, r, rows], sems.at[0, slot]))
                copies.append(pltpu.make_async_copy(cv_hbm.at[phys], vbuf.at[slot, r, rows], sems.at[1, slot]))
                copies.append(
                    pltpu.make_async_copy(cik_hbm.at[phys], ikbuf.at[slot, r, :, lanes], sems.at[2, slot]))
        return copies

    slot = b % 2

    @pl.when(b == 0)
    def _():
        for c in page_copies(0, 0):
            c.start()

    @pl.when(b + 1 < nb)
    def _():
        for c in page_copies(b + 1, 1 - slot):
            c.start()

    for c in page_copies(b, slot):
        c.wait()

    sc_rows, sc_news = [], []
    for r in range(rb):
        iq = iq_ref[r].astype(BF16)
        iw = iw_ref[r]
        sidx = _dot(iq, ikbuf[slot, r].astype(BF16))
        sc_row = jnp.sum(jnp.maximum(sidx, 0.0) * iw, axis=0, keepdims=True)
        for p in range(n_pages):
            sc_ref[r, p:p + 1, :] = sc_row[:, p * page:(p + 1) * page]
        ikn = ikn_ref[r].astype(BF16).astype(F32)
        s_new = jnp.sum(iq.astype(F32) * ikn, axis=1, keepdims=True)
        sc_rows.append(sc_row)
        sc_news.append(jnp.sum(jnp.maximum(s_new, 0.0) * iw, axis=0, keepdims=True))

    def count_ge(r, cand):
        hits = jnp.where(sc_rows[r] >= cand, 1.0, 0.0)
        return jnp.sum(hits, axis=1, keepdims=True) + jnp.where(sc_news[r] >= cand, 1.0, 0.0)

    ts = _kth_largest_by_digits([functools.partial(count_ge, r) for r in range(rb)], top_k,
                                SAMPLE_SEARCH_BITS)

    def total(x):
        return jnp.sum(jnp.sum(x, axis=1, keepdims=True), axis=0, keepdims=True)

    for r in range(rb):
        t, sc, sc_new = ts[r], sc_ref[r], sc_news[r]
        n_gt = total(jnp.where(sc > t, 1.0, 0.0)) + jnp.where(sc_new > t, 1.0, 0.0)
        need = float(top_k) - n_gt
        eq = sc == t
        eqf = jnp.where(eq, 1.0, 0.0)
        in_row = _dot(eqf.astype(BF16), tri_ref[...])
        row_tot = jnp.broadcast_to(jnp.sum(eqf, axis=1, keepdims=True), sc.shape)
        rows_before = _dot(low_ref[...], row_tot.astype(BF16))
        sel = (sc > t) | (eq & (in_row + rows_before < need))
        sel_new = (sc_new > t) | ((sc_new == t) & (total(eqf) < need))
        bias = jnp.where(sel, 0.0, NEG_INF)
        bias_row = jnp.concatenate([bias[p:p + 1, :] for p in range(n_pages)], axis=1)
        bias_new = jnp.where(sel_new, 0.0, NEG_INF)

        q = q_ref[r].astype(BF16)
        qf = q.astype(F32)
        for g in range(N_KV_HEADS):
            head_rows = pl.ds(g, past, stride=N_KV_HEADS)
            kg = kbuf[slot, r, head_rows, :].astype(BF16)
            vg = vbuf[slot, r, head_rows, :].astype(BF16)
            kn = kn_ref[r, g:g + 1, :].astype(BF16).astype(F32)
            vn = vn_ref[r, g:g + 1, :].astype(BF16).astype(F32)
            s = _dot_nt(q, kg) + bias_row
            sn = jnp.sum(qf * kn, axis=1, keepdims=True) + bias_new
            m = jnp.maximum(jnp.max(s, axis=1, keepdims=True), sn)
            p = jnp.exp(s - m)
            pn = jnp.exp(sn - m)
            l = jnp.sum(p, axis=1, keepdims=True) + pn
            o = (_dot(p.astype(BF16), vg) + pn * vn) / l
            o_ref[r, g * GROUP:(g + 1) * GROUP, :] = o[g * GROUP:(g + 1) * GROUP]


def _sample_attn(page_table, q, iq, iw, k_new, v_new, ik_new, cache_k, cache_v, cache_ik):
    nreq, n_pages = page_table.shape
    n_phys, page = cache_k.shape[0], cache_k.shape[1]
    kvw = N_KV_HEADS * HEAD_DIM
    top_k = min(TOP_K_MAX, (n_pages * page + 1) // 4)
    assert top_k < n_pages * page + 1
    ck = cache_k.reshape(n_phys, page * N_KV_HEADS, HEAD_DIM)
    cv = cache_v.reshape(n_phys, page * N_KV_HEADS, HEAD_DIM)
    cik = jnp.swapaxes(cache_ik, 1, 2)
    tri = jnp.triu(jnp.ones((page, page), BF16), k=1)
    low = jnp.tril(jnp.ones((n_pages, n_pages), BF16), k=-1)
    rb = SAMPLE_REQS
    assert nreq % rb == 0
    pad_rows = lambda a: jnp.pad(a.astype(F32), ((0, 0), (0, MXU_ROWS - a.shape[1]), (0, 0)))
    per_req = lambda *shape: pl.BlockSpec((rb,) + shape, lambda b, pt: (b,) + (0,) * len(shape))
    const = lambda *shape: pl.BlockSpec(shape, lambda b, pt: (0,) * len(shape))
    any_spec = pl.BlockSpec(memory_space=pl.ANY)
    grid_spec = pltpu.PrefetchScalarGridSpec(
        num_scalar_prefetch=1,
        grid=(nreq // rb,),
        in_specs=[per_req(MXU_ROWS, HEAD_DIM), per_req(MXU_ROWS, IDX_DIM), per_req(MXU_ROWS, 1),
                  per_req(N_KV_HEADS, HEAD_DIM), per_req(N_KV_HEADS, HEAD_DIM), per_req(1, IDX_DIM),
                  any_spec, any_spec, any_spec, const(page, page), const(n_pages, n_pages)],
        out_specs=per_req(N_HEADS, HEAD_DIM),
        scratch_shapes=[
            pltpu.VMEM((2, rb, n_pages * page * N_KV_HEADS, HEAD_DIM), F32),
            pltpu.VMEM((2, rb, n_pages * page * N_KV_HEADS, HEAD_DIM), F32),
            pltpu.VMEM((2, rb, IDX_DIM, n_pages * page), F32),
            pltpu.SemaphoreType.DMA((3, 2)),
            pltpu.VMEM((rb, n_pages, page), F32),
        ],
    )
    out = pl.pallas_call(
        functools.partial(_sample_attn_kernel, n_pages, page, top_k, rb),
        grid_spec=grid_spec,
        out_shape=jax.ShapeDtypeStruct((nreq, N_HEADS, HEAD_DIM), F32),
        compiler_params=pltpu.CompilerParams(
            dimension_semantics=("arbitrary",), vmem_limit_bytes=VMEM_LIMIT_BYTES),
        name="sample_attn",
    )(page_table,
      pad_rows(q.reshape(nreq, N_HEADS, HEAD_DIM)),
      pad_rows(jnp.transpose(iq, (1, 0, 2))),
      pad_rows(iw.reshape(nreq, IDX_HEADS, 1)),
      k_new.reshape(nreq, N_KV_HEADS, HEAD_DIM), v_new.reshape(nreq, N_KV_HEADS, HEAD_DIM),
      ik_new.reshape(nreq, 1, IDX_DIM),
      ck, cv, cik, tri, low)
    return out.reshape(nreq, N_HEADS * HEAD_DIM)


def _merge_stage(x, attn, oa, sgb, wao_ref, wo_ref, gmem_ref, wmq_ref):
    out_b = _dot(attn, wao_ref[...])
    merged = oa.astype(F32) + sgb.astype(F32) * out_b
    x1 = x + _dot(merged.astype(BF16), wo_ref[...])
    hm = _rms(x1, gmem_ref[...]).astype(BF16)
    return x1, _dot(hm, wmq_ref[...]) * MEM_SCALE


def _ffn_stage(x1, mem_o, wmo_ref, gffn_ref, wg_ref, wu_ref, wd_ref, gfin_ref):
    x2 = x1 + _dot(mem_o.astype(BF16), wmo_ref[...])
    hf = _rms(x2, gffn_ref[...]).astype(BF16)
    acc = jnp.zeros_like(x2)
    for c in range(D_FF // FF_CHUNK):
        cols = slice(c * FF_CHUNK, (c + 1) * FF_CHUNK)
        gate = _dot(hf, wg_ref[:, cols])
        f = gate * _sigmoid(gate) * _dot(hf, wu_ref[:, cols])
        acc = acc + _dot(f.astype(BF16), wd_ref[cols, :])
    return _rms(x2 + acc, gfin_ref[...])


def _post_prompt_kernel(x_ref, attn_ref, oa_ref, sgb_ref, mk_ref, mv_ref, wao_ref, wo_ref, gmem_ref,
                        wmq_ref, wmo_ref, gffn_ref, wg_ref, wu_ref, wd_ref, gfin_ref, y_ref, mo_ref):
    x1, qm = _merge_stage(x_ref[...], attn_ref[...], oa_ref[...], sgb_ref[...],
                          wao_ref, wo_ref, gmem_ref, wmq_ref)
    for h in range(MEM_HEADS):
        cols = slice(h * MEM_HEAD_DIM, (h + 1) * MEM_HEAD_DIM)
        s = _dot_nt(qm[:, cols].astype(BF16), mk_ref[:, cols])
        p = jnp.exp(s - jnp.max(s, axis=1, keepdims=True))
        o = _dot(p.astype(BF16), mv_ref[:, cols]) / jnp.sum(p, axis=1, keepdims=True)
        mo_ref[:, cols] = o.astype(BF16)
    y_ref[...] = _ffn_stage(x1, mo_ref[...], wmo_ref, gffn_ref, wg_ref, wu_ref, wd_ref, gfin_ref)


def _post_prompt(x, attn, oa, sgb, mk, mv, w, *, batch, seq, tm):
    nt = seq // tm
    n_mem = mk.shape[0] // batch
    tok = lambda width: pl.BlockSpec((tm, width), lambda b, t: (b * nt + t, 0))
    mem = pl.BlockSpec((n_mem, D_MODEL), lambda b, t: (b, 0))
    sq = _const_spec((D_MODEL, D_MODEL))
    gain = _const_spec((1, D_MODEL))
    return pl.pallas_call(
        _post_prompt_kernel,
        grid=(batch, nt),
        in_specs=[tok(D_MODEL), tok(D_MODEL), tok(D_MODEL), tok(D_MODEL), mem, mem,
                  sq, sq, gain, sq, sq, gain,
                  _const_spec((D_MODEL, D_FF)), _const_spec((D_MODEL, D_FF)), _const_spec((D_FF, D_MODEL)),
                  gain],
        out_specs=tok(D_MODEL),
        out_shape=jax.ShapeDtypeStruct((batch * seq, D_MODEL), F32),
        scratch_shapes=[pltpu.VMEM((tm, D_MODEL), BF16)],
        compiler_params=pltpu.CompilerParams(
            dimension_semantics=("arbitrary", "arbitrary"), vmem_limit_bytes=VMEM_LIMIT_BYTES),
        name="post_prompt",
    )(x, attn, oa, sgb, mk, mv, w["wao"], w["wo"], w["gmem"], w["wmq"], w["wmo"], w["gffn"],
      w["wg"], w["wu"], w["wd"], w["gfin"])


def _merge_sample_kernel(x_ref, attn_ref, oa_ref, sgb_ref, wao_ref, wo_ref, gmem_ref, wmq_ref,
                         x1_ref, qm_ref):
    x1, qm = _merge_stage(x_ref[...], attn_ref[...].astype(BF16), oa_ref[...], sgb_ref[...],
                          wao_ref, wo_ref, gmem_ref, wmq_ref)
    x1_ref[...] = x1
    qm_ref[...] = qm


def _mem_sample_kernel(rb, q_ref, mk_ref, mv_ref, o_ref):
    for r in range(rb):
        s = jnp.sum(mk_ref[r] * q_ref[r][None], axis=2, keepdims=True)
        p = jnp.exp(s - jnp.max(s, axis=0, keepdims=True))
        o_ref[r] = jnp.sum(p * mv_ref[r], axis=0) / jnp.sum(p, axis=0)


def _ffn_sample_kernel(x1_ref, mo_ref, wmo_ref, gffn_ref, wg_ref, wu_ref, wd_ref, gfin_ref, y_ref):
    y_ref[...] = _ffn_stage(x1_ref[...], mo_ref[...], wmo_ref, gffn_ref, wg_ref, wu_ref, wd_ref, gfin_ref)


def _post_sample(x, attn, oa, sgb, mem_k, mem_v, w):
    n = x.shape[0]
    n_mem = mem_k.shape[1]
    full = lambda *shape: _const_spec(shape)
    act = full(n, D_MODEL)
    sq = full(D_MODEL, D_MODEL)
    gain = full(1, D_MODEL)
    params = pltpu.CompilerParams(dimension_semantics=("arbitrary",), vmem_limit_bytes=VMEM_LIMIT_BYTES)
    x1, qm = pl.pallas_call(
        _merge_sample_kernel, grid=(1,),
        in_specs=[act, act, act, act, sq, sq, gain, sq], out_specs=[act, act],
        out_shape=[jax.ShapeDtypeStruct((n, D_MODEL), F32)] * 2,
        compiler_params=params, name="merge_sample",
    )(x, attn, oa, sgb, w["wao"], w["wo"], w["gmem"], w["wmq"])

    rb = 4
    heads = pl.BlockSpec((rb, MEM_HEADS, MEM_HEAD_DIM), lambda i: (i, 0, 0))
    mem = pl.BlockSpec((rb, n_mem, MEM_HEADS, MEM_HEAD_DIM), lambda i: (i, 0, 0, 0))
    mem_o = pl.pallas_call(
        functools.partial(_mem_sample_kernel, rb), grid=(n // rb,),
        in_specs=[heads, mem, mem], out_specs=heads,
        out_shape=jax.ShapeDtypeStruct((n, MEM_HEADS, MEM_HEAD_DIM), F32),
        compiler_params=params, name="mem_sample",
    )(qm.reshape(n, MEM_HEADS, MEM_HEAD_DIM), mem_k, mem_v)

    return pl.pallas_call(
        _ffn_sample_kernel, grid=(1,),
        in_specs=[act, act, sq, gain, full(D_MODEL, D_FF), full(D_MODEL, D_FF), full(D_FF, D_MODEL), gain],
        out_specs=act, out_shape=jax.ShapeDtypeStruct((n, D_MODEL), F32),
        compiler_params=params, name="ffn_sample",
    )(x1, mem_o.reshape(n, D_MODEL), w["wmo"], w["gffn"], w["wg"], w["wu"], w["wd"], w["gfin"])


def _memory_kv_kernel(mem_ref, g_ref, wk_ref, wv_ref, kf_ref, vf_ref, kb_ref, vb_ref):
    m = _rms(mem_ref[...], g_ref[...]).astype(BF16)
    k = _dot(m, wk_ref[...])
    v = _dot(m, wv_ref[...])
    kf_ref[...] = k
    vf_ref[...] = v
    kb_ref[...] = k.astype(BF16)
    vb_ref[...] = v.astype(BF16)


def _memory_kv(mem, g, wk, wv):
    n = mem.shape[0]
    full = lambda *shape: _const_spec(shape)
    act = full(n, D_MODEL)
    sq = full(D_MODEL, D_MODEL)
    return pl.pallas_call(
        _memory_kv_kernel, grid=(1,),
        in_specs=[act, full(1, D_MODEL), sq, sq], out_specs=[act] * 4,
        out_shape=[jax.ShapeDtypeStruct((n, D_MODEL), F32)] * 2 + [jax.ShapeDtypeStruct((n, D_MODEL), BF16)] * 2,
        compiler_params=pltpu.CompilerParams(
            dimension_semantics=("arbitrary",), vmem_limit_bytes=VMEM_LIMIT_BYTES),
        name="memory_kv",
    )(mem, g, wk, wv)


def _prep_w_in(w_in):
    w_t = w_in.T
    main = jnp.pad(w_t[:IDX_RAW_END].astype(BF16), ((0, W_COLS - IDX_RAW_END), (0, 0)))
    return main, w_t[IDX_RAW_END:].astype(BF16)


def kernel(x_prompt, x_sample, mem_prompt, cache_k, cache_v, cache_idx_k, cache_mem_k, cache_mem_v, state_conv, page_table, g_mix, w_in, conv_w, w_conv_out, w_attn_out, w_o, g_mem, g_mem_kv, w_mq, w_mk, w_mv, w_mo, g_ffn, w_gate, w_up, w_down, g_final):
    depth = w_in.shape[0]
    assert depth == 1, "single-layer step"
    batch, seq, _ = x_prompt.shape
    nreq, dec_seq, _ = x_sample.shape
    assert dec_seq == 1
    n_mem = mem_prompt.shape[1]
    l = 0
    bf = lambda a: a.astype(BF16)
    row = lambda a: a.reshape(1, -1)

    w_main, w_gates = _prep_w_in(w_in[l])
    wco = bf(w_conv_out[l])
    w = dict(wao=bf(w_attn_out[l]), wo=bf(w_o[l]), gmem=row(g_mem[l]), wmq=bf(w_mq[l]), wmo=bf(w_mo[l]),
             gffn=row(g_ffn[l]), wg=bf(w_gate[l]), wu=bf(w_up[l]), wd=bf(w_down[l]), gfin=row(g_final))

    xp = x_prompt.reshape(batch * seq, D_MODEL)
    mkf, mvf, mkb, mvb = _memory_kv(mem_prompt.reshape(batch * n_mem, D_MODEL), row(g_mem_kv[l]),
                                    bf(w_mk[l]), bf(w_mv[l]))
    (kf, vf, ikf, oa, sgb, qt, kb, vt, iqt, ikb, iwt, conv_p) = _mixer_in(
        xp, row(g_mix[l]), w_main, w_gates, conv_w[l], wco, batch=batch, seq=seq, tm=KEY_CHUNK)
    attn = _prompt_attn(qt, iqt, iwt, kb, vt, ikb, batch=batch, seq=seq)
    yp = _post_prompt(xp, attn, oa, sgb, mkb, mvb, w, batch=batch, seq=seq, tm=POST_TILE)

    xs = x_sample.reshape(nreq, D_MODEL)
    st = state_conv[l]
    (kf_s, vf_s, ikf_s, oa_s, sgb_s, q_s, iq_s, iw_s, u_s) = _mixer_in(
        xs, row(g_mix[l]), w_main, w_gates, conv_w[l], wco, batch=nreq, seq=1, tm=nreq, state=(st[:, 0], st[:, 1]))
    attn_s = _sample_attn(page_table, q_s, iq_s, iw_s, kf_s, vf_s, ikf_s,
                          cache_k[l], cache_v[l], cache_idx_k[l])
    ys = _post_sample(xs, attn_s, oa_s, sgb_s, cache_mem_k[l], cache_mem_v[l], w)

    return (
        yp.reshape(batch, seq, D_MODEL),
        ys.reshape(nreq, 1, D_MODEL),
        kf.reshape(1, batch, seq, N_KV_HEADS, HEAD_DIM),
        vf.reshape(1, batch, seq, N_KV_HEADS, HEAD_DIM),
        ikf.reshape(1, batch, seq, IDX_DIM),
        conv_p.reshape(1, batch, CONV_WIDTH - 1, D_CONV),
        mkf.reshape(1, batch, n_mem, MEM_HEADS, MEM_HEAD_DIM),
        mvf.reshape(1, batch, n_mem, MEM_HEADS, MEM_HEAD_DIM),
        kf_s.reshape(1, nreq, 1, N_KV_HEADS, HEAD_DIM),
        vf_s.reshape(1, nreq, 1, N_KV_HEADS, HEAD_DIM),
        ikf_s.reshape(1, nreq, 1, IDX_DIM),
        jnp.stack([st[:, 1], u_s], axis=1).reshape(1, nreq, CONV_WIDTH - 1, D_CONV),
    )
```

```python
import functools

import jax
import jax.numpy as jnp
from jax import lax
from jax.experimental import pallas as pl
from jax.experimental.pallas import tpu as pltpu

F32 = jnp.float32
BF16 = jnp.bfloat16
I32 = jnp.int32

D_MODEL = 1024
D_CONV = 1024
CONV_WIDTH = 3
N_HEADS = 8
HEAD_DIM = 128
N_KV_HEADS = 2
GROUP = N_HEADS // N_KV_HEADS
IDX_HEADS = 4
IDX_DIM = 64
TOP_K_MAX = 256
MEM_HEADS = 4
MEM_HEAD_DIM = D_MODEL // MEM_HEADS
D_FF = 2816
EPS = 1e-6
NEG_INF = -1e30
IDX_SCALE = (IDX_HEADS * IDX_DIM) ** -0.5
ATTN_SCALE = HEAD_DIM ** -0.5
MEM_SCALE = MEM_HEAD_DIM ** -0.5
LOG2_E = 1.4426950408889634

LANES = 128
SUBLANES = 8
MXU_ROWS = 16
VMEM_LIMIT_BYTES = 56 * 1024 * 1024

COL_CIN = 0
COL_CB = COL_CIN + D_CONV
COL_CC = COL_CB + D_CONV
COL_Q = COL_CC + D_CONV
COL_K = COL_Q + N_HEADS * HEAD_DIM
COL_V = COL_K + N_KV_HEADS * HEAD_DIM
COL_IQ = COL_V + N_KV_HEADS * HEAD_DIM
COL_IK = COL_IQ + IDX_HEADS * IDX_DIM
COL_IW = COL_IK + IDX_DIM
IDX_RAW_END = COL_IW + IDX_HEADS
W_COLS = -(-IDX_RAW_END // LANES) * LANES

Q_BLOCK = 256
KEY_CHUNK = 512
COUNT_ROWS = 64
KEY_BITS = 32
PLANE_ROWS = KEY_CHUNK // KEY_BITS
TIE_BLOCK = 128
SAMPLE_SEARCH_BITS = 4
SAMPLE_REQS = 4
MEM_REQS = 4
FF_CHUNK = D_FF // 2
POST_TILE = 512
STABILISER_SLACK = 64.0
INT_MIN = -2 ** 31


def _rms(x, g):
    return x * lax.rsqrt(jnp.mean(x * x, axis=-1, keepdims=True) + EPS) * g


def _dot(a, b):
    return jnp.dot(a, b, preferred_element_type=F32)


def _dot_nt(a, b):
    return lax.dot_general(a, b, (((1,), (1,)), ((), ())), preferred_element_type=F32)


def _sigmoid(x):
    return 1.0 / (1.0 + jnp.exp(-x))


def _key_to_float(ukey):
    skey = ukey ^ INT_MIN
    bits = jnp.where(skey < 0, skey ^ 0x7FFFFFFF, skey)
    return lax.bitcast_convert_type(bits, F32)


def _float_to_key(x):
    bits = lax.bitcast_convert_type(x, I32)
    return bits ^ (lax.shift_right_arithmetic(bits, jnp.full_like(bits, 31)) & 0x7FFFFFFF) ^ INT_MIN


def _kth_largest(count_ge, top_k, shape):
    def body(b, ukey):
        cand = ukey | jnp.left_shift(jnp.int32(1), KEY_BITS - 1 - b)
        ok = count_ge(_key_to_float(cand)) >= float(top_k)
        return jnp.where(ok, cand, ukey)

    ukey = lax.fori_loop(0, KEY_BITS, body, jnp.zeros(shape, I32))
    return _key_to_float(ukey)


def _kth_largest_by_digits(count_ge_fns, top_k, bits):
    digits = jnp.minimum(lax.broadcasted_iota(I32, (2 ** bits, 1), 0) + 1, 2 ** bits - 1)
    digits_f = digits.astype(F32)

    def body(rd, ukeys):
        shift = KEY_BITS - bits * (rd + 1)
        out = []
        for count_ge, ukey in zip(count_ge_fns, ukeys):
            cand = ukey | jnp.left_shift(digits, shift)
            ok = count_ge(_key_to_float(cand)) >= float(top_k)
            digit = jnp.max(jnp.where(ok, digits_f, 0.0), axis=0, keepdims=True).astype(I32)
            out.append(ukey | jnp.left_shift(digit, shift))
        return tuple(out)

    ukeys = lax.fori_loop(0, KEY_BITS // bits, body, tuple(jnp.zeros((1, 1), I32) for _ in count_ge_fns))
    return [_key_to_float(u) for u in ukeys]


def _bit_transpose32(words):
    a = list(words)
    j, m = 16, 0x0000FFFF
    while j:
        k = 0
        while k < 32:
            t = (a[k] ^ lax.shift_right_logical(a[k + j], jnp.full_like(a[k], j))) & m
            a[k] = a[k] ^ t
            a[k + j] = a[k + j] ^ lax.shift_left(t, jnp.full_like(t, j))
            k = (k + j + 1) & ~j
        j >>= 1
        m = (m ^ (m << j)) & 0xFFFFFFFF
    return a


def _radix_select(planes_ref, live, top_k):
    rows = live.shape[0]

    def body(i, carry):
        live, ukey, n_above = carry
        hit = live & planes_ref[i, :rows]
        cnt = jnp.sum(lax.population_count(hit).astype(F32), axis=0, keepdims=True)
        take = n_above + cnt >= float(top_k)
        live = jnp.where(take, hit, live ^ hit)
        ukey = jnp.where(take, ukey | jnp.left_shift(jnp.int32(1), KEY_BITS - 1 - i), ukey)
        n_above = jnp.where(take, n_above, n_above + cnt)
        return live, ukey, n_above

    lanes = live.shape[1]
    init = (live, jnp.zeros((1, lanes), I32), jnp.zeros((1, lanes), F32))
    return lax.fori_loop(0, KEY_BITS, body, init)[1]


def _mixer_in_kernel(is_prompt, tm, *refs):
    if is_prompt:
        (x_ref, g_ref, w_ref, wgate_ref, cw_ref, wco_ref,
         kf_ref, vf_ref, ikf_ref, oa_ref, sgb_ref,
         qt_ref, kb_ref, vt_ref, iqt_ref, ikb_ref, iwt_ref, cs_ref, ubuf) = refs
    else:
        (x_ref, g_ref, w_ref, wgate_ref, cw_ref, wco_ref, s0_ref, s1_ref,
         kf_ref, vf_ref, ikf_ref, oa_ref, sgb_ref,
         q_ref, iq_ref, iw_ref, u_ref) = refs

    h = _rms(x_ref[...], g_ref[...]).astype(BF16)

    def proj(lo, hi):
        return _dot_nt(h, w_ref[lo:hi, :])

    u = proj(COL_CC, COL_CC + D_CONV) * proj(COL_CIN, COL_CIN + D_CONV)
    cw = cw_ref[...]
    if is_prompt:
        @pl.when(pl.program_id(1) == 0)
        def _():
            ubuf[0:SUBLANES, :] = jnp.zeros((SUBLANES, D_CONV), F32)

        ubuf[SUBLANES:SUBLANES + tm, :] = u
        conv = (ubuf[SUBLANES - 2:SUBLANES - 2 + tm, :] * cw[0:1]
                + ubuf[SUBLANES - 1:SUBLANES - 1 + tm, :] * cw[1:2] + u * cw[2:3])
        ubuf[0:SUBLANES, :] = ubuf[tm:tm + SUBLANES, :]
        cs_ref[0] = u[tm - (CONV_WIDTH - 1):, :]
    else:
        conv = s0_ref[...] * cw[0:1] + s1_ref[...] * cw[1:2] + u * cw[2:3]
        u_ref[...] = u

    a_in = (proj(COL_CB, COL_CB + D_CONV) * conv).astype(BF16)
    out_a = _dot(a_in, wco_ref[...])
    oa_ref[...] = (_sigmoid(_dot_nt(h, wgate_ref[:D_MODEL, :])) * out_a).astype(BF16)
    sgb_ref[...] = _sigmoid(_dot_nt(h, wgate_ref[D_MODEL:, :])).astype(BF16)

    q = proj(COL_Q, COL_K) * (ATTN_SCALE * LOG2_E if is_prompt else ATTN_SCALE)
    kv = proj(COL_K, COL_IQ)
    k = kv[:, :N_KV_HEADS * HEAD_DIM]
    v = kv[:, N_KV_HEADS * HEAD_DIM:]
    for g in range(N_KV_HEADS):
        head_rows = pl.ds(g, tm, stride=N_KV_HEADS)
        kf_ref[head_rows, :] = k[:, g * HEAD_DIM:(g + 1) * HEAD_DIM]
        vf_ref[head_rows, :] = v[:, g * HEAD_DIM:(g + 1) * HEAD_DIM]
    idx = proj(COL_IQ, W_COLS)
    iq = idx[:, :IDX_HEADS * IDX_DIM]
    ikw = idx[:, COL_IK - COL_IQ:]
    ik = ikw[:, :IDX_DIM]
    ikf_ref[...] = ik
    if is_prompt:
        kb_ref[...] = k.astype(BF16)
        ikb_ref[...] = ik.astype(BF16)
        vt_ref[0] = v.T.astype(BF16)
        for j in range(tm // Q_BLOCK):
            rows = slice(j * Q_BLOCK, (j + 1) * Q_BLOCK)
            for head in range(N_HEADS):
                g, hh = divmod(head, GROUP)
                qt_ref[j, g, :, hh * Q_BLOCK:(hh + 1) * Q_BLOCK] = (
                    q[rows, head * HEAD_DIM:(head + 1) * HEAD_DIM].T.astype(BF16))
            iqt_ref[j] = iq[rows].T.astype(BF16)
            iwt_ref[j] = ikw[rows].T[IDX_DIM:IDX_DIM + SUBLANES] * IDX_SCALE
    else:
        q_ref[...] = q.astype(BF16)
        for hd in range(IDX_HEADS):
            iq_ref[hd] = iq[:, hd * IDX_DIM:(hd + 1) * IDX_DIM].astype(BF16)
        iw_ref[...] = ikw[:, IDX_DIM:IDX_DIM + IDX_HEADS] * IDX_SCALE


def _const_spec(shape):
    nd = len(shape)
    return pl.BlockSpec(shape, lambda *_: (0,) * nd, pipeline_mode=pl.Buffered(1))


def _mixer_in(x, g_mix, w_main, w_gates, conv_w, w_conv_out, *, batch, seq, tm, state=None):
    is_prompt = state is None
    t_all = batch * seq
    nt = seq // tm if is_prompt else 1
    grid = (batch, nt) if is_prompt else (1, 1)
    tok = lambda width: pl.BlockSpec((tm, width), lambda b, t: (b * nt + t, 0))
    in_specs = [tok(D_MODEL), _const_spec((1, D_MODEL)), _const_spec((W_COLS, D_MODEL)), _const_spec((2 * D_MODEL, D_MODEL)),
                _const_spec((CONV_WIDTH, D_CONV)), _const_spec((D_CONV, D_MODEL))]
    args = [x, g_mix, w_main, w_gates, conv_w, w_conv_out]
    kvw = N_KV_HEADS * HEAD_DIM
    out_shapes = [
        jax.ShapeDtypeStruct((t_all * N_KV_HEADS, HEAD_DIM), F32),
        jax.ShapeDtypeStruct((t_all * N_KV_HEADS, HEAD_DIM), F32),
        jax.ShapeDtypeStruct((t_all, IDX_DIM), F32),
        jax.ShapeDtypeStruct((t_all, D_MODEL), BF16),
        jax.ShapeDtypeStruct((t_all, D_MODEL), BF16),
    ]
    kv_rows = pl.BlockSpec((tm * N_KV_HEADS, HEAD_DIM), lambda b, t: (b * nt + t, 0))
    out_specs = [kv_rows, kv_rows] + [tok(s.shape[1]) for s in out_shapes[2:]]
    scratch = []
    if is_prompt:
        assert tm == KEY_CHUNK and tm % Q_BLOCK == 0
        qb = tm // Q_BLOCK
        nblk = t_all // Q_BLOCK
        per_qblock = lambda *shape: pl.BlockSpec((qb,) + shape, lambda b, t: (b * nt + t,) + (0,) * len(shape))
        out_shapes += [
            jax.ShapeDtypeStruct((nblk, N_KV_HEADS, HEAD_DIM, GROUP * Q_BLOCK), BF16),
            jax.ShapeDtypeStruct((t_all, kvw), BF16),
            jax.ShapeDtypeStruct((t_all // tm, kvw, tm), BF16),
            jax.ShapeDtypeStruct((nblk, IDX_HEADS * IDX_DIM, Q_BLOCK), BF16),
            jax.ShapeDtypeStruct((t_all, IDX_DIM), BF16),
            jax.ShapeDtypeStruct((nblk, SUBLANES, Q_BLOCK), F32),
            jax.ShapeDtypeStruct((batch, CONV_WIDTH - 1, D_CONV), F32),
        ]
        out_specs += [
            per_qblock(N_KV_HEADS, HEAD_DIM, GROUP * Q_BLOCK), tok(kvw),
            pl.BlockSpec((1, kvw, tm), lambda b, t: (b * nt + t, 0, 0)),
            per_qblock(IDX_HEADS * IDX_DIM, Q_BLOCK), tok(IDX_DIM), per_qblock(SUBLANES, Q_BLOCK),
            pl.BlockSpec((1, CONV_WIDTH - 1, D_CONV), lambda b, t: (b, 0, 0)),
        ]
        scratch.append(pltpu.VMEM((tm + SUBLANES, D_CONV), F32))
    else:
        in_specs += [tok(D_CONV), tok(D_CONV)]
        args += list(state)
        out_shapes += [
            jax.ShapeDtypeStruct((t_all, N_HEADS * HEAD_DIM), BF16),
            jax.ShapeDtypeStruct((IDX_HEADS, t_all, IDX_DIM), BF16),
            jax.ShapeDtypeStruct((t_all, IDX_HEADS), F32),
            jax.ShapeDtypeStruct((t_all, D_CONV), F32),
        ]
        out_specs += [tok(N_HEADS * HEAD_DIM),
                      pl.BlockSpec((IDX_HEADS, tm, IDX_DIM), lambda b, t: (0, b * nt + t, 0)),
                      tok(IDX_HEADS), tok(D_CONV)]
    return pl.pallas_call(
        functools.partial(_mixer_in_kernel, is_prompt, tm),
        grid=grid, in_specs=in_specs, out_specs=out_specs, out_shape=out_shapes,
        scratch_shapes=scratch,
        compiler_params=pltpu.CompilerParams(
            dimension_semantics=("arbitrary", "arbitrary"), vmem_limit_bytes=VMEM_LIMIT_BYTES),
        name="mixer_in_prompt" if is_prompt else "mixer_in_sample",
    )(*args)


def _prompt_attn_kernel(top_k, qt_ref, iqt_ref, iwt_ref, k_ref, vt_ref, ik_ref, low_ref, o_ref,
                        sc_ref, planes_ref, m_ref, acc_ref, p_ref):
    i = pl.program_id(1)
    nch = i // (KEY_CHUNK // Q_BLOCK) + 1
    qpos = i * Q_BLOCK + lax.broadcasted_iota(I32, (1, Q_BLOCK), 1)

    @pl.when(i == 0)
    def _():
        planes_ref[...] = jnp.zeros(planes_ref.shape, I32)

    iw = iwt_ref[0]
    iqt = iqt_ref[0]
    iq_pairs = [jnp.concatenate([iqt[h * IDX_DIM:(h + 1) * IDX_DIM] for h in (2 * pr, 2 * pr + 1)], axis=1)
                for pr in range(IDX_HEADS // 2)]

    def score_chunk(c):
        off = pl.multiple_of(c * KEY_CHUNK, KEY_CHUNK)
        ikc = ik_ref[pl.ds(off, KEY_CHUNK), :]
        acc = jnp.zeros((KEY_CHUNK, Q_BLOCK), F32)
        for pr in range(IDX_HEADS // 2):
            s2 = _dot(ikc, iq_pairs[pr])
            for e in range(2):
                h = 2 * pr + e
                acc = acc + jnp.maximum(s2[:, e * Q_BLOCK:(e + 1) * Q_BLOCK], 0.0) * iw[h:h + 1]
        kpos = off + lax.broadcasted_iota(I32, (KEY_CHUNK, Q_BLOCK), 0)
        sc_ref[c] = jnp.where(kpos <= qpos, acc, -jnp.inf)

    def plane_chunk(c):
        keys = _float_to_key(sc_ref[c])
        for wd in range(PLANE_ROWS // SUBLANES):
            words = [keys[(wd * KEY_BITS + j) * SUBLANES:(wd * KEY_BITS + j + 1) * SUBLANES]
                     for j in range(KEY_BITS)]
            row = pl.multiple_of(c * PLANE_ROWS + wd * SUBLANES, SUBLANES)
            for b, plane in enumerate(_bit_transpose32(words)):
                planes_ref[b, pl.ds(row, SUBLANES), :] = plane

    score_chunk(0)

    def score_step(c, carry):
        plane_chunk(c - 1)
        score_chunk(c)
        return carry

    lax.fori_loop(1, nch, score_step, 0)
    plane_chunk(nch - 1)

    def count(cmps, t):
        def body(c, accs):
            blk = sc_ref[c]
            hits = [jnp.where(cmp(blk, t), 1.0, 0.0) for cmp in cmps]
            return tuple(acc + jnp.sum(hit.reshape(KEY_CHUNK // COUNT_ROWS, COUNT_ROWS, Q_BLOCK), axis=0)
                         for acc, hit in zip(accs, hits))

        accs = lax.fori_loop(0, nch, body, tuple(jnp.zeros((COUNT_ROWS, Q_BLOCK), F32) for _ in cmps))
        return tuple(jnp.sum(acc, axis=0, keepdims=True) for acc in accs)

    ge = lambda a, b: a >= b
    gt = lambda a, b: a > b
    few = qpos < top_k
    def select(rows):
        plane_row = lax.broadcasted_iota(I32, (rows, Q_BLOCK), 0)
        return _radix_select(planes_ref, jnp.where(plane_row < nch * PLANE_ROWS, -1, 0), top_k)

    half_rows = planes_ref.shape[1] // 2
    t = _key_to_float(lax.cond(nch * PLANE_ROWS <= half_rows,
                               lambda: select(half_rows), lambda: select(2 * half_rows)))
    n_gt, n_ge = count((gt,), t)[0], count((ge,), t)[0]
    is_kth = few | ((n_gt < float(top_k)) & (n_ge >= float(top_k)))

    def recount():
        t2 = _kth_largest(lambda cand: count((ge,), cand)[0], top_k, (1, Q_BLOCK))
        return t2, count((gt,), t2)[0]

    t, n_gt = lax.cond(jnp.min(jnp.where(is_kth, 1.0, 0.0)) > 0.0, lambda: (t, n_gt), recount)
    t = jnp.where(few, -jnp.inf, t)
    need = jnp.where(few, 0.0, float(top_k) - n_gt)

    m_ref[...] = jnp.full(m_ref.shape, NEG_INF, F32)
    acc_ref[...] = jnp.zeros(acc_ref.shape, F32)
    ones_rows = jnp.ones((MXU_ROWS, KEY_CHUNK), BF16)

    heads = [(g, slice(hh * Q_BLOCK, (hh + 1) * Q_BLOCK)) for g in range(N_KV_HEADS) for hh in range(GROUP)]

    def chunk_bias(c, n_eq):
        blk = sc_ref[c]
        eq = blk == t
        eqf = jnp.where(eq, 1.0, 0.0)
        eqb = eqf.astype(BF16)
        before = []
        for blk_i in range(KEY_CHUNK // TIE_BLOCK):
            rows = slice(blk_i * TIE_BLOCK, (blk_i + 1) * TIE_BLOCK)
            before.append(n_eq + _dot(low_ref[...], eqb[rows]))
            n_eq = n_eq + jnp.sum(eqf[rows], axis=0, keepdims=True)
        sel = (blk > t) | (eq & (jnp.concatenate(before, axis=0) < need))
        return jnp.where(sel, 0.0, NEG_INF), n_eq

    def head_logits(c, g, lanes, bias):
        off = pl.multiple_of(c * KEY_CHUNK, KEY_CHUNK)
        kc = k_ref[pl.ds(off, KEY_CHUNK), g * HEAD_DIM:(g + 1) * HEAD_DIM]
        return _dot(kc, qt_ref[0, g, :, lanes]) + bias

    def reweigh_chunk(c, bias):
        for g, lanes in heads:
            sh = head_logits(c, g, lanes, bias)
            m_old = m_ref[g, :, lanes]
            m_new = jnp.maximum(m_old, jnp.max(sh, axis=0, keepdims=True))
            acc_ref[g, :, lanes] = jnp.exp2(m_old - m_new) * acc_ref[g, :, lanes]
            p_ref[g, :, lanes] = jnp.exp2(sh - m_new).astype(BF16)
            m_ref[g, :, lanes] = m_new

    def weigh_chunk(c, n_eq):
        bias, n_eq = chunk_bias(c, n_eq)
        excess = jnp.full((1, Q_BLOCK), -jnp.inf, F32)
        for g, lanes in heads:
            sh = head_logits(c, g, lanes, bias)
            m_cur = m_ref[g, :, lanes]
            p_ref[g, :, lanes] = jnp.exp2(sh - m_cur).astype(BF16)
            excess = jnp.maximum(excess, jnp.max(sh, axis=0, keepdims=True) - m_cur)

        @pl.when(jnp.max(excess) > STABILISER_SLACK)
        def _():
            reweigh_chunk(c, bias)

        return n_eq

    def accumulate_chunk(c):
        for g in range(N_KV_HEADS):
            vext = jnp.concatenate([vt_ref[c, g * HEAD_DIM:(g + 1) * HEAD_DIM, :], ones_rows], axis=0)
            acc_ref[g] += _dot(vext, p_ref[g])

    def attn_step(c, n_eq):
        accumulate_chunk(c - 1)
        return weigh_chunk(c, n_eq)

    bias0, n_eq0 = chunk_bias(0, jnp.zeros((1, Q_BLOCK), F32))
    reweigh_chunk(0, bias0)
    lax.fori_loop(1, nch, attn_step, n_eq0)
    accumulate_chunk(nch - 1)

    for g in range(N_KV_HEADS):
        acc = acc_ref[g]
        o = acc[:HEAD_DIM] / acc[HEAD_DIM:HEAD_DIM + 1]
        for hh in range(GROUP):
            col = (g * GROUP + hh) * HEAD_DIM
            o_ref[:, col:col + HEAD_DIM] = o[:, hh * Q_BLOCK:(hh + 1) * Q_BLOCK].T.astype(o_ref.dtype)


def _prompt_attn(qt, iqt, iwt, kb, vt, ikb, *, batch, seq):
    nqb = seq // Q_BLOCK
    nch = seq // KEY_CHUNK
    kvw = N_KV_HEADS * HEAD_DIM
    top_k = min(TOP_K_MAX, seq // 4)
    low = jnp.tril(jnp.ones((TIE_BLOCK, TIE_BLOCK), BF16), k=-1)
    per_qblock = lambda *shape: pl.BlockSpec((1,) + shape, lambda b, i: (b * nqb + i,) + (0,) * len(shape))
    per_batch = lambda width: pl.BlockSpec((seq, width), lambda b, i: (b, 0))
    return pl.pallas_call(
        functools.partial(_prompt_attn_kernel, top_k),
        grid=(batch, nqb),
        in_specs=[per_qblock(N_KV_HEADS, HEAD_DIM, GROUP * Q_BLOCK),
                  per_qblock(IDX_HEADS * IDX_DIM, Q_BLOCK), per_qblock(SUBLANES, Q_BLOCK),
                  per_batch(kvw), pl.BlockSpec((nch, kvw, KEY_CHUNK), lambda b, i: (b, 0, 0)),
                  per_batch(IDX_DIM), _const_spec((TIE_BLOCK, TIE_BLOCK))],
        out_specs=pl.BlockSpec((Q_BLOCK, N_HEADS * HEAD_DIM), lambda b, i: (b * nqb + i, 0)),
        out_shape=jax.ShapeDtypeStruct((batch * seq, N_HEADS * HEAD_DIM), BF16),
        scratch_shapes=[
            pltpu.VMEM((nch, KEY_CHUNK, Q_BLOCK), F32),
            pltpu.VMEM((KEY_BITS, nch * PLANE_ROWS, Q_BLOCK), I32),
            pltpu.VMEM((N_KV_HEADS, 1, GROUP * Q_BLOCK), F32),
            pltpu.VMEM((N_KV_HEADS, HEAD_DIM + MXU_ROWS, GROUP * Q_BLOCK), F32),
            pltpu.VMEM((N_KV_HEADS, KEY_CHUNK, GROUP * Q_BLOCK), BF16),
        ],
        compiler_params=pltpu.CompilerParams(
            dimension_semantics=("arbitrary", "arbitrary"), vmem_limit_bytes=VMEM_LIMIT_BYTES),
        name="prompt_attn",
    )(qt, iqt, iwt, kb, vt, ikb, low)


def _sample_attn_kernel(n_pages, page, top_k, rb, pt_ref, q_ref, iq_ref, iw_ref, kn_ref, vn_ref, ikn_ref,
                        ck_hbm, cv_hbm, cik_hbm, tri_ref, low_ref, o_ref,
                        kbuf, vbuf, ikbuf, sems, sc_ref):
    b = pl.program_id(0)
    nb = pl.num_programs(0)
    past = n_pages * page

    def page_copies(step, slot):
        copies = []
        for r in range(rb):
            for p in range(n_pages):
                phys = pt_ref[step * rb + r, p]
                rows = pl.ds(p * page * N_KV_HEADS, page * N_KV_HEADS)
                lanes = pl.ds(p * page, page)
                copies.append(pltpu.make_async_copy(ck_hbm.at[phys], kbuf.at[slot, r, rows], sems.at[0, slot]))
                copies.append(pltpu.make_async_copy(cv_hbm.at[phys], vbuf.at[slot, r, rows], sems.at[1, slot]))
                copies.append(
                    pltpu.make_async_copy(cik_hbm.at[phys], ikbuf.at[slot, r, :, lanes], sems.at[2, slot]))
        return copies

    slot = b % 2

    @pl.when(b == 0)
    def _():
        for c in page_copies(0, 0):
            c.start()

    @pl.when(b + 1 < nb)
    def _():
        for c in page_copies(b + 1, 1 - slot):
            c.start()

    for c in page_copies(b, slot):
        c.wait()

    sc_rows, sc_news = [], []
    for r in range(rb):
        iq = iq_ref[r].astype(BF16)
        iw = iw_ref[r]
        sidx = _dot(iq, ikbuf[slot, r].astype(BF16))
        sc_row = jnp.sum(jnp.maximum(sidx, 0.0) * iw, axis=0, keepdims=True)
        for p in range(n_pages):
            sc_ref[r, p:p + 1, :] = sc_row[:, p * page:(p + 1) * page]
        ikn = ikn_ref[r].astype(BF16).astype(F32)
        s_new = jnp.sum(iq.astype(F32) * ikn, axis=1, keepdims=True)
        sc_rows.append(sc_row)
        sc_news.append(jnp.sum(jnp.maximum(s_new, 0.0) * iw, axis=0, keepdims=True))

    def count_ge(r, cand):
        hits = jnp.where(sc_rows[r] >= cand, 1.0, 0.0)
        return jnp.sum(hits, axis=1, keepdims=True) + jnp.where(sc_news[r] >= cand, 1.0, 0.0)

    ts = _kth_largest_by_digits([functools.partial(count_ge, r) for r in range(rb)], top_k,
                                SAMPLE_SEARCH_BITS)

    def total(x):
        return jnp.sum(jnp.sum(x, axis=1, keepdims=True), axis=0, keepdims=True)

    for r in range(rb):
        t, sc, sc_new = ts[r], sc_ref[r], sc_news[r]
        n_gt = total(jnp.where(sc > t, 1.0, 0.0)) + jnp.where(sc_new > t, 1.0, 0.0)
        need = float(top_k) - n_gt
        eq = sc == t
        eqf = jnp.where(eq, 1.0, 0.0)
        in_row = _dot(eqf.astype(BF16), tri_ref[...])
        row_tot = jnp.broadcast_to(jnp.sum(eqf, axis=1, keepdims=True), sc.shape)
        rows_before = _dot(low_ref[...], row_tot.astype(BF16))
        sel = (sc > t) | (eq & (in_row + rows_before < need))
        sel_new = (sc_new > t) | ((sc_new == t) & (total(eqf) < need))
        bias = jnp.where(sel, 0.0, NEG_INF)
        bias_row = jnp.concatenate([bias[p:p + 1, :] for p in range(n_pages)], axis=1)
        bias_new = jnp.where(sel_new, 0.0, NEG_INF)

        q = q_ref[r].astype(BF16)
        qf = q.astype(F32)
        for g in range(N_KV_HEADS):
            head_rows = pl.ds(g, past, stride=N_KV_HEADS)
            kg = kbuf[slot, r, head_rows, :].astype(BF16)
            vg = vbuf[slot, r, head_rows, :].astype(BF16)
            kn = kn_ref[r, g:g + 1, :].astype(BF16).astype(F32)
            vn = vn_ref[r, g:g + 1, :].astype(BF16).astype(F32)
            s = _dot_nt(q, kg) + bias_row
            sn = jnp.sum(qf * kn, axis=1, keepdims=True) + bias_new
            m = jnp.maximum(jnp.max(s, axis=1, keepdims=True), sn)
            p = jnp.exp(s - m)
            pn = jnp.exp(sn - m)
            l = jnp.sum(p, axis=1, keepdims=True) + pn
            o = (_dot(p.astype(BF16), vg) + pn * vn) / l
            o_ref[r, g * GROUP:(g + 1) * GROUP, :] = o[g * GROUP:(g + 1) * GROUP]


def _sample_attn(page_table, q, iq, iw, k_new, v_new, ik_new, cache_k, cache_v, cache_ik):
    nreq, n_pages = page_table.shape
    n_phys, page = cache_k.shape[0], cache_k.shape[1]
    top_k = min(TOP_K_MAX, (n_pages * page + 1) // 4)
    assert top_k < n_pages * page + 1
    ck = cache_k.reshape(n_phys, page * N_KV_HEADS, HEAD_DIM)
    cv = cache_v.reshape(n_phys, page * N_KV_HEADS, HEAD_DIM)
    cik = jnp.swapaxes(cache_ik, 1, 2)
    tri = jnp.triu(jnp.ones((page, page), BF16), k=1)
    low = jnp.tril(jnp.ones((n_pages, n_pages), BF16), k=-1)
    rb = SAMPLE_REQS
    assert nreq % rb == 0
    pad_rows = lambda a: jnp.pad(a.astype(F32), ((0, 0), (0, MXU_ROWS - a.shape[1]), (0, 0)))
    per_req = lambda *shape: pl.BlockSpec((rb,) + shape, lambda b, pt: (b,) + (0,) * len(shape))
    const = lambda *shape: pl.BlockSpec(shape, lambda b, pt: (0,) * len(shape))
    any_spec = pl.BlockSpec(memory_space=pl.ANY)
    grid_spec = pltpu.PrefetchScalarGridSpec(
        num_scalar_prefetch=1,
        grid=(nreq // rb,),
        in_specs=[per_req(MXU_ROWS, HEAD_DIM), per_req(MXU_ROWS, IDX_DIM), per_req(MXU_ROWS, 1),
                  per_req(N_KV_HEADS, HEAD_DIM), per_req(N_KV_HEADS, HEAD_DIM), per_req(1, IDX_DIM),
                  any_spec, any_spec, any_spec, const(page, page), const(n_pages, n_pages)],
        out_specs=per_req(N_HEADS, HEAD_DIM),
        scratch_shapes=[
            pltpu.VMEM((2, rb, n_pages * page * N_KV_HEADS, HEAD_DIM), F32),
            pltpu.VMEM((2, rb, n_pages * page * N_KV_HEADS, HEAD_DIM), F32),
            pltpu.VMEM((2, rb, IDX_DIM, n_pages * page), F32),
            pltpu.SemaphoreType.DMA((3, 2)),
            pltpu.VMEM((rb, n_pages, page), F32),
        ],
    )
    out = pl.pallas_call(
        functools.partial(_sample_attn_kernel, n_pages, page, top_k, rb),
        grid_spec=grid_spec,
        out_shape=jax.ShapeDtypeStruct((nreq, N_HEADS, HEAD_DIM), F32),
        compiler_params=pltpu.CompilerParams(
            dimension_semantics=("arbitrary",), vmem_limit_bytes=VMEM_LIMIT_BYTES),
        name="sample_attn",
    )(page_table,
      pad_rows(q.reshape(nreq, N_HEADS, HEAD_DIM)),
      pad_rows(jnp.transpose(iq, (1, 0, 2))),
      pad_rows(iw.reshape(nreq, IDX_HEADS, 1)),
      k_new.reshape(nreq, N_KV_HEADS, HEAD_DIM), v_new.reshape(nreq, N_KV_HEADS, HEAD_DIM),
      ik_new.reshape(nreq, 1, IDX_DIM),
      ck, cv, cik, tri, low)
    return out.reshape(nreq, N_HEADS * HEAD_DIM)


def _merge_stage(x, attn, oa, sgb, wao_ref, wo_ref, gmem_ref, wmq_ref):
    out_b = _dot(attn, wao_ref[...])
    merged = oa.astype(F32) + sgb.astype(F32) * out_b
    x1 = x + _dot(merged.astype(BF16), wo_ref[...])
    hm = _rms(x1, gmem_ref[...]).astype(BF16)
    return x1, _dot(hm, wmq_ref[...]) * MEM_SCALE


def _ffn_stage(x1, mem_o, wmo_ref, gffn_ref, wg_ref, wu_ref, wd_ref, gfin_ref):
    x2 = x1 + _dot(mem_o.astype(BF16), wmo_ref[...])
    hf = _rms(x2, gffn_ref[...]).astype(BF16)
    acc = jnp.zeros_like(x2)
    for c in range(D_FF // FF_CHUNK):
        cols = slice(c * FF_CHUNK, (c + 1) * FF_CHUNK)
        gate = _dot(hf, wg_ref[:, cols])
        f = gate * _sigmoid(gate) * _dot(hf, wu_ref[:, cols])
        acc = acc + _dot(f.astype(BF16), wd_ref[cols, :])
    return _rms(x2 + acc, gfin_ref[...])


def _post_prompt_kernel(x_ref, attn_ref, oa_ref, sgb_ref, mk_ref, mv_ref, wao_ref, wo_ref, gmem_ref,
                        wmq_ref, wmo_ref, gffn_ref, wg_ref, wu_ref, wd_ref, gfin_ref, y_ref, mo_ref):
    x1, qm = _merge_stage(x_ref[...], attn_ref[...], oa_ref[...], sgb_ref[...],
                          wao_ref, wo_ref, gmem_ref, wmq_ref)
    for h in range(MEM_HEADS):
        cols = slice(h * MEM_HEAD_DIM, (h + 1) * MEM_HEAD_DIM)
        s = _dot_nt(qm[:, cols].astype(BF16), mk_ref[:, cols])
        p = jnp.exp(s - jnp.max(s, axis=1, keepdims=True))
        o = _dot(p.astype(BF16), mv_ref[:, cols]) / jnp.sum(p, axis=1, keepdims=True)
        mo_ref[:, cols] = o.astype(BF16)
    y_ref[...] = _ffn_stage(x1, mo_ref[...], wmo_ref, gffn_ref, wg_ref, wu_ref, wd_ref, gfin_ref)


def _post_prompt(x, attn, oa, sgb, mk, mv, w, *, batch, seq, tm):
    nt = seq // tm
    n_mem = mk.shape[0] // batch
    tok = lambda width: pl.BlockSpec((tm, width), lambda b, t: (b * nt + t, 0))
    mem = pl.BlockSpec((n_mem, D_MODEL), lambda b, t: (b, 0))
    sq = _const_spec((D_MODEL, D_MODEL))
    gain = _const_spec((1, D_MODEL))
    return pl.pallas_call(
        _post_prompt_kernel,
        grid=(batch, nt),
        in_specs=[tok(D_MODEL), tok(D_MODEL), tok(D_MODEL), tok(D_MODEL), mem, mem,
                  sq, sq, gain, sq, sq, gain,
                  _const_spec((D_MODEL, D_FF)), _const_spec((D_MODEL, D_FF)), _const_spec((D_FF, D_MODEL)),
                  gain],
        out_specs=tok(D_MODEL),
        out_shape=jax.ShapeDtypeStruct((batch * seq, D_MODEL), F32),
        scratch_shapes=[pltpu.VMEM((tm, D_MODEL), BF16)],
        compiler_params=pltpu.CompilerParams(
            dimension_semantics=("arbitrary", "arbitrary"), vmem_limit_bytes=VMEM_LIMIT_BYTES),
        name="post_prompt",
    )(x, attn, oa, sgb, mk, mv, w["wao"], w["wo"], w["gmem"], w["wmq"], w["wmo"], w["gffn"],
      w["wg"], w["wu"], w["wd"], w["gfin"])


def _merge_sample_kernel(x_ref, attn_ref, oa_ref, sgb_ref, wao_ref, wo_ref, gmem_ref, wmq_ref,
                         x1_ref, qm_ref):
    x1, qm = _merge_stage(x_ref[...], attn_ref[...].astype(BF16), oa_ref[...], sgb_ref[...],
                          wao_ref, wo_ref, gmem_ref, wmq_ref)
    x1_ref[...] = x1
    qm_ref[...] = qm


def _mem_sample_kernel(rb, q_ref, mk_ref, mv_ref, o_ref):
    for r in range(rb):
        s = jnp.sum(mk_ref[r] * q_ref[r][None], axis=2, keepdims=True)
        p = jnp.exp(s - jnp.max(s, axis=0, keepdims=True))
        o_ref[r] = jnp.sum(p * mv_ref[r], axis=0) / jnp.sum(p, axis=0)


def _ffn_sample_kernel(x1_ref, mo_ref, wmo_ref, gffn_ref, wg_ref, wu_ref, wd_ref, gfin_ref, y_ref):
    y_ref[...] = _ffn_stage(x1_ref[...], mo_ref[...], wmo_ref, gffn_ref, wg_ref, wu_ref, wd_ref, gfin_ref)


def _post_sample(x, attn, oa, sgb, mem_k, mem_v, w):
    n = x.shape[0]
    n_mem = mem_k.shape[1]
    full = lambda *shape: _const_spec(shape)
    act = full(n, D_MODEL)
    sq = full(D_MODEL, D_MODEL)
    gain = full(1, D_MODEL)
    params = pltpu.CompilerParams(dimension_semantics=("arbitrary",), vmem_limit_bytes=VMEM_LIMIT_BYTES)
    x1, qm = pl.pallas_call(
        _merge_sample_kernel, grid=(1,),
        in_specs=[act, act, act, act, sq, sq, gain, sq], out_specs=[act, act],
        out_shape=[jax.ShapeDtypeStruct((n, D_MODEL), F32)] * 2,
        compiler_params=params, name="merge_sample",
    )(x, attn, oa, sgb, w["wao"], w["wo"], w["gmem"], w["wmq"])

    rb = MEM_REQS
    assert n % rb == 0
    heads = pl.BlockSpec((rb, MEM_HEADS, MEM_HEAD_DIM), lambda i: (i, 0, 0))
    mem = pl.BlockSpec((rb, n_mem, MEM_HEADS, MEM_HEAD_DIM), lambda i: (i, 0, 0, 0))
    mem_o = pl.pallas_call(
        functools.partial(_mem_sample_kernel, rb), grid=(n // rb,),
        in_specs=[heads, mem, mem], out_specs=heads,
        out_shape=jax.ShapeDtypeStruct((n, MEM_HEADS, MEM_HEAD_DIM), F32),
        compiler_params=params, name="mem_sample",
    )(qm.reshape(n, MEM_HEADS, MEM_HEAD_DIM), mem_k, mem_v)

    return pl.pallas_call(
        _ffn_sample_kernel, grid=(1,),
        in_specs=[act, act, sq, gain, full(D_MODEL, D_FF), full(D_MODEL, D_FF), full(D_FF, D_MODEL), gain],
        out_specs=act, out_shape=jax.ShapeDtypeStruct((n, D_MODEL), F32),
        compiler_params=params, name="ffn_sample",
    )(x1, mem_o.reshape(n, D_MODEL), w["wmo"], w["gffn"], w["wg"], w["wu"], w["wd"], w["gfin"])


def _memory_kv_kernel(mem_ref, g_ref, wk_ref, wv_ref, kf_ref, vf_ref, kb_ref, vb_ref):
    m = _rms(mem_ref[...], g_ref[...]).astype(BF16)
    k = _dot(m, wk_ref[...])
    v = _dot(m, wv_ref[...])
    kf_ref[...] = k
    vf_ref[...] = v
    kb_ref[...] = k.astype(BF16)
    vb_ref[...] = v.astype(BF16)


def _memory_kv(mem, g, wk, wv):
    n = mem.shape[0]
    full = lambda *shape: _const_spec(shape)
    act = full(n, D_MODEL)
    sq = full(D_MODEL, D_MODEL)
    return pl.pallas_call(
        _memory_kv_kernel, grid=(1,),
        in_specs=[act, full(1, D_MODEL), sq, sq], out_specs=[act] * 4,
        out_shape=[jax.ShapeDtypeStruct((n, D_MODEL), F32)] * 2 + [jax.ShapeDtypeStruct((n, D_MODEL), BF16)] * 2,
        compiler_params=pltpu.CompilerParams(
            dimension_semantics=("arbitrary",), vmem_limit_bytes=VMEM_LIMIT_BYTES),
        name="memory_kv",
    )(mem, g, wk, wv)


def _prep_w_in(w_in):
    w_t = w_in.T
    main = jnp.pad(w_t[:IDX_RAW_END].astype(BF16), ((0, W_COLS - IDX_RAW_END), (0, 0)))
    return main, w_t[IDX_RAW_END:].astype(BF16)


def kernel(x_prompt, x_sample, mem_prompt, cache_k, cache_v, cache_idx_k, cache_mem_k, cache_mem_v, state_conv, page_table, g_mix, w_in, conv_w, w_conv_out, w_attn_out, w_o, g_mem, g_mem_kv, w_mq, w_mk, w_mv, w_mo, g_ffn, w_gate, w_up, w_down, g_final):
    depth = w_in.shape[0]
    assert depth == 1, "single-layer step"
    batch, seq, _ = x_prompt.shape
    nreq, dec_seq, _ = x_sample.shape
    assert dec_seq == 1
    n_mem = mem_prompt.shape[1]
    l = 0
    bf = lambda a: a.astype(BF16)
    row = lambda a: a.reshape(1, -1)

    w_main, w_gates = _prep_w_in(w_in[l])
    wco = bf(w_conv_out[l])
    w = dict(wao=bf(w_attn_out[l]), wo=bf(w_o[l]), gmem=row(g_mem[l]), wmq=bf(w_mq[l]), wmo=bf(w_mo[l]),
             gffn=row(g_ffn[l]), wg=bf(w_gate[l]), wu=bf(w_up[l]), wd=bf(w_down[l]), gfin=row(g_final))

    xp = x_prompt.reshape(batch * seq, D_MODEL)
    mkf, mvf, mkb, mvb = _memory_kv(mem_prompt.reshape(batch * n_mem, D_MODEL), row(g_mem_kv[l]),
                                    bf(w_mk[l]), bf(w_mv[l]))
    (kf, vf, ikf, oa, sgb, qt, kb, vt, iqt, ikb, iwt, conv_p) = _mixer_in(
        xp, row(g_mix[l]), w_main, w_gates, conv_w[l], wco, batch=batch, seq=seq, tm=KEY_CHUNK)
    attn = _prompt_attn(qt, iqt, iwt, kb, vt, ikb, batch=batch, seq=seq)
    yp = _post_prompt(xp, attn, oa, sgb, mkb, mvb, w, batch=batch, seq=seq, tm=POST_TILE)

    xs = x_sample.reshape(nreq, D_MODEL)
    st = state_conv[l]
    (kf_s, vf_s, ikf_s, oa_s, sgb_s, q_s, iq_s, iw_s, u_s) = _mixer_in(
        xs, row(g_mix[l]), w_main, w_gates, conv_w[l], wco, batch=nreq, seq=1, tm=nreq, state=(st[:, 0], st[:, 1]))
    attn_s = _sample_attn(page_table, q_s, iq_s, iw_s, kf_s, vf_s, ikf_s,
                          cache_k[l], cache_v[l], cache_idx_k[l])
    ys = _post_sample(xs, attn_s, oa_s, sgb_s, cache_mem_k[l], cache_mem_v[l], w)

    return (
        yp.reshape(batch, seq, D_MODEL),
        ys.reshape(nreq, 1, D_MODEL),
        kf.reshape(1, batch, seq, N_KV_HEADS, HEAD_DIM),
        vf.reshape(1, batch, seq, N_KV_HEADS, HEAD_DIM),
        ikf.reshape(1, batch, seq, IDX_DIM),
        conv_p.reshape(1, batch, CONV_WIDTH - 1, D_CONV),
        mkf.reshape(1, batch, n_mem, MEM_HEADS, MEM_HEAD_DIM),
        mvf.reshape(1, batch, n_mem, MEM_HEADS, MEM_HEAD_DIM),
        kf_s.reshape(1, nreq, 1, N_KV_HEADS, HEAD_DIM),
        vf_s.reshape(1, nreq, 1, N_KV_HEADS, HEAD_DIM),
        ikf_s.reshape(1, nreq, 1, IDX_DIM),
        jnp.stack([st[:, 1], u_s], axis=1).reshape(1, nreq, CONV_WIDTH - 1, D_CONV),
    )
```

```python
import functools

import jax
import jax.numpy as jnp
from jax import lax
from jax.experimental import pallas as pl
from jax.experimental.pallas import tpu as pltpu

F32 = jnp.float32
BF16 = jnp.bfloat16
I32 = jnp.int32

D_MODEL = 1024
D_CONV = 1024
CONV_WIDTH = 3
N_HEADS = 8
HEAD_DIM = 128
N_KV_HEADS = 2
GROUP = N_HEADS // N_KV_HEADS
IDX_HEADS = 4
IDX_DIM = 64
TOP_K_MAX = 256
MEM_HEADS = 4
MEM_HEAD_DIM = D_MODEL // MEM_HEADS
D_FF = 2816
EPS = 1e-6
NEG_INF = -1e30
IDX_SCALE = (IDX_HEADS * IDX_DIM) ** -0.5
ATTN_SCALE = HEAD_DIM ** -0.5
MEM_SCALE = MEM_HEAD_DIM ** -0.5
LOG2_E = 1.4426950408889634

LANES = 128
SUBLANES = 8
MXU_ROWS = 16
VMEM_LIMIT_BYTES = 56 * 1024 * 1024

COL_CIN = 0
COL_CB = COL_CIN + D_CONV
COL_CC = COL_CB + D_CONV
COL_Q = COL_CC + D_CONV
COL_K = COL_Q + N_HEADS * HEAD_DIM
COL_V = COL_K + N_KV_HEADS * HEAD_DIM
COL_IQ = COL_V + N_KV_HEADS * HEAD_DIM
COL_IK = COL_IQ + IDX_HEADS * IDX_DIM
COL_IW = COL_IK + IDX_DIM
IDX_RAW_END = COL_IW + IDX_HEADS
W_COLS = -(-IDX_RAW_END // LANES) * LANES

Q_BLOCK = 256
KEY_CHUNK = 512
COUNT_ROWS = 64
KEY_BITS = 32
PLANE_ROWS = KEY_CHUNK // KEY_BITS
TIE_BLOCK = 128
SAMPLE_SEARCH_BITS = 4
SAMPLE_REQS = 4
MEM_REQS = 4
FF_CHUNK = D_FF // 11
POST_TILE = 512
STABILISER_SLACK = 64.0
INT_MIN = -2 ** 31


def _rms(x, g):
    return x * lax.rsqrt(jnp.mean(x * x, axis=-1, keepdims=True) + EPS) * g


def _dot(a, b):
    return jnp.dot(a, b, preferred_element_type=F32)


def _dot_nt(a, b):
    return lax.dot_general(a, b, (((1,), (1,)), ((), ())), preferred_element_type=F32)


def _sigmoid(x):
    return 1.0 / (1.0 + jnp.exp(-x))


def _key_to_float(ukey):
    skey = ukey ^ INT_MIN
    bits = jnp.where(skey < 0, skey ^ 0x7FFFFFFF, skey)
    return lax.bitcast_convert_type(bits, F32)


def _float_to_key(x):
    bits = lax.bitcast_convert_type(x, I32)
    return bits ^ (lax.shift_right_arithmetic(bits, jnp.full_like(bits, 31)) & 0x7FFFFFFF) ^ INT_MIN


def _kth_largest(count_ge, top_k, shape):
    def body(b, ukey):
        cand = ukey | jnp.left_shift(jnp.int32(1), KEY_BITS - 1 - b)
        ok = count_ge(_key_to_float(cand)) >= float(top_k)
        return jnp.where(ok, cand, ukey)

    ukey = lax.fori_loop(0, KEY_BITS, body, jnp.zeros(shape, I32))
    return _key_to_float(ukey)


def _kth_largest_by_digits(count_ge_fns, top_k, bits):
    digits = jnp.minimum(lax.broadcasted_iota(I32, (2 ** bits, 1), 0) + 1, 2 ** bits - 1)
    digits_f = digits.astype(F32)

    def body(rd, ukeys):
        shift = KEY_BITS - bits * (rd + 1)
        out = []
        for count_ge, ukey in zip(count_ge_fns, ukeys):
            cand = ukey | jnp.left_shift(digits, shift)
            ok = count_ge(_key_to_float(cand)) >= float(top_k)
            digit = jnp.max(jnp.where(ok, digits_f, 0.0), axis=0, keepdims=True).astype(I32)
            out.append(ukey | jnp.left_shift(digit, shift))
        return tuple(out)

    ukeys = lax.fori_loop(0, KEY_BITS // bits, body, tuple(jnp.zeros((1, 1), I32) for _ in count_ge_fns))
    return [_key_to_float(u) for u in ukeys]


def _bit_transpose32(words):
    a = list(words)
    j, m = 16, 0x0000FFFF
    while j:
        k = 0
        while k < 32:
            t = (a[k] ^ lax.shift_right_logical(a[k + j], jnp.full_like(a[k], j))) & m
            a[k] = a[k] ^ t
            a[k + j] = a[k + j] ^ lax.shift_left(t, jnp.full_like(t, j))
            k = (k + j + 1) & ~j
        j >>= 1
        m = (m ^ (m << j)) & 0xFFFFFFFF
    return a


def _radix_select(planes_ref, live, top_k):
    rows = live.shape[0]

    def body(i, carry):
        live, ukey, n_above = carry
        hit = live & planes_ref[i, :rows]
        cnt = jnp.sum(lax.population_count(hit).astype(F32), axis=0, keepdims=True)
        take = n_above + cnt >= float(top_k)
        live = jnp.where(take, hit, live ^ hit)
        ukey = jnp.where(take, ukey | jnp.left_shift(jnp.int32(1), KEY_BITS - 1 - i), ukey)
        n_above = jnp.where(take, n_above, n_above + cnt)
        return live, ukey, n_above

    lanes = live.shape[1]
    init = (live, jnp.zeros((1, lanes), I32), jnp.zeros((1, lanes), F32))
    return lax.fori_loop(0, KEY_BITS, body, init)[1]


def _mixer_in_kernel(is_prompt, tm, *refs):
    if is_prompt:
        (x_ref, g_ref, w_ref, wgate_ref, cw_ref, wco_ref,
         kf_ref, vf_ref, ikf_ref, oa_ref, sgb_ref,
         qt_ref, kb_ref, vt_ref, iqt_ref, ikb_ref, iwt_ref, cs_ref, ubuf) = refs
    else:
        (x_ref, g_ref, w_ref, wgate_ref, cw_ref, wco_ref, s0_ref, s1_ref,
         kf_ref, vf_ref, ikf_ref, oa_ref, sgb_ref,
         q_ref, iq_ref, iw_ref, u_ref) = refs

    h = _rms(x_ref[...], g_ref[...]).astype(BF16)

    def proj(lo, hi):
        return _dot_nt(h, w_ref[lo:hi, :])

    u = proj(COL_CC, COL_CC + D_CONV) * proj(COL_CIN, COL_CIN + D_CONV)
    cw = cw_ref[...]
    if is_prompt:
        @pl.when(pl.program_id(1) == 0)
        def _():
            ubuf[0:SUBLANES, :] = jnp.zeros((SUBLANES, D_CONV), F32)

        ubuf[SUBLANES:SUBLANES + tm, :] = u
        conv = (ubuf[SUBLANES - 2:SUBLANES - 2 + tm, :] * cw[0:1]
                + ubuf[SUBLANES - 1:SUBLANES - 1 + tm, :] * cw[1:2] + u * cw[2:3])
        ubuf[0:SUBLANES, :] = ubuf[tm:tm + SUBLANES, :]
        cs_ref[0] = u[tm - (CONV_WIDTH - 1):, :]
    else:
        conv = s0_ref[...] * cw[0:1] + s1_ref[...] * cw[1:2] + u * cw[2:3]
        u_ref[...] = u

    a_in = (proj(COL_CB, COL_CB + D_CONV) * conv).astype(BF16)
    out_a = _dot(a_in, wco_ref[...])
    oa_ref[...] = (_sigmoid(_dot_nt(h, wgate_ref[:D_MODEL, :])) * out_a).astype(BF16)
    sgb_ref[...] = _sigmoid(_dot_nt(h, wgate_ref[D_MODEL:, :])).astype(BF16)

    q = proj(COL_Q, COL_K) * (ATTN_SCALE * LOG2_E if is_prompt else ATTN_SCALE)
    kv = proj(COL_K, COL_IQ)
    k = kv[:, :N_KV_HEADS * HEAD_DIM]
    v = kv[:, N_KV_HEADS * HEAD_DIM:]
    for g in range(N_KV_HEADS):
        head_rows = pl.ds(g, tm, stride=N_KV_HEADS)
        kf_ref[head_rows, :] = k[:, g * HEAD_DIM:(g + 1) * HEAD_DIM]
        vf_ref[head_rows, :] = v[:, g * HEAD_DIM:(g + 1) * HEAD_DIM]
    idx = proj(COL_IQ, W_COLS)
    iq = idx[:, :IDX_HEADS * IDX_DIM]
    ikw = idx[:, COL_IK - COL_IQ:]
    ik = ikw[:, :IDX_DIM]
    ikf_ref[...] = ik
    if is_prompt:
        kb_ref[...] = k.astype(BF16)
        ikb_ref[...] = ik.astype(BF16)
        vt_ref[0] = v.T.astype(BF16)
        for j in range(tm // Q_BLOCK):
            rows = slice(j * Q_BLOCK, (j + 1) * Q_BLOCK)
            for head in range(N_HEADS):
                g, hh = divmod(head, GROUP)
                qt_ref[j, g, :, hh * Q_BLOCK:(hh + 1) * Q_BLOCK] = (
                    q[rows, head * HEAD_DIM:(head + 1) * HEAD_DIM].T.astype(BF16))
            iqt_ref[j] = iq[rows].T.astype(BF16)
            iwt_ref[j] = ikw[rows].T[IDX_DIM:IDX_DIM + SUBLANES] * IDX_SCALE
    else:
        q_ref[...] = q.astype(BF16)
        for hd in range(IDX_HEADS):
            iq_ref[hd] = iq[:, hd * IDX_DIM:(hd + 1) * IDX_DIM].astype(BF16)
        iw_ref[...] = ikw[:, IDX_DIM:IDX_DIM + IDX_HEADS] * IDX_SCALE


def _const_spec(shape):
    nd = len(shape)
    return pl.BlockSpec(shape, lambda *_: (0,) * nd, pipeline_mode=pl.Buffered(1))


def _mixer_in(x, g_mix, w_main, w_gates, conv_w, w_conv_out, *, batch, seq, tm, state=None):
    is_prompt = state is None
    t_all = batch * seq
    nt = seq // tm if is_prompt else 1
    grid = (batch, nt) if is_prompt else (1, 1)
    tok = lambda width: pl.BlockSpec((tm, width), lambda b, t: (b * nt + t, 0))
    in_specs = [tok(D_MODEL), _const_spec((1, D_MODEL)), _const_spec((W_COLS, D_MODEL)), _const_spec((2 * D_MODEL, D_MODEL)),
                _const_spec((CONV_WIDTH, D_CONV)), _const_spec((D_CONV, D_MODEL))]
    args = [x, g_mix, w_main, w_gates, conv_w, w_conv_out]
    kvw = N_KV_HEADS * HEAD_DIM
    out_shapes = [
        jax.ShapeDtypeStruct((t_all * N_KV_HEADS, HEAD_DIM), F32),
        jax.ShapeDtypeStruct((t_all * N_KV_HEADS, HEAD_DIM), F32),
        jax.ShapeDtypeStruct((t_all, IDX_DIM), F32),
        jax.ShapeDtypeStruct((t_all, D_MODEL), BF16),
        jax.ShapeDtypeStruct((t_all, D_MODEL), BF16),
    ]
    kv_rows = pl.BlockSpec((tm * N_KV_HEADS, HEAD_DIM), lambda b, t: (b * nt + t, 0))
    out_specs = [kv_rows, kv_rows] + [tok(s.shape[1]) for s in out_shapes[2:]]
    scratch = []
    if is_prompt:
        assert tm == KEY_CHUNK and tm % Q_BLOCK == 0
        qb = tm // Q_BLOCK
        nblk = t_all // Q_BLOCK
        per_qblock = lambda *shape: pl.BlockSpec((qb,) + shape, lambda b, t: (b * nt + t,) + (0,) * len(shape))
        out_shapes += [
            jax.ShapeDtypeStruct((nblk, N_KV_HEADS, HEAD_DIM, GROUP * Q_BLOCK), BF16),
            jax.ShapeDtypeStruct((t_all, kvw), BF16),
            jax.ShapeDtypeStruct((t_all // tm, kvw, tm), BF16),
            jax.ShapeDtypeStruct((nblk, IDX_HEADS * IDX_DIM, Q_BLOCK), BF16),
            jax.ShapeDtypeStruct((t_all, IDX_DIM), BF16),
            jax.ShapeDtypeStruct((nblk, SUBLANES, Q_BLOCK), F32),
            jax.ShapeDtypeStruct((batch, CONV_WIDTH - 1, D_CONV), F32),
        ]
        out_specs += [
            per_qblock(N_KV_HEADS, HEAD_DIM, GROUP * Q_BLOCK), tok(kvw),
            pl.BlockSpec((1, kvw, tm), lambda b, t: (b * nt + t, 0, 0)),
            per_qblock(IDX_HEADS * IDX_DIM, Q_BLOCK), tok(IDX_DIM), per_qblock(SUBLANES, Q_BLOCK),
            pl.BlockSpec((1, CONV_WIDTH - 1, D_CONV), lambda b, t: (b, 0, 0)),
        ]
        scratch.append(pltpu.VMEM((tm + SUBLANES, D_CONV), F32))
    else:
        in_specs += [tok(D_CONV), tok(D_CONV)]
        args += list(state)
        out_shapes += [
            jax.ShapeDtypeStruct((t_all, N_HEADS * HEAD_DIM), BF16),
            jax.ShapeDtypeStruct((IDX_HEADS, t_all, IDX_DIM), BF16),
            jax.ShapeDtypeStruct((t_all, IDX_HEADS), F32),
            jax.ShapeDtypeStruct((t_all, D_CONV), F32),
        ]
        out_specs += [tok(N_HEADS * HEAD_DIM),
                      pl.BlockSpec((IDX_HEADS, tm, IDX_DIM), lambda b, t: (0, b * nt + t, 0)),
                      tok(IDX_HEADS), tok(D_CONV)]
    return pl.pallas_call(
        functools.partial(_mixer_in_kernel, is_prompt, tm),
        grid=grid, in_specs=in_specs, out_specs=out_specs, out_shape=out_shapes,
        scratch_shapes=scratch,
        compiler_params=pltpu.CompilerParams(
            dimension_semantics=("arbitrary", "arbitrary"), vmem_limit_bytes=VMEM_LIMIT_BYTES),
        name="mixer_in_prompt" if is_prompt else "mixer_in_sample",
    )(*args)


def _prompt_attn_kernel(top_k, qt_ref, iqt_ref, iwt_ref, k_ref, vt_ref, ik_ref, low_ref, o_ref,
                        sc_ref, planes_ref, m_ref, acc_ref, p_ref):
    i = pl.program_id(1)
    nch = i // (KEY_CHUNK // Q_BLOCK) + 1
    qpos = i * Q_BLOCK + lax.broadcasted_iota(I32, (1, Q_BLOCK), 1)

    @pl.when(i == 0)
    def _():
        planes_ref[...] = jnp.zeros(planes_ref.shape, I32)

    iw = iwt_ref[0]
    iqt = iqt_ref[0]
    iq_pairs = [jnp.concatenate([iqt[h * IDX_DIM:(h + 1) * IDX_DIM] for h in (2 * pr, 2 * pr + 1)], axis=1)
                for pr in range(IDX_HEADS // 2)]

    def score_chunk(c):
        off = pl.multiple_of(c * KEY_CHUNK, KEY_CHUNK)
        ikc = ik_ref[pl.ds(off, KEY_CHUNK), :]
        acc = jnp.zeros((KEY_CHUNK, Q_BLOCK), F32)
        for pr in range(IDX_HEADS // 2):
            s2 = _dot(ikc, iq_pairs[pr])
            for e in range(2):
                h = 2 * pr + e
                acc = acc + jnp.maximum(s2[:, e * Q_BLOCK:(e + 1) * Q_BLOCK], 0.0) * iw[h:h + 1]
        kpos = off + lax.broadcasted_iota(I32, (KEY_CHUNK, Q_BLOCK), 0)
        sc_ref[c] = jnp.where(kpos <= qpos, acc, -jnp.inf)

    def plane_chunk(c):
        keys = _float_to_key(sc_ref[c])
        for wd in range(PLANE_ROWS // SUBLANES):
            words = [keys[(wd * KEY_BITS + j) * SUBLANES:(wd * KEY_BITS + j + 1) * SUBLANES]
                     for j in range(KEY_BITS)]
            row = pl.multiple_of(c * PLANE_ROWS + wd * SUBLANES, SUBLANES)
            for b, plane in enumerate(_bit_transpose32(words)):
                planes_ref[b, pl.ds(row, SUBLANES), :] = plane

    score_chunk(0)

    def score_step(c, carry):
        plane_chunk(c - 1)
        score_chunk(c)
        return carry

    lax.fori_loop(1, nch, score_step, 0)
    plane_chunk(nch - 1)

    def count(cmps, t):
        def body(c, accs):
            blk = sc_ref[c]
            hits = [jnp.where(cmp(blk, t), 1.0, 0.0) for cmp in cmps]
            return tuple(acc + jnp.sum(hit.reshape(KEY_CHUNK // COUNT_ROWS, COUNT_ROWS, Q_BLOCK), axis=0)
                         for acc, hit in zip(accs, hits))

        accs = lax.fori_loop(0, nch, body, tuple(jnp.zeros((COUNT_ROWS, Q_BLOCK), F32) for _ in cmps))
        return tuple(jnp.sum(acc, axis=0, keepdims=True) for acc in accs)

    ge = lambda a, b: a >= b
    gt = lambda a, b: a > b
    few = qpos < top_k
    def select(rows):
        plane_row = lax.broadcasted_iota(I32, (rows, Q_BLOCK), 0)
        return _radix_select(planes_ref, jnp.where(plane_row < nch * PLANE_ROWS, -1, 0), top_k)

    half_rows = planes_ref.shape[1] // 2
    t = _key_to_float(lax.cond(nch * PLANE_ROWS <= half_rows,
                               lambda: select(half_rows), lambda: select(2 * half_rows)))
    n_gt, n_ge = count((gt,), t)[0], count((ge,), t)[0]
    is_kth = few | ((n_gt < float(top_k)) & (n_ge >= float(top_k)))

    def recount():
        t2 = _kth_largest(lambda cand: count((ge,), cand)[0], top_k, (1, Q_BLOCK))
        return t2, count((gt,), t2)[0]

    t, n_gt = lax.cond(jnp.min(jnp.where(is_kth, 1.0, 0.0)) > 0.0, lambda: (t, n_gt), recount)
    t = jnp.where(few, -jnp.inf, t)
    need = jnp.where(few, 0.0, float(top_k) - n_gt)

    m_ref[...] = jnp.full(m_ref.shape, NEG_INF, F32)
    acc_ref[...] = jnp.zeros(acc_ref.shape, F32)
    ones_rows = jnp.ones((MXU_ROWS, KEY_CHUNK), BF16)

    heads = [(g, slice(hh * Q_BLOCK, (hh + 1) * Q_BLOCK)) for g in range(N_KV_HEADS) for hh in range(GROUP)]

    def chunk_bias(c, n_eq):
        blk = sc_ref[c]
        eq = blk == t
        eqf = jnp.where(eq, 1.0, 0.0)
        eqb = eqf.astype(BF16)
        before = []
        for blk_i in range(KEY_CHUNK // TIE_BLOCK):
            rows = slice(blk_i * TIE_BLOCK, (blk_i + 1) * TIE_BLOCK)
            before.append(n_eq + _dot(low_ref[...], eqb[rows]))
            n_eq = n_eq + jnp.sum(eqf[rows], axis=0, keepdims=True)
        sel = (blk > t) | (eq & (jnp.concatenate(before, axis=0) < need))
        return jnp.where(sel, 0.0, NEG_INF), n_eq

    def head_logits(c, g, lanes, bias):
        off = pl.multiple_of(c * KEY_CHUNK, KEY_CHUNK)
        kc = k_ref[pl.ds(off, KEY_CHUNK), g * HEAD_DIM:(g + 1) * HEAD_DIM]
        return _dot(kc, qt_ref[0, g, :, lanes]) + bias

    def reweigh_chunk(c, bias):
        for g, lanes in heads:
            sh = head_logits(c, g, lanes, bias)
            m_old = m_ref[g, :, lanes]
            m_new = jnp.maximum(m_old, jnp.max(sh, axis=0, keepdims=True))
            acc_ref[g, :, lanes] = jnp.exp2(m_old - m_new) * acc_ref[g, :, lanes]
            p_ref[g, :, lanes] = jnp.exp2(sh - m_new).astype(BF16)
            m_ref[g, :, lanes] = m_new

    def weigh_chunk(c, n_eq):
        bias, n_eq = chunk_bias(c, n_eq)
        excess = jnp.full((1, Q_BLOCK), -jnp.inf, F32)
        for g, lanes in heads:
            sh = head_logits(c, g, lanes, bias)
            m_cur = m_ref[g, :, lanes]
            p_ref[g, :, lanes] = jnp.exp2(sh - m_cur).astype(BF16)
            excess = jnp.maximum(excess, jnp.max(sh, axis=0, keepdims=True) - m_cur)

        @pl.when(jnp.max(excess) > STABILISER_SLACK)
        def _():
            reweigh_chunk(c, bias)

        return n_eq

    def accumulate_chunk(c):
        for g in range(N_KV_HEADS):
            vext = jnp.concatenate([vt_ref[c, g * HEAD_DIM:(g + 1) * HEAD_DIM, :], ones_rows], axis=0)
            acc_ref[g] += _dot(vext, p_ref[g])

    def attn_step(c, n_eq):
        accumulate_chunk(c - 1)
        return weigh_chunk(c, n_eq)

    bias0, n_eq0 = chunk_bias(0, jnp.zeros((1, Q_BLOCK), F32))
    reweigh_chunk(0, bias0)
    lax.fori_loop(1, nch, attn_step, n_eq0)
    accumulate_chunk(nch - 1)

    for g in range(N_KV_HEADS):
        acc = acc_ref[g]
        o = acc[:HEAD_DIM] / acc[HEAD_DIM:HEAD_DIM + 1]
        for hh in range(GROUP):
            col = (g * GROUP + hh) * HEAD_DIM
            o_ref[:, col:col + HEAD_DIM] = o[:, hh * Q_BLOCK:(hh + 1) * Q_BLOCK].T.astype(o_ref.dtype)


def _prompt_attn(qt, iqt, iwt, kb, vt, ikb, *, batch, seq):
    nqb = seq // Q_BLOCK
    nch = seq // KEY_CHUNK
    kvw = N_KV_HEADS * HEAD_DIM
    top_k = min(TOP_K_MAX, seq // 4)
    low = jnp.tril(jnp.ones((TIE_BLOCK, TIE_BLOCK), BF16), k=-1)
    per_qblock = lambda *shape: pl.BlockSpec((1,) + shape, lambda b, i: (b * nqb + i,) + (0,) * len(shape))
    per_batch = lambda width: pl.BlockSpec((seq, width), lambda b, i: (b, 0))
    return pl.pallas_call(
        functools.partial(_prompt_attn_kernel, top_k),
        grid=(batch, nqb),
        in_specs=[per_qblock(N_KV_HEADS, HEAD_DIM, GROUP * Q_BLOCK),
                  per_qblock(IDX_HEADS * IDX_DIM, Q_BLOCK), per_qblock(SUBLANES, Q_BLOCK),
                  per_batch(kvw), pl.BlockSpec((nch, kvw, KEY_CHUNK), lambda b, i: (b, 0, 0)),
                  per_batch(IDX_DIM), _const_spec((TIE_BLOCK, TIE_BLOCK))],
        out_specs=pl.BlockSpec((Q_BLOCK, N_HEADS * HEAD_DIM), lambda b, i: (b * nqb + i, 0)),
        out_shape=jax.ShapeDtypeStruct((batch * seq, N_HEADS * HEAD_DIM), BF16),
        scratch_shapes=[
            pltpu.VMEM((nch, KEY_CHUNK, Q_BLOCK), F32),
            pltpu.VMEM((KEY_BITS, nch * PLANE_ROWS, Q_BLOCK), I32),
            pltpu.VMEM((N_KV_HEADS, 1, GROUP * Q_BLOCK), F32),
            pltpu.VMEM((N_KV_HEADS, HEAD_DIM + MXU_ROWS, GROUP * Q_BLOCK), F32),
            pltpu.VMEM((N_KV_HEADS, KEY_CHUNK, GROUP * Q_BLOCK), BF16),
        ],
        compiler_params=pltpu.CompilerParams(
            dimension_semantics=("arbitrary", "arbitrary"), vmem_limit_bytes=VMEM_LIMIT_BYTES),
        name="prompt_attn",
    )(qt, iqt, iwt, kb, vt, ikb, low)


def _sample_attn_kernel(n_pages, page, top_k, rb, pt_ref, q_ref, iq_ref, iw_ref, kn_ref, vn_ref, ikn_ref,
                        ck_hbm, cv_hbm, cik_hbm, tri_ref, low_ref, o_ref,
                        kbuf, vbuf, ikbuf, sems, sc_ref):
    b = pl.program_id(0)
    nb = pl.num_programs(0)
    past = n_pages * page

    def page_copies(step, slot):
        copies = []
        for r in range(rb):
            for p in range(n_pages):
                phys = pt_ref[step * rb + r, p]
                rows = pl.ds(p * page * N_KV_HEADS, page * N_KV_HEADS)
                lanes = pl.ds(p * page, page)
                copies.append(pltpu.make_async_copy(ck_hbm.at[phys], kbuf.at[slot, r, rows], sems.at[0, slot]))
                copies.append(pltpu.make_async_copy(cv_hbm.at[phys], vbuf.at[slot, r, rows], sems.at[1, slot]))
                copies.append(
                    pltpu.make_async_copy(cik_hbm.at[phys], ikbuf.at[slot, r, :, lanes], sems.at[2, slot]))
        return copies

    slot = b % 2

    @pl.when(b == 0)
    def _():
        for c in page_copies(0, 0):
            c.start()

    @pl.when(b + 1 < nb)
    def _():
        for c in page_copies(b + 1, 1 - slot):
            c.start()

    for c in page_copies(b, slot):
        c.wait()

    sc_rows, sc_news = [], []
    for r in range(rb):
        iq = iq_ref[r].astype(BF16)
        iw = iw_ref[r]
        sidx = _dot(iq, ikbuf[slot, r].astype(BF16))
        sc_row = jnp.sum(jnp.maximum(sidx, 0.0) * iw, axis=0, keepdims=True)
        for p in range(n_pages):
            sc_ref[r, p:p + 1, :] = sc_row[:, p * page:(p + 1) * page]
        ikn = ikn_ref[r].astype(BF16).astype(F32)
        s_new = jnp.sum(iq.astype(F32) * ikn, axis=1, keepdims=True)
        sc_rows.append(sc_row)
        sc_news.append(jnp.sum(jnp.maximum(s_new, 0.0) * iw, axis=0, keepdims=True))

    def count_ge(r, cand):
        hits = jnp.where(sc_rows[r] >= cand, 1.0, 0.0)
        return jnp.sum(hits, axis=1, keepdims=True) + jnp.where(sc_news[r] >= cand, 1.0, 0.0)

    ts = _kth_largest_by_digits([functools.partial(count_ge, r) for r in range(rb)], top_k,
                                SAMPLE_SEARCH_BITS)

    def total(x):
        return jnp.sum(jnp.sum(x, axis=1, keepdims=True), axis=0, keepdims=True)

    for r in range(rb):
        t, sc, sc_new = ts[r], sc_ref[r], sc_news[r]
        n_gt = total(jnp.where(sc > t, 1.0, 0.0)) + jnp.where(sc_new > t, 1.0, 0.0)
        need = float(top_k) - n_gt
        eq = sc == t
        eqf = jnp.where(eq, 1.0, 0.0)
        in_row = _dot(eqf.astype(BF16), tri_ref[...])
        row_tot = jnp.broadcast_to(jnp.sum(eqf, axis=1, keepdims=True), sc.shape)
        rows_before = _dot(low_ref[...], row_tot.astype(BF16))
        sel = (sc > t) | (eq & (in_row + rows_before < need))
        sel_new = (sc_new > t) | ((sc_new == t) & (total(eqf) < need))
        bias = jnp.where(sel, 0.0, NEG_INF)
        bias_row = jnp.concatenate([bias[p:p + 1, :] for p in range(n_pages)], axis=1)
        bias_new = jnp.where(sel_new, 0.0, NEG_INF)

        q = q_ref[r].astype(BF16)
        qf = q.astype(F32)
        for g in range(N_KV_HEADS):
            head_rows = pl.ds(g, past, stride=N_KV_HEADS)
            kg = kbuf[slot, r, head_rows, :].astype(BF16)
            vg = vbuf[slot, r, head_rows, :].astype(BF16)
            kn = kn_ref[r, g:g + 1, :].astype(BF16).astype(F32)
            vn = vn_ref[r, g:g + 1, :].astype(BF16).astype(F32)
            s = _dot_nt(q, kg) + bias_row
            sn = jnp.sum(qf * kn, axis=1, keepdims=True) + bias_new
            m = jnp.maximum(jnp.max(s, axis=1, keepdims=True), sn)
            p = jnp.exp(s - m)
            pn = jnp.exp(sn - m)
            l = jnp.sum(p, axis=1, keepdims=True) + pn
            o = (_dot(p.astype(BF16), vg) + pn * vn) / l
            o_ref[r, g * GROUP:(g + 1) * GROUP, :] = o[g * GROUP:(g + 1) * GROUP]


def _sample_attn(page_table, q, iq, iw, k_new, v_new, ik_new, cache_k, cache_v, cache_ik):
    nreq, n_pages = page_table.shape
    n_phys, page = cache_k.shape[0], cache_k.shape[1]
    top_k = min(TOP_K_MAX, (n_pages * page + 1) // 4)
    assert top_k < n_pages * page + 1
    ck = cache_k.reshape(n_phys, page * N_KV_HEADS, HEAD_DIM)
    cv = cache_v.reshape(n_phys, page * N_KV_HEADS, HEAD_DIM)
    cik = jnp.swapaxes(cache_ik, 1, 2)
    tri = jnp.triu(jnp.ones((page, page), BF16), k=1)
    low = jnp.tril(jnp.ones((n_pages, n_pages), BF16), k=-1)
    rb = SAMPLE_REQS
    assert nreq % rb == 0
    pad_rows = lambda a: jnp.pad(a.astype(F32), ((0, 0), (0, MXU_ROWS - a.shape[1]), (0, 0)))
    per_req = lambda *shape: pl.BlockSpec((rb,) + shape, lambda b, pt: (b,) + (0,) * len(shape))
    const = lambda *shape: pl.BlockSpec(shape, lambda b, pt: (0,) * len(shape))
    any_spec = pl.BlockSpec(memory_space=pl.ANY)
    grid_spec = pltpu.PrefetchScalarGridSpec(
        num_scalar_prefetch=1,
        grid=(nreq // rb,),
        in_specs=[per_req(MXU_ROWS, HEAD_DIM), per_req(MXU_ROWS, IDX_DIM), per_req(MXU_ROWS, 1),
                  per_req(N_KV_HEADS, HEAD_DIM), per_req(N_KV_HEADS, HEAD_DIM), per_req(1, IDX_DIM),
                  any_spec, any_spec, any_spec, const(page, page), const(n_pages, n_pages)],
        out_specs=per_req(N_HEADS, HEAD_DIM),
        scratch_shapes=[
            pltpu.VMEM((2, rb, n_pages * page * N_KV_HEADS, HEAD_DIM), F32),
            pltpu.VMEM((2, rb, n_pages * page * N_KV_HEADS, HEAD_DIM), F32),
            pltpu.VMEM((2, rb, IDX_DIM, n_pages * page), F32),
            pltpu.SemaphoreType.DMA((3, 2)),
            pltpu.VMEM((rb, n_pages, page), F32),
        ],
    )
    out = pl.pallas_call(
        functools.partial(_sample_attn_kernel, n_pages, page, top_k, rb),
        grid_spec=grid_spec,
        out_shape=jax.ShapeDtypeStruct((nreq, N_HEADS, HEAD_DIM), F32),
        compiler_params=pltpu.CompilerParams(
            dimension_semantics=("arbitrary",), vmem_limit_bytes=VMEM_LIMIT_BYTES),
        name="sample_attn",
    )(page_table,
      pad_rows(q.reshape(nreq, N_HEADS, HEAD_DIM)),
      pad_rows(jnp.transpose(iq, (1, 0, 2))),
      pad_rows(iw.reshape(nreq, IDX_HEADS, 1)),
      k_new.reshape(nreq, N_KV_HEADS, HEAD_DIM), v_new.reshape(nreq, N_KV_HEADS, HEAD_DIM),
      ik_new.reshape(nreq, 1, IDX_DIM),
      ck, cv, cik, tri, low)
    return out.reshape(nreq, N_HEADS * HEAD_DIM)


def _merge_stage(x, attn, oa, sgb, wao_ref, wo_ref, gmem_ref, wmq_ref):
    out_b = _dot(attn, wao_ref[...])
    merged = oa.astype(F32) + sgb.astype(F32) * out_b
    x1 = x + _dot(merged.astype(BF16), wo_ref[...])
    hm = _rms(x1, gmem_ref[...]).astype(BF16)
    return x1, _dot(hm, wmq_ref[...]) * MEM_SCALE


def _ffn_stage(x1, mem_o, wmo_ref, gffn_ref, wg_ref, wu_ref, wd_ref, gfin_ref):
    x2 = x1 + _dot(mem_o.astype(BF16), wmo_ref[...])
    hf = _rms(x2, gffn_ref[...]).astype(BF16)
    acc = jnp.zeros_like(x2)
    for c in range(D_FF // FF_CHUNK):
        cols = slice(c * FF_CHUNK, (c + 1) * FF_CHUNK)
        gate = _dot(hf, wg_ref[:, cols])
        f = gate * _sigmoid(gate) * _dot(hf, wu_ref[:, cols])
        acc = acc + _dot(f.astype(BF16), wd_ref[cols, :])
    return _rms(x2 + acc, gfin_ref[...])


def _post_prompt_kernel(x_ref, attn_ref, oa_ref, sgb_ref, mk_ref, mv_ref, wao_ref, wo_ref, gmem_ref,
                        wmq_ref, wmo_ref, gffn_ref, wg_ref, wu_ref, wd_ref, gfin_ref, y_ref, mo_ref):
    x1, qm = _merge_stage(x_ref[...], attn_ref[...], oa_ref[...], sgb_ref[...],
                          wao_ref, wo_ref, gmem_ref, wmq_ref)
    for h in range(MEM_HEADS):
        cols = slice(h * MEM_HEAD_DIM, (h + 1) * MEM_HEAD_DIM)
        s = _dot_nt(qm[:, cols].astype(BF16), mk_ref[:, cols])
        p = jnp.exp(s - jnp.max(s, axis=1, keepdims=True))
        o = _dot(p.astype(BF16), mv_ref[:, cols]) / jnp.sum(p, axis=1, keepdims=True)
        mo_ref[:, cols] = o.astype(BF16)
    y_ref[...] = _ffn_stage(x1, mo_ref[...], wmo_ref, gffn_ref, wg_ref, wu_ref, wd_ref, gfin_ref)


def _post_prompt(x, attn, oa, sgb, mk, mv, w, *, batch, seq, tm):
    nt = seq // tm
    n_mem = mk.shape[0] // batch
    tok = lambda width: pl.BlockSpec((tm, width), lambda b, t: (b * nt + t, 0))
    mem = pl.BlockSpec((n_mem, D_MODEL), lambda b, t: (b, 0))
    sq = _const_spec((D_MODEL, D_MODEL))
    gain = _const_spec((1, D_MODEL))
    return pl.pallas_call(
        _post_prompt_kernel,
        grid=(batch, nt),
        in_specs=[tok(D_MODEL), tok(D_MODEL), tok(D_MODEL), tok(D_MODEL), mem, mem,
                  sq, sq, gain, sq, sq, gain,
                  _const_spec((D_MODEL, D_FF)), _const_spec((D_MODEL, D_FF)), _const_spec((D_FF, D_MODEL)),
                  gain],
        out_specs=tok(D_MODEL),
        out_shape=jax.ShapeDtypeStruct((batch * seq, D_MODEL), F32),
        scratch_shapes=[pltpu.VMEM((tm, D_MODEL), BF16)],
        compiler_params=pltpu.CompilerParams(
            dimension_semantics=("arbitrary", "arbitrary"), vmem_limit_bytes=VMEM_LIMIT_BYTES),
        name="post_prompt",
    )(x, attn, oa, sgb, mk, mv, w["wao"], w["wo"], w["gmem"], w["wmq"], w["wmo"], w["gffn"],
      w["wg"], w["wu"], w["wd"], w["gfin"])


def _merge_sample_kernel(x_ref, attn_ref, oa_ref, sgb_ref, wao_ref, wo_ref, gmem_ref, wmq_ref,
                         x1_ref, qm_ref):
    x1, qm = _merge_stage(x_ref[...], attn_ref[...].astype(BF16), oa_ref[...], sgb_ref[...],
                          wao_ref, wo_ref, gmem_ref, wmq_ref)
    x1_ref[...] = x1
    qm_ref[...] = qm


def _mem_sample_kernel(rb, q_ref, mk_ref, mv_ref, o_ref):
    for r in range(rb):
        s = jnp.sum(mk_ref[r] * q_ref[r][None], axis=2, keepdims=True)
        p = jnp.exp(s - jnp.max(s, axis=0, keepdims=True))
        o_ref[r] = jnp.sum(p * mv_ref[r], axis=0) / jnp.sum(p, axis=0)


def _ffn_sample_kernel(x1_ref, mo_ref, wmo_ref, gffn_ref, wg_ref, wu_ref, wd_ref, gfin_ref, y_ref):
    y_ref[...] = _ffn_stage(x1_ref[...], mo_ref[...], wmo_ref, gffn_ref, wg_ref, wu_ref, wd_ref, gfin_ref)


def _post_sample(x, attn, oa, sgb, mem_k, mem_v, w):
    n = x.shape[0]
    n_mem = mem_k.shape[1]
    full = lambda *shape: _const_spec(shape)
    act = full(n, D_MODEL)
    sq = full(D_MODEL, D_MODEL)
    gain = full(1, D_MODEL)
    params = pltpu.CompilerParams(dimension_semantics=("arbitrary",), vmem_limit_bytes=VMEM_LIMIT_BYTES)
    x1, qm = pl.pallas_call(
        _merge_sample_kernel, grid=(1,),
        in_specs=[act, act, act, act, sq, sq, gain, sq], out_specs=[act, act],
        out_shape=[jax.ShapeDtypeStruct((n, D_MODEL), F32)] * 2,
        compiler_params=params, name="merge_sample",
    )(x, attn, oa, sgb, w["wao"], w["wo"], w["gmem"], w["wmq"])

    rb = MEM_REQS
    assert n % rb == 0
    heads = pl.BlockSpec((rb, MEM_HEADS, MEM_HEAD_DIM), lambda i: (i, 0, 0))
    mem = pl.BlockSpec((rb, n_mem, MEM_HEADS, MEM_HEAD_DIM), lambda i: (i, 0, 0, 0))
    mem_o = pl.pallas_call(
        functools.partial(_mem_sample_kernel, rb), grid=(n // rb,),
        in_specs=[heads, mem, mem], out_specs=heads,
        out_shape=jax.ShapeDtypeStruct((n, MEM_HEADS, MEM_HEAD_DIM), F32),
        compiler_params=params, name="mem_sample",
    )(qm.reshape(n, MEM_HEADS, MEM_HEAD_DIM), mem_k, mem_v)

    return pl.pallas_call(
        _ffn_sample_kernel, grid=(1,),
        in_specs=[act, act, sq, gain, full(D_MODEL, D_FF), full(D_MODEL, D_FF), full(D_FF, D_MODEL), gain],
        out_specs=act, out_shape=jax.ShapeDtypeStruct((n, D_MODEL), F32),
        compiler_params=params, name="ffn_sample",
    )(x1, mem_o.reshape(n, D_MODEL), w["wmo"], w["gffn"], w["wg"], w["wu"], w["wd"], w["gfin"])


def _memory_kv_kernel(mem_ref, g_ref, wk_ref, wv_ref, kf_ref, vf_ref, kb_ref, vb_ref):
    m = _rms(mem_ref[...], g_ref[...]).astype(BF16)
    k = _dot(m, wk_ref[...])
    v = _dot(m, wv_ref[...])
    kf_ref[...] = k
    vf_ref[...] = v
    kb_ref[...] = k.astype(BF16)
    vb_ref[...] = v.astype(BF16)


def _memory_kv(mem, g, wk, wv):
    n = mem.shape[0]
    full = lambda *shape: _const_spec(shape)
    act = full(n, D_MODEL)
    sq = full(D_MODEL, D_MODEL)
    return pl.pallas_call(
        _memory_kv_kernel, grid=(1,),
        in_specs=[act, full(1, D_MODEL), sq, sq], out_specs=[act] * 4,
        out_shape=[jax.ShapeDtypeStruct((n, D_MODEL), F32)] * 2 + [jax.ShapeDtypeStruct((n, D_MODEL), BF16)] * 2,
        compiler_params=pltpu.CompilerParams(
            dimension_semantics=("arbitrary",), vmem_limit_bytes=VMEM_LIMIT_BYTES),
        name="memory_kv",
    )(mem, g, wk, wv)


def _prep_w_in(w_in):
    w_t = w_in.T
    main = jnp.pad(w_t[:IDX_RAW_END].astype(BF16), ((0, W_COLS - IDX_RAW_END), (0, 0)))
    return main, w_t[IDX_RAW_END:].astype(BF16)


def kernel(x_prompt, x_sample, mem_prompt, cache_k, cache_v, cache_idx_k, cache_mem_k, cache_mem_v, state_conv, page_table, g_mix, w_in, conv_w, w_conv_out, w_attn_out, w_o, g_mem, g_mem_kv, w_mq, w_mk, w_mv, w_mo, g_ffn, w_gate, w_up, w_down, g_final):
    depth = w_in.shape[0]
    assert depth == 1, "single-layer step"
    batch, seq, _ = x_prompt.shape
    nreq, dec_seq, _ = x_sample.shape
    assert dec_seq == 1
    n_mem = mem_prompt.shape[1]
    l = 0
    bf = lambda a: a.astype(BF16)
    row = lambda a: a.reshape(1, -1)

    w_main, w_gates = _prep_w_in(w_in[l])
    wco = bf(w_conv_out[l])
    w = dict(wao=bf(w_attn_out[l]), wo=bf(w_o[l]), gmem=row(g_mem[l]), wmq=bf(w_mq[l]), wmo=bf(w_mo[l]),
             gffn=row(g_ffn[l]), wg=bf(w_gate[l]), wu=bf(w_up[l]), wd=bf(w_down[l]), gfin=row(g_final))

    xp = x_prompt.reshape(batch * seq, D_MODEL)
    mkf, mvf, mkb, mvb = _memory_kv(mem_prompt.reshape(batch * n_mem, D_MODEL), row(g_mem_kv[l]),
                                    bf(w_mk[l]), bf(w_mv[l]))
    (kf, vf, ikf, oa, sgb, qt, kb, vt, iqt, ikb, iwt, conv_p) = _mixer_in(
        xp, row(g_mix[l]), w_main, w_gates, conv_w[l], wco, batch=batch, seq=seq, tm=KEY_CHUNK)
    attn = _prompt_attn(qt, iqt, iwt, kb, vt, ikb, batch=batch, seq=seq)
    yp = _post_prompt(xp, attn, oa, sgb, mkb, mvb, w, batch=batch, seq=seq, tm=POST_TILE)

    xs = x_sample.reshape(nreq, D_MODEL)
    st = state_conv[l]
    (kf_s, vf_s, ikf_s, oa_s, sgb_s, q_s, iq_s, iw_s, u_s) = _mixer_in(
        xs, row(g_mix[l]), w_main, w_gates, conv_w[l], wco, batch=nreq, seq=1, tm=nreq, state=(st[:, 0], st[:, 1]))
    attn_s = _sample_attn(page_table, q_s, iq_s, iw_s, kf_s, vf_s, ikf_s,
                          cache_k[l], cache_v[l], cache_idx_k[l])
    ys = _post_sample(xs, attn_s, oa_s, sgb_s, cache_mem_k[l], cache_mem_v[l], w)

    return (
        yp.reshape(batch, seq, D_MODEL),
        ys.reshape(nreq, 1, D_MODEL),
        kf.reshape(1, batch, seq, N_KV_HEADS, HEAD_DIM),
        vf.reshape(1, batch, seq, N_KV_HEADS, HEAD_DIM),
        ikf.reshape(1, batch, seq, IDX_DIM),
        conv_p.reshape(1, batch, CONV_WIDTH - 1, D_CONV),
        mkf.reshape(1, batch, n_mem, MEM_HEADS, MEM_HEAD_DIM),
        mvf.reshape(1, batch, n_mem, MEM_HEADS, MEM_HEAD_DIM),
        kf_s.reshape(1, nreq, 1, N_KV_HEADS, HEAD_DIM),
        vf_s.reshape(1, nreq, 1, N_KV_HEADS, HEAD_DIM),
        ikf_s.reshape(1, nreq, 1, IDX_DIM),
        jnp.stack([st[:, 1], u_s], axis=1).reshape(1, nreq, CONV_WIDTH - 1, D_CONV),
    )
```

```python
import functools

import jax
import jax.numpy as jnp
from jax import lax
from jax.experimental import pallas as pl
from jax.experimental.pallas import tpu as pltpu

F32 = jnp.float32
BF16 = jnp.bfloat16
I32 = jnp.int32

D_MODEL = 1024
D_CONV = 1024
CONV_WIDTH = 3
N_HEADS = 8
HEAD_DIM = 128
N_KV_HEADS = 2
GROUP = N_HEADS // N_KV_HEADS
IDX_HEADS = 4
IDX_DIM = 64
TOP_K_MAX = 256
MEM_HEADS = 4
MEM_HEAD_DIM = D_MODEL // MEM_HEADS
D_FF = 2816
EPS = 1e-6
NEG_INF = -1e30
IDX_SCALE = (IDX_HEADS * IDX_DIM) ** -0.5
ATTN_SCALE = HEAD_DIM ** -0.5
MEM_SCALE = MEM_HEAD_DIM ** -0.5
LOG2_E = 1.4426950408889634

LANES = 128
SUBLANES = 8
MXU_ROWS = 16
VMEM_LIMIT_BYTES = 56 * 1024 * 1024

COL_CIN = 0
COL_CB = COL_CIN + D_CONV
COL_CC = COL_CB + D_CONV
COL_Q = COL_CC + D_CONV
COL_K = COL_Q + N_HEADS * HEAD_DIM
COL_V = COL_K + N_KV_HEADS * HEAD_DIM
COL_IQ = COL_V + N_KV_HEADS * HEAD_DIM
COL_IK = COL_IQ + IDX_HEADS * IDX_DIM
COL_IW = COL_IK + IDX_DIM
IDX_RAW_END = COL_IW + IDX_HEADS
W_COLS = -(-IDX_RAW_END // LANES) * LANES

Q_BLOCK = 256
KEY_CHUNK = 512
COUNT_ROWS = 32
KEY_BITS = 32
PLANE_ROWS = KEY_CHUNK // KEY_BITS
TIE_BLOCK = 128
SAMPLE_SEARCH_BITS = 4
SAMPLE_REQS = 4
MEM_REQS = 4
FF_CHUNK = D_FF // 11
POST_TILE = 512
STABILISER_SLACK = 64.0
INT_MIN = -2 ** 31


def _rms(x, g):
    return x * lax.rsqrt(jnp.mean(x * x, axis=-1, keepdims=True) + EPS) * g


def _dot(a, b):
    return jnp.dot(a, b, preferred_element_type=F32)


def _dot_nt(a, b):
    return lax.dot_general(a, b, (((1,), (1,)), ((), ())), preferred_element_type=F32)


def _sigmoid(x):
    return 1.0 / (1.0 + jnp.exp(-x))


def _key_to_float(ukey):
    skey = ukey ^ INT_MIN
    bits = jnp.where(skey < 0, skey ^ 0x7FFFFFFF, skey)
    return lax.bitcast_convert_type(bits, F32)


def _float_to_key(x):
    bits = lax.bitcast_convert_type(x, I32)
    return bits ^ (lax.shift_right_arithmetic(bits, jnp.full_like(bits, 31)) & 0x7FFFFFFF) ^ INT_MIN


def _kth_largest(count_ge, top_k, shape):
    def body(b, ukey):
        cand = ukey | jnp.left_shift(jnp.int32(1), KEY_BITS - 1 - b)
        ok = count_ge(_key_to_float(cand)) >= float(top_k)
        return jnp.where(ok, cand, ukey)

    ukey = lax.fori_loop(0, KEY_BITS, body, jnp.zeros(shape, I32))
    return _key_to_float(ukey)


def _kth_largest_by_digits(count_ge_fns, top_k, bits):
    digits = jnp.minimum(lax.broadcasted_iota(I32, (2 ** bits, 1), 0) + 1, 2 ** bits - 1)
    digits_f = digits.astype(F32)

    def body(rd, ukeys):
        shift = KEY_BITS - bits * (rd + 1)
        out = []
        for count_ge, ukey in zip(count_ge_fns, ukeys):
            cand = ukey | jnp.left_shift(digits, shift)
            ok = count_ge(_key_to_float(cand)) >= float(top_k)
            digit = jnp.max(jnp.where(ok, digits_f, 0.0), axis=0, keepdims=True).astype(I32)
            out.append(ukey | jnp.left_shift(digit, shift))
        return tuple(out)

    ukeys = lax.fori_loop(0, KEY_BITS // bits, body, tuple(jnp.zeros((1, 1), I32) for _ in count_ge_fns))
    return [_key_to_float(u) for u in ukeys]


def _bit_transpose32(words):
    a = list(words)
    j, m = 16, 0x0000FFFF
    while j:
        k = 0
        while k < 32:
            t = (a[k] ^ lax.shift_right_logical(a[k + j], jnp.full_like(a[k], j))) & m
            a[k] = a[k] ^ t
            a[k + j] = a[k + j] ^ lax.shift_left(t, jnp.full_like(t, j))
            k = (k + j + 1) & ~j
        j >>= 1
        m = (m ^ (m << j)) & 0xFFFFFFFF
    return a


def _radix_select(planes_ref, live, top_k):
    rows = live.shape[0]

    def body(i, carry):
        live, ukey, n_above = carry
        hit = live & planes_ref[i, :rows]
        cnt = jnp.sum(lax.population_count(hit).astype(F32), axis=0, keepdims=True)
        take = n_above + cnt >= float(top_k)
        live = jnp.where(take, hit, live ^ hit)
        ukey = jnp.where(take, ukey | jnp.left_shift(jnp.int32(1), KEY_BITS - 1 - i), ukey)
        n_above = jnp.where(take, n_above, n_above + cnt)
        return live, ukey, n_above

    lanes = live.shape[1]
    init = (live, jnp.zeros((1, lanes), I32), jnp.zeros((1, lanes), F32))
    return lax.fori_loop(0, KEY_BITS, body, init)[1]


def _mixer_in_kernel(is_prompt, tm, *refs):
    if is_prompt:
        (x_ref, g_ref, w_ref, wgate_ref, cw_ref, wco_ref,
         kf_ref, vf_ref, ikf_ref, oa_ref, sgb_ref,
         qt_ref, kb_ref, vt_ref, iqt_ref, ikb_ref, iwt_ref, cs_ref, ubuf) = refs
    else:
        (x_ref, g_ref, w_ref, wgate_ref, cw_ref, wco_ref, s0_ref, s1_ref,
         kf_ref, vf_ref, ikf_ref, oa_ref, sgb_ref,
         q_ref, iq_ref, iw_ref, u_ref) = refs

    h = _rms(x_ref[...], g_ref[...]).astype(BF16)

    def proj(lo, hi):
        return _dot_nt(h, w_ref[lo:hi, :])

    u = proj(COL_CC, COL_CC + D_CONV) * proj(COL_CIN, COL_CIN + D_CONV)
    cw = cw_ref[...]
    if is_prompt:
        @pl.when(pl.program_id(1) == 0)
        def _():
            ubuf[0:SUBLANES, :] = jnp.zeros((SUBLANES, D_CONV), F32)

        ubuf[SUBLANES:SUBLANES + tm, :] = u
        conv = (ubuf[SUBLANES - 2:SUBLANES - 2 + tm, :] * cw[0:1]
                + ubuf[SUBLANES - 1:SUBLANES - 1 + tm, :] * cw[1:2] + u * cw[2:3])
        ubuf[0:SUBLANES, :] = ubuf[tm:tm + SUBLANES, :]
        cs_ref[0] = u[tm - (CONV_WIDTH - 1):, :]
    else:
        conv = s0_ref[...] * cw[0:1] + s1_ref[...] * cw[1:2] + u * cw[2:3]
        u_ref[...] = u

    a_in = (proj(COL_CB, COL_CB + D_CONV) * conv).astype(BF16)
    out_a = _dot(a_in, wco_ref[...])
    oa_ref[...] = (_sigmoid(_dot_nt(h, wgate_ref[:D_MODEL, :])) * out_a).astype(BF16)
    sgb_ref[...] = _sigmoid(_dot_nt(h, wgate_ref[D_MODEL:, :])).astype(BF16)

    q = proj(COL_Q, COL_K) * (ATTN_SCALE * LOG2_E if is_prompt else ATTN_SCALE)
    kv = proj(COL_K, COL_IQ)
    k = kv[:, :N_KV_HEADS * HEAD_DIM]
    v = kv[:, N_KV_HEADS * HEAD_DIM:]
    for g in range(N_KV_HEADS):
        head_rows = pl.ds(g, tm, stride=N_KV_HEADS)
        kf_ref[head_rows, :] = k[:, g * HEAD_DIM:(g + 1) * HEAD_DIM]
        vf_ref[head_rows, :] = v[:, g * HEAD_DIM:(g + 1) * HEAD_DIM]
    idx = proj(COL_IQ, W_COLS)
    iq = idx[:, :IDX_HEADS * IDX_DIM]
    ikw = idx[:, COL_IK - COL_IQ:]
    ik = ikw[:, :IDX_DIM]
    ikf_ref[...] = ik
    if is_prompt:
        kb_ref[...] = k.astype(BF16)
        ikb_ref[...] = ik.astype(BF16)
        vt_ref[0] = v.T.astype(BF16)
        for j in range(tm // Q_BLOCK):
            rows = slice(j * Q_BLOCK, (j + 1) * Q_BLOCK)
            for head in range(N_HEADS):
                g, hh = divmod(head, GROUP)
                qt_ref[j, g, :, hh * Q_BLOCK:(hh + 1) * Q_BLOCK] = (
                    q[rows, head * HEAD_DIM:(head + 1) * HEAD_DIM].T.astype(BF16))
            iqt_ref[j] = iq[rows].T.astype(BF16)
            iwt_ref[j] = ikw[rows].T[IDX_DIM:IDX_DIM + SUBLANES] * IDX_SCALE
    else:
        q_ref[...] = q.astype(BF16)
        for hd in range(IDX_HEADS):
            iq_ref[hd] = iq[:, hd * IDX_DIM:(hd + 1) * IDX_DIM].astype(BF16)
        iw_ref[...] = ikw[:, IDX_DIM:IDX_DIM + IDX_HEADS] * IDX_SCALE


def _const_spec(shape):
    nd = len(shape)
    return pl.BlockSpec(shape, lambda *_: (0,) * nd, pipeline_mode=pl.Buffered(1))


def _mixer_in(x, g_mix, w_main, w_gates, conv_w, w_conv_out, *, batch, seq, tm, state=None):
    is_prompt = state is None
    t_all = batch * seq
    nt = seq // tm if is_prompt else 1
    grid = (batch, nt) if is_prompt else (1, 1)
    tok = lambda width: pl.BlockSpec((tm, width), lambda b, t: (b * nt + t, 0))
    in_specs = [tok(D_MODEL), _const_spec((1, D_MODEL)), _const_spec((W_COLS, D_MODEL)), _const_spec((2 * D_MODEL, D_MODEL)),
                _const_spec((CONV_WIDTH, D_CONV)), _const_spec((D_CONV, D_MODEL))]
    args = [x, g_mix, w_main, w_gates, conv_w, w_conv_out]
    kvw = N_KV_HEADS * HEAD_DIM
    out_shapes = [
        jax.ShapeDtypeStruct((t_all * N_KV_HEADS, HEAD_DIM), F32),
        jax.ShapeDtypeStruct((t_all * N_KV_HEADS, HEAD_DIM), F32),
        jax.ShapeDtypeStruct((t_all, IDX_DIM), F32),
        jax.ShapeDtypeStruct((t_all, D_MODEL), BF16),
        jax.ShapeDtypeStruct((t_all, D_MODEL), BF16),
    ]
    kv_rows = pl.BlockSpec((tm * N_KV_HEADS, HEAD_DIM), lambda b, t: (b * nt + t, 0))
    out_specs = [kv_rows, kv_rows] + [tok(s.shape[1]) for s in out_shapes[2:]]
    scratch = []
    if is_prompt:
        assert tm == KEY_CHUNK and tm % Q_BLOCK == 0
        qb = tm // Q_BLOCK
        nblk = t_all // Q_BLOCK
        per_qblock = lambda *shape: pl.BlockSpec((qb,) + shape, lambda b, t: (b * nt + t,) + (0,) * len(shape))
        out_shapes += [
            jax.ShapeDtypeStruct((nblk, N_KV_HEADS, HEAD_DIM, GROUP * Q_BLOCK), BF16),
            jax.ShapeDtypeStruct((t_all, kvw), BF16),
            jax.ShapeDtypeStruct((t_all // tm, kvw, tm), BF16),
            jax.ShapeDtypeStruct((nblk, IDX_HEADS * IDX_DIM, Q_BLOCK), BF16),
            jax.ShapeDtypeStruct((t_all, IDX_DIM), BF16),
            jax.ShapeDtypeStruct((nblk, SUBLANES, Q_BLOCK), F32),
            jax.ShapeDtypeStruct((batch, CONV_WIDTH - 1, D_CONV), F32),
        ]
        out_specs += [
            per_qblock(N_KV_HEADS, HEAD_DIM, GROUP * Q_BLOCK), tok(kvw),
            pl.BlockSpec((1, kvw, tm), lambda b, t: (b * nt + t, 0, 0)),
            per_qblock(IDX_HEADS * IDX_DIM, Q_BLOCK), tok(IDX_DIM), per_qblock(SUBLANES, Q_BLOCK),
            pl.BlockSpec((1, CONV_WIDTH - 1, D_CONV), lambda b, t: (b, 0, 0)),
        ]
        scratch.append(pltpu.VMEM((tm + SUBLANES, D_CONV), F32))
    else:
        in_specs += [tok(D_CONV), tok(D_CONV)]
        args += list(state)
        out_shapes += [
            jax.ShapeDtypeStruct((t_all, N_HEADS * HEAD_DIM), BF16),
            jax.ShapeDtypeStruct((IDX_HEADS, t_all, IDX_DIM), BF16),
            jax.ShapeDtypeStruct((t_all, IDX_HEADS), F32),
            jax.ShapeDtypeStruct((t_all, D_CONV), F32),
        ]
        out_specs += [tok(N_HEADS * HEAD_DIM),
                      pl.BlockSpec((IDX_HEADS, tm, IDX_DIM), lambda b, t: (0, b * nt + t, 0)),
                      tok(IDX_HEADS), tok(D_CONV)]
    return pl.pallas_call(
        functools.partial(_mixer_in_kernel, is_prompt, tm),
        grid=grid, in_specs=in_specs, out_specs=out_specs, out_shape=out_shapes,
        scratch_shapes=scratch,
        compiler_params=pltpu.CompilerParams(
            dimension_semantics=("arbitrary", "arbitrary"), vmem_limit_bytes=VMEM_LIMIT_BYTES),
        name="mixer_in_prompt" if is_prompt else "mixer_in_sample",
    )(*args)


def _prompt_attn_kernel(top_k, qt_ref, iqt_ref, iwt_ref, k_ref, vt_ref, ik_ref, low_ref, o_ref,
                        sc_ref, planes_ref, m_ref, acc_ref, p_ref):
    i = pl.program_id(1)
    nch = i // (KEY_CHUNK // Q_BLOCK) + 1
    qpos = i * Q_BLOCK + lax.broadcasted_iota(I32, (1, Q_BLOCK), 1)

    @pl.when(i == 0)
    def _():
        planes_ref[...] = jnp.zeros(planes_ref.shape, I32)

    iw = iwt_ref[0]
    iqt = iqt_ref[0]
    iq_pairs = [jnp.concatenate([iqt[h * IDX_DIM:(h + 1) * IDX_DIM] for h in (2 * pr, 2 * pr + 1)], axis=1)
                for pr in range(IDX_HEADS // 2)]

    def score_chunk(c, diagonal):
        off = pl.multiple_of(c * KEY_CHUNK, KEY_CHUNK)
        ikc = ik_ref[pl.ds(off, KEY_CHUNK), :]
        acc = jnp.zeros((KEY_CHUNK, Q_BLOCK), F32)
        for pr in range(IDX_HEADS // 2):
            s2 = _dot(ikc, iq_pairs[pr])
            for e in range(2):
                h = 2 * pr + e
                acc = acc + jnp.maximum(s2[:, e * Q_BLOCK:(e + 1) * Q_BLOCK], 0.0) * iw[h:h + 1]
        if diagonal:
            kpos = off + lax.broadcasted_iota(I32, (KEY_CHUNK, Q_BLOCK), 0)
            acc = jnp.where(kpos <= qpos, acc, -jnp.inf)
        sc_ref[c] = acc

    def plane_chunk(c):
        keys = _float_to_key(sc_ref[c])
        for wd in range(PLANE_ROWS // SUBLANES):
            words = [keys[(wd * KEY_BITS + j) * SUBLANES:(wd * KEY_BITS + j + 1) * SUBLANES]
                     for j in range(KEY_BITS)]
            row = pl.multiple_of(c * PLANE_ROWS + wd * SUBLANES, SUBLANES)
            for b, plane in enumerate(_bit_transpose32(words)):
                planes_ref[b, pl.ds(row, SUBLANES), :] = plane

    assert KEY_CHUNK % Q_BLOCK == 0
    score_chunk(nch - 1, True)

    def score_step(c, carry):
        plane_chunk(jnp.where(c == 0, nch - 1, c - 1))
        score_chunk(c, False)
        return carry

    lax.fori_loop(0, nch - 1, score_step, 0)
    plane_chunk(jnp.maximum(nch - 2, 0))

    def count(cmps, t):
        def body(c, accs):
            blk = sc_ref[c]
            hits = [jnp.where(cmp(blk, t), 1.0, 0.0) for cmp in cmps]
            return tuple(acc + jnp.sum(hit.reshape(KEY_CHUNK // COUNT_ROWS, COUNT_ROWS, Q_BLOCK), axis=0)
                         for acc, hit in zip(accs, hits))

        accs = lax.fori_loop(0, nch, body, tuple(jnp.zeros((COUNT_ROWS, Q_BLOCK), F32) for _ in cmps))
        return tuple(jnp.sum(acc, axis=0, keepdims=True) for acc in accs)

    ge = lambda a, b: a >= b
    gt = lambda a, b: a > b
    few = qpos < top_k
    def select(rows):
        plane_row = lax.broadcasted_iota(I32, (rows, Q_BLOCK), 0)
        return _radix_select(planes_ref, jnp.where(plane_row < nch * PLANE_ROWS, -1, 0), top_k)

    half_rows = planes_ref.shape[1] // 2
    t = _key_to_float(lax.cond(nch * PLANE_ROWS <= half_rows,
                               lambda: select(half_rows), lambda: select(2 * half_rows)))
    n_gt, n_ge = count((gt,), t)[0], count((ge,), t)[0]
    is_kth = few | ((n_gt < float(top_k)) & (n_ge >= float(top_k)))

    def recount():
        t2 = _kth_largest(lambda cand: count((ge,), cand)[0], top_k, (1, Q_BLOCK))
        return t2, count((gt,), t2)[0]

    t, n_gt = lax.cond(jnp.min(jnp.where(is_kth, 1.0, 0.0)) > 0.0, lambda: (t, n_gt), recount)
    t = jnp.where(few, -jnp.inf, t)
    need = jnp.where(few, 0.0, float(top_k) - n_gt)

    m_ref[...] = jnp.full(m_ref.shape, NEG_INF, F32)
    acc_ref[...] = jnp.zeros(acc_ref.shape, F32)
    ones_rows = jnp.ones((MXU_ROWS, KEY_CHUNK), BF16)

    heads = [(g, slice(hh * Q_BLOCK, (hh + 1) * Q_BLOCK)) for g in range(N_KV_HEADS) for hh in range(GROUP)]

    def chunk_bias(c, n_eq):
        blk = sc_ref[c]
        eq = blk == t
        eqf = jnp.where(eq, 1.0, 0.0)
        eqb = eqf.astype(BF16)
        before = []
        for blk_i in range(KEY_CHUNK // TIE_BLOCK):
            rows = slice(blk_i * TIE_BLOCK, (blk_i + 1) * TIE_BLOCK)
            before.append(n_eq + _dot(low_ref[...], eqb[rows]))
            n_eq = n_eq + jnp.sum(eqf[rows], axis=0, keepdims=True)
        sel = (blk > t) | (eq & (jnp.concatenate(before, axis=0) < need))
        return jnp.where(sel, 0.0, NEG_INF), n_eq

    def head_logits(c, g, lanes, bias):
        off = pl.multiple_of(c * KEY_CHUNK, KEY_CHUNK)
        kc = k_ref[pl.ds(off, KEY_CHUNK), g * HEAD_DIM:(g + 1) * HEAD_DIM]
        return _dot(kc, qt_ref[0, g, :, lanes]) + bias

    def reweigh_chunk(c, bias):
        for g, lanes in heads:
            sh = head_logits(c, g, lanes, bias)
            m_old = m_ref[g, :, lanes]
            m_new = jnp.maximum(m_old, jnp.max(sh, axis=0, keepdims=True))
            acc_ref[g, :, lanes] = jnp.exp2(m_old - m_new) * acc_ref[g, :, lanes]
            p_ref[g, :, lanes] = jnp.exp2(sh - m_new).astype(BF16)
            m_ref[g, :, lanes] = m_new

    def weigh_chunk(c, n_eq):
        bias, n_eq = chunk_bias(c, n_eq)
        excess = jnp.full((1, Q_BLOCK), -jnp.inf, F32)
        for g, lanes in heads:
            sh = head_logits(c, g, lanes, bias)
            m_cur = m_ref[g, :, lanes]
            p_ref[g, :, lanes] = jnp.exp2(sh - m_cur).astype(BF16)
            excess = jnp.maximum(excess, jnp.max(sh, axis=0, keepdims=True) - m_cur)

        @pl.when(jnp.max(excess) > STABILISER_SLACK)
        def _():
            reweigh_chunk(c, bias)

        return n_eq

    def accumulate_chunk(c):
        for g in range(N_KV_HEADS):
            vext = jnp.concatenate([vt_ref[c, g * HEAD_DIM:(g + 1) * HEAD_DIM, :], ones_rows], axis=0)
            acc_ref[g] += _dot(vext, p_ref[g])

    def attn_step(c, n_eq):
        accumulate_chunk(c - 1)
        return weigh_chunk(c, n_eq)

    bias0, n_eq0 = chunk_bias(0, jnp.zeros((1, Q_BLOCK), F32))
    reweigh_chunk(0, bias0)
    lax.fori_loop(1, nch, attn_step, n_eq0)
    accumulate_chunk(nch - 1)

    for g in range(N_KV_HEADS):
        acc = acc_ref[g]
        o = acc[:HEAD_DIM] / acc[HEAD_DIM:HEAD_DIM + 1]
        for hh in range(GROUP):
            col = (g * GROUP + hh) * HEAD_DIM
            o_ref[:, col:col + HEAD_DIM] = o[:, hh * Q_BLOCK:(hh + 1) * Q_BLOCK].T.astype(o_ref.dtype)


def _prompt_attn(qt, iqt, iwt, kb, vt, ikb, *, batch, seq):
    nqb = seq // Q_BLOCK
    nch = seq // KEY_CHUNK
    kvw = N_KV_HEADS * HEAD_DIM
    top_k = min(TOP_K_MAX, seq // 4)
    low = jnp.tril(jnp.ones((TIE_BLOCK, TIE_BLOCK), BF16), k=-1)
    per_qblock = lambda *shape: pl.BlockSpec((1,) + shape, lambda b, i: (b * nqb + i,) + (0,) * len(shape))
    per_batch = lambda width: pl.BlockSpec((seq, width), lambda b, i: (b, 0))
    return pl.pallas_call(
        functools.partial(_prompt_attn_kernel, top_k),
        grid=(batch, nqb),
        in_specs=[per_qblock(N_KV_HEADS, HEAD_DIM, GROUP * Q_BLOCK),
                  per_qblock(IDX_HEADS * IDX_DIM, Q_BLOCK), per_qblock(SUBLANES, Q_BLOCK),
                  per_batch(kvw), pl.BlockSpec((nch, kvw, KEY_CHUNK), lambda b, i: (b, 0, 0)),
                  per_batch(IDX_DIM), _const_spec((TIE_BLOCK, TIE_BLOCK))],
        out_specs=pl.BlockSpec((Q_BLOCK, N_HEADS * HEAD_DIM), lambda b, i: (b * nqb + i, 0)),
        out_shape=jax.ShapeDtypeStruct((batch * seq, N_HEADS * HEAD_DIM), BF16),
        scratch_shapes=[
            pltpu.VMEM((nch, KEY_CHUNK, Q_BLOCK), F32),
            pltpu.VMEM((KEY_BITS, nch * PLANE_ROWS, Q_BLOCK), I32),
            pltpu.VMEM((N_KV_HEADS, 1, GROUP * Q_BLOCK), F32),
            pltpu.VMEM((N_KV_HEADS, HEAD_DIM + MXU_ROWS, GROUP * Q_BLOCK), F32),
            pltpu.VMEM((N_KV_HEADS, KEY_CHUNK, GROUP * Q_BLOCK), BF16),
        ],
        compiler_params=pltpu.CompilerParams(
            dimension_semantics=("arbitrary", "arbitrary"), vmem_limit_bytes=VMEM_LIMIT_BYTES),
        name="prompt_attn",
    )(qt, iqt, iwt, kb, vt, ikb, low)


def _sample_attn_kernel(n_pages, page, top_k, rb, pt_ref, q_ref, iq_ref, iw_ref, kn_ref, vn_ref, ikn_ref,
                        ck_hbm, cv_hbm, cik_hbm, tri_ref, low_ref, o_ref,
                        kbuf, vbuf, ikbuf, sems, sc_ref):
    b = pl.program_id(0)
    nb = pl.num_programs(0)
    past = n_pages * page

    def page_copies(step, slot):
        copies = []
        for r in range(rb):
            for p in range(n_pages):
                phys = pt_ref[step * rb + r, p]
                rows = pl.ds(p * page * N_KV_HEADS, page * N_KV_HEADS)
                lanes = pl.ds(p * page, page)
                copies.append(pltpu.make_async_copy(ck_hbm.at[phys], kbuf.at[slot, r, rows], sems.at[0, slot]))
                copies.append(pltpu.make_async_copy(cv_hbm.at[phys], vbuf.at[slot, r, rows], sems.at[1, slot]))
                copies.append(
                    pltpu.make_async_copy(cik_hbm.at[phys], ikbuf.at[slot, r, :, lanes], sems.at[2, slot]))
        return copies

    slot = b % 2

    @pl.when(b == 0)
    def _():
        for c in page_copies(0, 0):
            c.start()

    @pl.when(b + 1 < nb)
    def _():
        for c in page_copies(b + 1, 1 - slot):
            c.start()

    for c in page_copies(b, slot):
        c.wait()

    sc_rows, sc_news = [], []
    for r in range(rb):
        iq = iq_ref[r].astype(BF16)
        iw = iw_ref[r]
        sidx = _dot(iq, ikbuf[slot, r].astype(BF16))
        sc_row = jnp.sum(jnp.maximum(sidx, 0.0) * iw, axis=0, keepdims=True)
        for p in range(n_pages):
            sc_ref[r, p:p + 1, :] = sc_row[:, p * page:(p + 1) * page]
        ikn = ikn_ref[r].astype(BF16).astype(F32)
        s_new = jnp.sum(iq.astype(F32) * ikn, axis=1, keepdims=True)
        sc_rows.append(sc_row)
        sc_news.append(jnp.sum(jnp.maximum(s_new, 0.0) * iw, axis=0, keepdims=True))

    def count_ge(r, cand):
        hits = jnp.where(sc_rows[r] >= cand, 1.0, 0.0)
        return jnp.sum(hits, axis=1, keepdims=True) + jnp.where(sc_news[r] >= cand, 1.0, 0.0)

    ts = _kth_largest_by_digits([functools.partial(count_ge, r) for r in range(rb)], top_k,
                                SAMPLE_SEARCH_BITS)

    def total(x):
        return jnp.sum(jnp.sum(x, axis=1, keepdims=True), axis=0, keepdims=True)

    for r in range(rb):
        t, sc, sc_new = ts[r], sc_ref[r], sc_news[r]
        n_gt = total(jnp.where(sc > t, 1.0, 0.0)) + jnp.where(sc_new > t, 1.0, 0.0)
        need = float(top_k) - n_gt
        eq = sc == t
        eqf = jnp.where(eq, 1.0, 0.0)
        in_row = _dot(eqf.astype(BF16), tri_ref[...])
        row_tot = jnp.broadcast_to(jnp.sum(eqf, axis=1, keepdims=True), sc.shape)
        rows_before = _dot(low_ref[...], row_tot.astype(BF16))
        sel = (sc > t) | (eq & (in_row + rows_before < need))
        sel_new = (sc_new > t) | ((sc_new == t) & (total(eqf) < need))
        bias = jnp.where(sel, 0.0, NEG_INF)
        bias_row = jnp.concatenate([bias[p:p + 1, :] for p in range(n_pages)], axis=1)
        bias_new = jnp.where(sel_new, 0.0, NEG_INF)

        q = q_ref[r].astype(BF16)
        qf = q.astype(F32)
        for g in range(N_KV_HEADS):
            head_rows = pl.ds(g, past, stride=N_KV_HEADS)
            kg = kbuf[slot, r, head_rows, :].astype(BF16)
            vg = vbuf[slot, r, head_rows, :].astype(BF16)
            kn = kn_ref[r, g:g + 1, :].astype(BF16).astype(F32)
            vn = vn_ref[r, g:g + 1, :].astype(BF16).astype(F32)
            s = _dot_nt(q, kg) + bias_row
            sn = jnp.sum(qf * kn, axis=1, keepdims=True) + bias_new
            m = jnp.maximum(jnp.max(s, axis=1, keepdims=True), sn)
            p = jnp.exp(s - m)
            pn = jnp.exp(sn - m)
            l = jnp.sum(p, axis=1, keepdims=True) + pn
            o = (_dot(p.astype(BF16), vg) + pn * vn) / l
            o_ref[r, g * GROUP:(g + 1) * GROUP, :] = o[g * GROUP:(g + 1) * GROUP]


def _sample_attn(page_table, q, iq, iw, k_new, v_new, ik_new, cache_k, cache_v, cache_ik):
    nreq, n_pages = page_table.shape
    n_phys, page = cache_k.shape[0], cache_k.shape[1]
    top_k = min(TOP_K_MAX, (n_pages * page + 1) // 4)
    assert top_k < n_pages * page + 1
    ck = cache_k.reshape(n_phys, page * N_KV_HEADS, HEAD_DIM)
    cv = cache_v.reshape(n_phys, page * N_KV_HEADS, HEAD_DIM)
    cik = jnp.swapaxes(cache_ik, 1, 2)
    tri = jnp.triu(jnp.ones((page, page), BF16), k=1)
    low = jnp.tril(jnp.ones((n_pages, n_pages), BF16), k=-1)
    rb = SAMPLE_REQS
    assert nreq % rb == 0
    pad_rows = lambda a: jnp.pad(a.astype(F32), ((0, 0), (0, MXU_ROWS - a.shape[1]), (0, 0)))
    per_req = lambda *shape: pl.BlockSpec((rb,) + shape, lambda b, pt: (b,) + (0,) * len(shape))
    const = lambda *shape: pl.BlockSpec(shape, lambda b, pt: (0,) * len(shape))
    any_spec = pl.BlockSpec(memory_space=pl.ANY)
    grid_spec = pltpu.PrefetchScalarGridSpec(
        num_scalar_prefetch=1,
        grid=(nreq // rb,),
        in_specs=[per_req(MXU_ROWS, HEAD_DIM), per_req(MXU_ROWS, IDX_DIM), per_req(MXU_ROWS, 1),
                  per_req(N_KV_HEADS, HEAD_DIM), per_req(N_KV_HEADS, HEAD_DIM), per_req(1, IDX_DIM),
                  any_spec, any_spec, any_spec, const(page, page), const(n_pages, n_pages)],
        out_specs=per_req(N_HEADS, HEAD_DIM),
        scratch_shapes=[
            pltpu.VMEM((2, rb, n_pages * page * N_KV_HEADS, HEAD_DIM), F32),
            pltpu.VMEM((2, rb, n_pages * page * N_KV_HEADS, HEAD_DIM), F32),
            pltpu.VMEM((2, rb, IDX_DIM, n_pages * page), F32),
            pltpu.SemaphoreType.DMA((3, 2)),
            pltpu.VMEM((rb, n_pages, page), F32),
        ],
    )
    out = pl.pallas_call(
        functools.partial(_sample_attn_kernel, n_pages, page, top_k, rb),
        grid_spec=grid_spec,
        out_shape=jax.ShapeDtypeStruct((nreq, N_HEADS, HEAD_DIM), F32),
        compiler_params=pltpu.CompilerParams(
            dimension_semantics=("arbitrary",), vmem_limit_bytes=VMEM_LIMIT_BYTES),
        name="sample_attn",
    )(page_table,
      pad_rows(q.reshape(nreq, N_HEADS, HEAD_DIM)),
      pad_rows(jnp.transpose(iq, (1, 0, 2))),
      pad_rows(iw.reshape(nreq, IDX_HEADS, 1)),
      k_new.reshape(nreq, N_KV_HEADS, HEAD_DIM), v_new.reshape(nreq, N_KV_HEADS, HEAD_DIM),
      ik_new.reshape(nreq, 1, IDX_DIM),
      ck, cv, cik, tri, low)
    return out.reshape(nreq, N_HEADS * HEAD_DIM)


def _merge_stage(x, attn, oa, sgb, wao_ref, wo_ref, gmem_ref, wmq_ref):
    out_b = _dot(attn, wao_ref[...])
    merged = oa.astype(F32) + sgb.astype(F32) * out_b
    x1 = x + _dot(merged.astype(BF16), wo_ref[...])
    hm = _rms(x1, gmem_ref[...]).astype(BF16)
    return x1, _dot(hm, wmq_ref[...]) * MEM_SCALE


def _ffn_stage(x1, mem_o, wmo_ref, gffn_ref, wg_ref, wu_ref, wd_ref, gfin_ref):
    x2 = x1 + _dot(mem_o.astype(BF16), wmo_ref[...])
    hf = _rms(x2, gffn_ref[...]).astype(BF16)
    acc = jnp.zeros_like(x2)
    for c in range(D_FF // FF_CHUNK):
        cols = slice(c * FF_CHUNK, (c + 1) * FF_CHUNK)
        gate = _dot(hf, wg_ref[:, cols])
        f = gate * _sigmoid(gate) * _dot(hf, wu_ref[:, cols])
        acc = acc + _dot(f.astype(BF16), wd_ref[cols, :])
    return _rms(x2 + acc, gfin_ref[...])


def _post_prompt_kernel(x_ref, attn_ref, oa_ref, sgb_ref, mk_ref, mv_ref, wao_ref, wo_ref, gmem_ref,
                        wmq_ref, wmo_ref, gffn_ref, wg_ref, wu_ref, wd_ref, gfin_ref, y_ref, mo_ref):
    x1, qm = _merge_stage(x_ref[...], attn_ref[...], oa_ref[...], sgb_ref[...],
                          wao_ref, wo_ref, gmem_ref, wmq_ref)
    for h in range(MEM_HEADS):
        cols = slice(h * MEM_HEAD_DIM, (h + 1) * MEM_HEAD_DIM)
        s = _dot_nt(qm[:, cols].astype(BF16), mk_ref[:, cols])
        p = jnp.exp(s - jnp.max(s, axis=1, keepdims=True))
        o = _dot(p.astype(BF16), mv_ref[:, cols]) / jnp.sum(p, axis=1, keepdims=True)
        mo_ref[:, cols] = o.astype(BF16)
    y_ref[...] = _ffn_stage(x1, mo_ref[...], wmo_ref, gffn_ref, wg_ref, wu_ref, wd_ref, gfin_ref)


def _post_prompt(x, attn, oa, sgb, mk, mv, w, *, batch, seq, tm):
    nt = seq // tm
    n_mem = mk.shape[0] // batch
    tok = lambda width: pl.BlockSpec((tm, width), lambda b, t: (b * nt + t, 0))
    mem = pl.BlockSpec((n_mem, D_MODEL), lambda b, t: (b, 0))
    sq = _const_spec((D_MODEL, D_MODEL))
    gain = _const_spec((1, D_MODEL))
    return pl.pallas_call(
        _post_prompt_kernel,
        grid=(batch, nt),
        in_specs=[tok(D_MODEL), tok(D_MODEL), tok(D_MODEL), tok(D_MODEL), mem, mem,
                  sq, sq, gain, sq, sq, gain,
                  _const_spec((D_MODEL, D_FF)), _const_spec((D_MODEL, D_FF)), _const_spec((D_FF, D_MODEL)),
                  gain],
        out_specs=tok(D_MODEL),
        out_shape=jax.ShapeDtypeStruct((batch * seq, D_MODEL), F32),
        scratch_shapes=[pltpu.VMEM((tm, D_MODEL), BF16)],
        compiler_params=pltpu.CompilerParams(
            dimension_semantics=("arbitrary", "arbitrary"), vmem_limit_bytes=VMEM_LIMIT_BYTES),
        name="post_prompt",
    )(x, attn, oa, sgb, mk, mv, w["wao"], w["wo"], w["gmem"], w["wmq"], w["wmo"], w["gffn"],
      w["wg"], w["wu"], w["wd"], w["gfin"])


def _merge_sample_kernel(x_ref, attn_ref, oa_ref, sgb_ref, wao_ref, wo_ref, gmem_ref, wmq_ref,
                         x1_ref, qm_ref):
    x1, qm = _merge_stage(x_ref[...], attn_ref[...].astype(BF16), oa_ref[...], sgb_ref[...],
                          wao_ref, wo_ref, gmem_ref, wmq_ref)
    x1_ref[...] = x1
    qm_ref[...] = qm


def _mem_sample_kernel(rb, q_ref, mk_ref, mv_ref, o_ref):
    for r in range(rb):
        s = jnp.sum(mk_ref[r] * q_ref[r][None], axis=2, keepdims=True)
        p = jnp.exp(s - jnp.max(s, axis=0, keepdims=True))
        o_ref[r] = jnp.sum(p * mv_ref[r], axis=0) / jnp.sum(p, axis=0)


def _ffn_sample_kernel(x1_ref, mo_ref, wmo_ref, gffn_ref, wg_ref, wu_ref, wd_ref, gfin_ref, y_ref):
    y_ref[...] = _ffn_stage(x1_ref[...], mo_ref[...], wmo_ref, gffn_ref, wg_ref, wu_ref, wd_ref, gfin_ref)


def _post_sample(x, attn, oa, sgb, mem_k, mem_v, w):
    n = x.shape[0]
    n_mem = mem_k.shape[1]
    full = lambda *shape: _const_spec(shape)
    act = full(n, D_MODEL)
    sq = full(D_MODEL, D_MODEL)
    gain = full(1, D_MODEL)
    params = pltpu.CompilerParams(dimension_semantics=("arbitrary",), vmem_limit_bytes=VMEM_LIMIT_BYTES)
    x1, qm = pl.pallas_call(
        _merge_sample_kernel, grid=(1,),
        in_specs=[act, act, act, act, sq, sq, gain, sq], out_specs=[act, act],
        out_shape=[jax.ShapeDtypeStruct((n, D_MODEL), F32)] * 2,
        compiler_params=params, name="merge_sample",
    )(x, attn, oa, sgb, w["wao"], w["wo"], w["gmem"], w["wmq"])

    rb = MEM_REQS
    assert n % rb == 0
    heads = pl.BlockSpec((rb, MEM_HEADS, MEM_HEAD_DIM), lambda i: (i, 0, 0))
    mem = pl.BlockSpec((rb, n_mem, MEM_HEADS, MEM_HEAD_DIM), lambda i: (i, 0, 0, 0))
    mem_o = pl.pallas_call(
        functools.partial(_mem_sample_kernel, rb), grid=(n // rb,),
        in_specs=[heads, mem, mem], out_specs=heads,
        out_shape=jax.ShapeDtypeStruct((n, MEM_HEADS, MEM_HEAD_DIM), F32),
        compiler_params=params, name="mem_sample",
    )(qm.reshape(n, MEM_HEADS, MEM_HEAD_DIM), mem_k, mem_v)

    return pl.pallas_call(
        _ffn_sample_kernel, grid=(1,),
        in_specs=[act, act, sq, gain, full(D_MODEL, D_FF), full(D_MODEL, D_FF), full(D_FF, D_MODEL), gain],
        out_specs=act, out_shape=jax.ShapeDtypeStruct((n, D_MODEL), F32),
        compiler_params=params, name="ffn_sample",
    )(x1, mem_o.reshape(n, D_MODEL), w["wmo"], w["gffn"], w["wg"], w["wu"], w["wd"], w["gfin"])


def _memory_kv_kernel(mem_ref, g_ref, wk_ref, wv_ref, kf_ref, vf_ref, kb_ref, vb_ref):
    m = _rms(mem_ref[...], g_ref[...]).astype(BF16)
    k = _dot(m, wk_ref[...])
    v = _dot(m, wv_ref[...])
    kf_ref[...] = k
    vf_ref[...] = v
    kb_ref[...] = k.astype(BF16)
    vb_ref[...] = v.astype(BF16)


def _memory_kv(mem, g, wk, wv):
    n = mem.shape[0]
    full = lambda *shape: _const_spec(shape)
    act = full(n, D_MODEL)
    sq = full(D_MODEL, D_MODEL)
    return pl.pallas_call(
        _memory_kv_kernel, grid=(1,),
        in_specs=[act, full(1, D_MODEL), sq, sq], out_specs=[act] * 4,
        out_shape=[jax.ShapeDtypeStruct((n, D_MODEL), F32)] * 2 + [jax.ShapeDtypeStruct((n, D_MODEL), BF16)] * 2,
        compiler_params=pltpu.CompilerParams(
            dimension_semantics=("arbitrary",), vmem_limit_bytes=VMEM_LIMIT_BYTES),
        name="memory_kv",
    )(mem, g, wk, wv)


def _prep_w_in(w_in):
    w_t = w_in.T
    main = jnp.pad(w_t[:IDX_RAW_END].astype(BF16), ((0, W_COLS - IDX_RAW_END), (0, 0)))
    return main, w_t[IDX_RAW_END:].astype(BF16)


def kernel(x_prompt, x_sample, mem_prompt, cache_k, cache_v, cache_idx_k, cache_mem_k, cache_mem_v, state_conv, page_table, g_mix, w_in, conv_w, w_conv_out, w_attn_out, w_o, g_mem, g_mem_kv, w_mq, w_mk, w_mv, w_mo, g_ffn, w_gate, w_up, w_down, g_final):
    depth = w_in.shape[0]
    assert depth == 1, "single-layer step"
    batch, seq, _ = x_prompt.shape
    nreq, dec_seq, _ = x_sample.shape
    assert dec_seq == 1
    n_mem = mem_prompt.shape[1]
    l = 0
    bf = lambda a: a.astype(BF16)
    row = lambda a: a.reshape(1, -1)

    w_main, w_gates = _prep_w_in(w_in[l])
    wco = bf(w_conv_out[l])
    w = dict(wao=bf(w_attn_out[l]), wo=bf(w_o[l]), gmem=row(g_mem[l]), wmq=bf(w_mq[l]), wmo=bf(w_mo[l]),
             gffn=row(g_ffn[l]), wg=bf(w_gate[l]), wu=bf(w_up[l]), wd=bf(w_down[l]), gfin=row(g_final))

    xp = x_prompt.reshape(batch * seq, D_MODEL)
    mkf, mvf, mkb, mvb = _memory_kv(mem_prompt.reshape(batch * n_mem, D_MODEL), row(g_mem_kv[l]),
                                    bf(w_mk[l]), bf(w_mv[l]))
    (kf, vf, ikf, oa, sgb, qt, kb, vt, iqt, ikb, iwt, conv_p) = _mixer_in(
        xp, row(g_mix[l]), w_main, w_gates, conv_w[l], wco, batch=batch, seq=seq, tm=KEY_CHUNK)
    attn = _prompt_attn(qt, iqt, iwt, kb, vt, ikb, batch=batch, seq=seq)
    yp = _post_prompt(xp, attn, oa, sgb, mkb, mvb, w, batch=batch, seq=seq, tm=POST_TILE)

    xs = x_sample.reshape(nreq, D_MODEL)
    st = state_conv[l]
    (kf_s, vf_s, ikf_s, oa_s, sgb_s, q_s, iq_s, iw_s, u_s) = _mixer_in(
        xs, row(g_mix[l]), w_main, w_gates, conv_w[l], wco, batch=nreq, seq=1, tm=nreq, state=(st[:, 0], st[:, 1]))
    attn_s = _sample_attn(page_table, q_s, iq_s, iw_s, kf_s, vf_s, ikf_s,
                          cache_k[l], cache_v[l], cache_idx_k[l])
    ys = _post_sample(xs, attn_s, oa_s, sgb_s, cache_mem_k[l], cache_mem_v[l], w)

    return (
        yp.reshape(batch, seq, D_MODEL),
        ys.reshape(nreq, 1, D_MODEL),
        kf.reshape(1, batch, seq, N_KV_HEADS, HEAD_DIM),
        vf.reshape(1, batch, seq, N_KV_HEADS, HEAD_DIM),
        ikf.reshape(1, batch, seq, IDX_DIM),
        conv_p.reshape(1, batch, CONV_WIDTH - 1, D_CONV),
        mkf.reshape(1, batch, n_mem, MEM_HEADS, MEM_HEAD_DIM),
        mvf.reshape(1, batch, n_mem, MEM_HEADS, MEM_HEAD_DIM),
        kf_s.reshape(1, nreq, 1, N_KV_HEADS, HEAD_DIM),
        vf_s.reshape(1, nreq, 1, N_KV_HEADS, HEAD_DIM),
        ikf_s.reshape(1, nreq, 1, IDX_DIM),
        jnp.stack([st[:, 1], u_s], axis=1).reshape(1, nreq, CONV_WIDTH - 1, D_CONV),
    )
```
